```python
import jax, jax.numpy as jnp
from jax import lax
import numpy as np

D_MODEL = 2048
BATCH = 2
SEQ = 4096
DEPTH = 2

GRID_W = 64
CTX_LEN = 256
EPS = 1e-6

QK_NOPE = 128
QK_ROPE = 64
V_DIM = 128
N_HEADS = D_MODEL // V_DIM
Q_LORA = D_MODEL // 4
KV_LORA = D_MODEL // 4
ROPE_THETA = 10000.0
ATTN_SCALE = (QK_NOPE + QK_ROPE) ** -0.5
Q_BLOCK = 128

POOL_WINDOWS = (2, 4, 8, 16)
N_POOL_GROUPS = len(POOL_WINDOWS)
POOL_WIDTH = D_MODEL // 2
POOL_GROUP = POOL_WIDTH // N_POOL_GROUPS
POOL_OUT_GROUP = D_MODEL // N_POOL_GROUPS

N_BRANCH = 2
OFF_Q = 0
OFF_KV = OFF_Q + Q_LORA
OFF_KR = OFF_KV + KV_LORA
OFF_POOL = OFF_KR + QK_ROPE
OFF_GATE = OFF_POOL + POOL_WIDTH
IN_WIDTH = OFF_GATE + N_BRANCH * D_MODEL

D_FF = 7 * D_MODEL // 2
N_EXPERTS = 8
TOP_K = 2
D_EXPERT = 7 * D_MODEL // 2
N_DENSE = (DEPTH + 1) // 2
N_MOE = DEPTH // 2

kernel_name = "hybrid_mla_pool_moe_dit_block"


def rmsnorm(x, g):
    xf = x.astype(jnp.float32)
    y = xf * lax.rsqrt(jnp.mean(xf * xf, axis=-1, keepdims=True) + EPS) * g.astype(jnp.float32)
    return y.astype(x.dtype)


def modulate(h, shift, scale):
    return h * (1.0 + scale) + shift


def axial_rope(n_tokens):
    rows = n_tokens // GRID_W
    row_ids = jnp.repeat(jnp.arange(rows, dtype=jnp.float32), GRID_W)
    col_ids = jnp.tile(jnp.arange(GRID_W, dtype=jnp.float32), rows)
    n_freq = QK_ROPE // 4
    inv = ROPE_THETA ** (-jnp.arange(n_freq, dtype=jnp.float32) / n_freq)
    ang = jnp.concatenate([row_ids[:, None] * inv, col_ids[:, None] * inv], axis=-1)
    return jnp.cos(ang), jnp.sin(ang)


def apply_rope(x, cos, sin):
    half = x.shape[-1] // 2
    x1 = x[..., :half].astype(jnp.float32)
    x2 = x[..., half:].astype(jnp.float32)
    return jnp.concatenate([x1 * cos - x2 * sin, x1 * sin + x2 * cos], axis=-1).astype(x.dtype)


def q_heads(q_lat, g_q, w_q_up):
    B, L = q_lat.shape[:2]
    q = (rmsnorm(q_lat, g_q) @ w_q_up).reshape(B, L, N_HEADS, QK_NOPE + QK_ROPE)
    return q[..., :QK_NOPE], q[..., QK_NOPE:]


def kv_heads(kv_lat, g_kv, w_kv_up):
    B, L = kv_lat.shape[:2]
    kv = (rmsnorm(kv_lat, g_kv) @ w_kv_up).reshape(B, L, N_HEADS, QK_NOPE + V_DIM)
    return kv[..., :QK_NOPE], kv[..., QK_NOPE:]


def mla_attend(qn, qr, kn, kr, v):
    s = jnp.einsum('bqhd,bkhd->bhqk', qn, kn) + jnp.einsum('bqhr,bkr->bhqk', qr, kr)
    p = jax.nn.softmax(s.astype(jnp.float32) * ATTN_SCALE, axis=-1).astype(v.dtype)
    o = jnp.einsum('bhqk,bkhd->bqhd', p, v)
    return o.reshape(o.shape[0], o.shape[1], N_HEADS * V_DIM)


def latent_attention(qn, qr, kn, kr, v):
    B, S = qn.shape[:2]
    nb = S // Q_BLOCK
    qn_b = qn.reshape(B, nb, Q_BLOCK, N_HEADS, QK_NOPE).swapaxes(0, 1)
    qr_b = qr.reshape(B, nb, Q_BLOCK, N_HEADS, QK_ROPE).swapaxes(0, 1)
    out = lax.map(lambda q: mla_attend(q[0], q[1], kn, kr, v), (qn_b, qr_b))
    return out.swapaxes(0, 1).reshape(B, S, N_HEADS * V_DIM)


def multiscale_pool(u, w_pool, pool_scale):
    B, L, _ = u.shape
    ug = u.reshape(B, L, N_POOL_GROUPS, POOL_GROUP)
    cs = jnp.concatenate([jnp.zeros((B, 1, N_POOL_GROUPS, POOL_GROUP), jnp.float32),
                          jnp.cumsum(ug.astype(jnp.float32), axis=1)], axis=1)
    t = jnp.arange(L)
    mixed = []
    for gi, w in enumerate(POOL_WINDOWS):
        lo = jnp.clip(t - w // 2, 0, L)
        hi = jnp.clip(t - w // 2 + w, 0, L)
        cnt = (hi - lo).astype(jnp.float32)[None, :, None]
        mean = (cs[:, hi, gi] - cs[:, lo, gi]) / cnt
        mixed.append(mean - ug[:, :, gi].astype(jnp.float32))
    pooled = jnp.stack(mixed, axis=2).astype(u.dtype)
    out = jnp.einsum('blgc,gcd->blgd', pooled, w_pool).reshape(B, L, D_MODEL)
    return out * pool_scale


def merge_branches(attn, pool, gate_logits, b_gate, w_out):
    g = jax.nn.sigmoid((gate_logits + b_gate).astype(jnp.float32)).astype(attn.dtype)
    return (g[..., :D_MODEL] * attn + g[..., D_MODEL:] * pool) @ w_out


def swiglu(h, wg, wu, wd):
    return (jax.nn.silu(h @ wg) * (h @ wu)) @ wd


def moe_swiglu(h, w_router, b_router, wg, wu, wd):
    logits = (h @ w_router).astype(jnp.float32) + b_router.astype(jnp.float32)
    probs = jax.nn.softmax(logits, axis=-1)
    top_p, top_i = lax.top_k(probs, TOP_K)
    top_p = top_p / jnp.sum(top_p, axis=-1, keepdims=True)
    combine = jnp.sum(jax.nn.one_hot(top_i, N_EXPERTS, dtype=jnp.float32) * top_p[..., None], axis=-2)
    y = jnp.zeros(h.shape[:-1] + (D_MODEL,), h.dtype)
    for e in range(N_EXPERTS):
        y = y + combine[..., e:e + 1].astype(h.dtype) * swiglu(h, wg[e], wu[e], wd[e])
    return y


def setup_inputs(seed: int = 0) -> dict:
    key = jax.random.key(seed)
    ks = jax.random.split(key, 32)
    f32 = jnp.float32

    def nrm(k, shape, scale):
        return jax.random.normal(k, shape, f32) * scale

    def gain(k, shape):
        return 1.0 + 0.05 * jax.random.normal(k, shape, f32)

    D = D_MODEL
    return {
        "x": nrm(ks[0], (BATCH, SEQ, D), 1.0),
        "c": nrm(ks[1], (BATCH, D), 1.0),
        "ctx": nrm(ks[2], (BATCH, CTX_LEN, D), 1.0),
        "c_ctx": nrm(ks[3], (D,), 1.0),
        "w_mod": nrm(ks[4], (DEPTH, D, 6 * D), D ** -0.5),
        "b_mod": nrm(ks[5], (DEPTH, 6 * D), 0.02),
        "g_mix_pre": gain(ks[6], (DEPTH, D)),
        "g_mix_post": gain(ks[7], (DEPTH, D)),
        "g_ffn_pre": gain(ks[8], (DEPTH, D)),
        "g_ffn_post": gain(ks[9], (DEPTH, D)),
        "w_in": nrm(ks[10], (DEPTH, D, IN_WIDTH), D ** -0.5),
        "b_gate": nrm(ks[11], (DEPTH, N_BRANCH * D), 0.02),
        "g_q_lat": gain(ks[12], (DEPTH, Q_LORA)),
        "g_kv_lat": gain(ks[13], (DEPTH, KV_LORA)),
        "w_q_up": nrm(ks[14], (DEPTH, Q_LORA, N_HEADS * (QK_NOPE + QK_ROPE)), Q_LORA ** -0.5),
        "w_kv_up": nrm(ks[15], (DEPTH, KV_LORA, N_HEADS * (QK_NOPE + V_DIM)), KV_LORA ** -0.5),
        "w_pool": nrm(ks[16], (DEPTH, N_POOL_GROUPS, POOL_GROUP, POOL_OUT_GROUP), POOL_GROUP ** -0.5),
        "pool_scale": gain(ks[17], (DEPTH, D)),
        "w_out": nrm(ks[18], (DEPTH, D, D), D ** -0.5),
        "w_ff_gate": nrm(ks[19], (N_DENSE, D, D_FF), D ** -0.5),
        "w_ff_up": nrm(ks[20], (N_DENSE, D, D_FF), D ** -0.5),
        "w_ff_down": nrm(ks[21], (N_DENSE, D_FF, D), D_FF ** -0.5),
        "w_router": nrm(ks[22], (N_MOE, D, N_EXPERTS), D ** -0.5),
        "b_router": nrm(ks[23], (N_MOE, N_EXPERTS), 0.01),
        "w_exp_gate": nrm(ks[24], (N_MOE, N_EXPERTS, D, D_EXPERT), D ** -0.5),
        "w_exp_up": nrm(ks[25], (N_MOE, N_EXPERTS, D, D_EXPERT), D ** -0.5),
        "w_exp_down": nrm(ks[26], (N_MOE, N_EXPERTS, D_EXPERT, D), D_EXPERT ** -0.5),
    }


def reference(x, c, ctx, c_ctx, w_mod, b_mod, g_mix_pre, g_mix_post, g_ffn_pre, g_ffn_post,
              w_in, b_gate, g_q_lat, g_kv_lat, w_q_up, w_kv_up, w_pool, pool_scale, w_out,
              w_ff_gate, w_ff_up, w_ff_down, w_router, b_router, w_exp_gate, w_exp_up, w_exp_down):
    S = x.shape[1]
    cos, sin = axial_rope(S)
    h_ctx = ctx

    def channel_mixer(l, h):
        i = l // 2
        if l % 2 == 0:
            return swiglu(h, w_ff_gate[i], w_ff_up[i], w_ff_down[i])
        return moe_swiglu(h, w_router[i], b_router[i], w_exp_gate[i], w_exp_up[i], w_exp_down[i])

    for l in range(DEPTH):
        last = l == DEPTH - 1
        mod_x = (jax.nn.silu(c) @ w_mod[l] + b_mod[l])[:, None, :]
        mod_c = (jax.nn.silu(c_ctx) @ w_mod[l] + b_mod[l])[None, None, :]
        sh1, sc1, g1, sh2, sc2, g2 = jnp.split(mod_x, 6, axis=-1)
        csh1, csc1, cg1, csh2, csc2, cg2 = jnp.split(mod_c, 6, axis=-1)

        h = modulate(rmsnorm(x, g_mix_pre[l]), sh1, sc1)
        hc = modulate(rmsnorm(h_ctx, g_mix_pre[l]), csh1, csc1)

        p = h @ w_in[l]
        qn, qr = q_heads(p[..., OFF_Q:OFF_KV], g_q_lat[l], w_q_up[l])
        qr = apply_rope(qr, cos[:, None, :], sin[:, None, :])
        kn, v = kv_heads(p[..., OFF_KV:OFF_KR], g_kv_lat[l], w_kv_up[l])
        kr = apply_rope(p[..., OFF_KR:OFF_POOL], cos, sin)

        if last:
            pc_kv = hc @ w_in[l][:, OFF_KV:OFF_POOL]
        else:
            pc = hc @ w_in[l]
            pc_kv = pc[..., OFF_KV:OFF_POOL]
        kn_c, v_c = kv_heads(pc_kv[..., :KV_LORA], g_kv_lat[l], w_kv_up[l])
        kr_c = pc_kv[..., KV_LORA:]

        kn_all = jnp.concatenate([kn_c, kn], axis=1)
        kr_all = jnp.concatenate([kr_c, kr], axis=1)
        v_all = jnp.concatenate([v_c, v], axis=1)
        attn = latent_attention(qn, qr, kn_all, kr_all, v_all)
        pool = multiscale_pool(p[..., OFF_POOL:OFF_GATE], w_pool[l], pool_scale[l])
        y = merge_branches(attn, pool, p[..., OFF_GATE:], b_gate[l], w_out[l])
        x = x + g1 * rmsnorm(y, g_mix_post[l])

        if not last:
            qn_c, qr_c = q_heads(pc[..., OFF_Q:OFF_KV], g_q_lat[l], w_q_up[l])
            attn_c = mla_attend(qn_c, qr_c, kn_c, kr_c, v_c)
            pool_c = multiscale_pool(pc[..., OFF_POOL:OFF_GATE], w_pool[l], pool_scale[l])
            yc = merge_branches(attn_c, pool_c, pc[..., OFF_GATE:], b_gate[l], w_out[l])
            h_ctx = h_ctx + cg1 * rmsnorm(yc, g_mix_post[l])

        h2 = modulate(rmsnorm(x, g_ffn_pre[l]), sh2, sc2)
        if last:
            y2 = channel_mixer(l, h2)
            x = x + g2 * rmsnorm(y2, g_ffn_post[l])
        else:
            C = h_ctx.shape[1]
            h2c = modulate(rmsnorm(h_ctx, g_ffn_pre[l]), csh2, csc2)
            y_all = channel_mixer(l, jnp.concatenate([h2c, h2], axis=1))
            x = x + g2 * rmsnorm(y_all[:, C:], g_ffn_post[l])
            h_ctx = h_ctx + cg2 * rmsnorm(y_all[:, :C], g_ffn_post[l])

    return x
```

```python
import functools

import jax
import jax.numpy as jnp
from jax import lax
from jax.experimental import pallas as pl
from jax.experimental.pallas import tpu as pltpu

F32 = jnp.float32
BF16 = jnp.bfloat16

D_MODEL = 2048
BATCH = 2
SEQ = 4096
DEPTH = 2
GRID_W = 64
CTX_LEN = 256
EPS = 1e-6

QK_NOPE = 128
QK_ROPE = 64
V_DIM = 128
N_HEADS = 16
Q_LORA = 512
KV_LORA = 512
ROPE_THETA = 10000.0
ATTN_SCALE = (QK_NOPE + QK_ROPE) ** -0.5

POOL_WINDOWS = (2, 4, 8, 16)
N_POOL_GROUPS = 4
POOL_WIDTH = 1024
POOL_GROUP = 256
POOL_OUT_GROUP = 512

OFF_KV = 512
OFF_KR = 1024
OFF_POOL = 1088
OFF_GATE = 2112

D_FF = 7168
N_EXPERTS = 8
TOP_K = 2

T_ROWS = CTX_LEN + SEQ
ROWS = BATCH * T_ROWS
ROW_TILE = 256
TILES_PER_BATCH = T_ROWS // ROW_TILE
N_TILES_ALL = BATCH * TILES_PER_BATCH
N_TILES_LATENT = BATCH * (SEQ // ROW_TILE)
HALO = 16
K_WIDTH = 256

MAIN_WIDTH = Q_LORA + KV_LORA + POOL_WIDTH + 2 * D_MODEL
COL_Q, COL_KV, COL_POOL, COL_GATE = 0, 512, 1024, 2048

FFN_TILE = 272
FFN_TILES_PER_SEG = 4
SEG_ROWS = FFN_TILE * FFN_TILES_PER_SEG
FFN_CHUNK = 256
MOE_SEGS = (SEQ * BATCH * TOP_K) // SEG_ROWS + N_EXPERTS

VMEM_LIMIT = 56 * 1024 * 1024


def _cparams(sem, vmem=None):
    return pltpu.CompilerParams(dimension_semantics=sem, vmem_limit_bytes=vmem)


def _tile_all(i):
    return i


def _tile_latent(i):
    per = SEQ // ROW_TILE
    return (i // per) * TILES_PER_BATCH + 1 + i % per


def _group_of_tile(t):
    return jnp.where(t % TILES_PER_BATCH == 0, BATCH, t // TILES_PER_BATCH)


def _mod_spec(layer, chunk, tile_fn):
    def index(i):
        return ((layer * 3 + _group_of_tile(tile_fn(i))) * 6 + chunk, 0, 0)
    return pl.BlockSpec((None, 1, D_MODEL), index)


def _vec_spec(layer):
    return lambda width: pl.BlockSpec((None, 1, width), lambda i: (layer, 0, 0))


def _rmsnorm(x, g):
    return x * lax.rsqrt(jnp.mean(x * x, axis=-1, keepdims=True) + EPS) * g


def _mod_kernel(c_ref, w_ref, b_ref, o_ref):
    c = c_ref[...]
    a = c * jax.nn.sigmoid(c)
    o_ref[...] = jnp.dot(a, w_ref[...], preferred_element_type=F32) + b_ref[...]


def _modulation(c8, w_mod, b_mod):
    n = w_mod.shape[2]
    tn = 1024
    return pl.pallas_call(
        _mod_kernel,
        grid=(DEPTH, n // tn),
        in_specs=[pl.BlockSpec((8, D_MODEL), lambda l, j: (0, 0)),
                  pl.BlockSpec((None, D_MODEL, tn), lambda l, j: (l, 0, j)),
                  pl.BlockSpec((None, 1, tn), lambda l, j: (l, 0, j))],
        out_specs=pl.BlockSpec((None, 8, tn), lambda l, j: (l, 0, j)),
        out_shape=jax.ShapeDtypeStruct((DEPTH, 8, n), F32),
        compiler_params=_cparams(("arbitrary", "arbitrary"), VMEM_LIMIT),
        name="modulation",
    )(c8, w_mod, b_mod.reshape(DEPTH, 1, n))


def _norm_mod_kernel(x_ref, g_ref, sh_ref, sc_ref, o_ref):
    y = _rmsnorm(x_ref[...], g_ref[...])
    o_ref[...] = (y * (1.0 + sc_ref[...]) + sh_ref[...]).astype(o_ref.dtype)


def _norm_mod(x, g, mods3, layer, sh_chunk, sc_chunk):
    row = pl.BlockSpec((ROW_TILE, D_MODEL), lambda i: (i, 0))
    return pl.pallas_call(
        _norm_mod_kernel,
        grid=(N_TILES_ALL,),
        in_specs=[row, _vec_spec(layer)(D_MODEL),
                  _mod_spec(layer, sh_chunk, _tile_all), _mod_spec(layer, sc_chunk, _tile_all)],
        out_specs=row,
        out_shape=jax.ShapeDtypeStruct((ROWS, D_MODEL), BF16),
        compiler_params=_cparams(("arbitrary",)),
        name="norm_mod",
    )(x, g, mods3, mods3)


def _matmul_kernel(a_ref, w_ref, o_ref):
    o_ref[...] = jnp.dot(a_ref[...], w_ref[...], preferred_element_type=F32).astype(o_ref.dtype)


def _matmul(a, w, tm, tn):
    m, k = a.shape
    n = w.shape[1]
    return pl.pallas_call(
        _matmul_kernel,
        grid=(n // tn, m // tm),
        in_specs=[pl.BlockSpec((tm, k), lambda j, i: (i, 0)),
                  pl.BlockSpec((k, tn), lambda j, i: (0, j))],
        out_specs=pl.BlockSpec((tm, tn), lambda j, i: (i, j)),
        out_shape=jax.ShapeDtypeStruct((m, n), BF16),
        compiler_params=_cparams(("arbitrary", "arbitrary"), VMEM_LIMIT),
        name="in_proj",
    )(a, w)


def _rope_mix(t):
    lane = lax.broadcasted_iota(jnp.int32, t.shape, 1)
    return jnp.where(lane < QK_ROPE, t + pltpu.roll(t, QK_ROPE, 1), 0.0)


def _rope_key_kernel(a_ref, w_ref, cs_ref, o_ref):
    y = jnp.dot(a_ref[...], w_ref[...], preferred_element_type=F32)
    o_ref[...] = _rope_mix(y * cs_ref[...]).astype(o_ref.dtype)


def _rope_key(h, w_kr2, cs):
    tm = 512
    return pl.pallas_call(
        _rope_key_kernel,
        grid=(ROWS // tm,),
        in_specs=[pl.BlockSpec((tm, D_MODEL), lambda i: (i, 0)),
                  pl.BlockSpec((D_MODEL, 128), lambda i: (0, 0)),
                  pl.BlockSpec((tm, 128), lambda i: (i, 0))],
        out_specs=pl.BlockSpec((tm, 128), lambda i: (i, 0)),
        out_shape=jax.ShapeDtypeStruct((ROWS, 128), BF16),
        compiler_params=_cparams(("arbitrary",)),
        name="rope_key",
    )(h, w_kr2, cs)


def _q_up_kernel(p_ref, g_ref, w_ref, cs_ref, q_ref):
    n = _rmsnorm(p_ref[...].astype(F32), g_ref[...]).astype(BF16)
    y = jnp.dot(n, w_ref[...], preferred_element_type=F32)
    cs = cs_ref[...]
    for h in range(N_HEADS):
        yh = y[:, h * K_WIDTH:(h + 1) * K_WIDTH]
        q_ref[h, :, :QK_NOPE] = yh[:, :QK_NOPE].astype(q_ref.dtype)
        q_ref[h, :, QK_NOPE:] = _rope_mix(yh[:, QK_NOPE:] * cs).astype(q_ref.dtype)


def _head_index(i):
    return (i // TILES_PER_BATCH, 0, i % TILES_PER_BATCH, 0)


def _q_up(p, g_q, w_q2, cs, layer):
    return pl.pallas_call(
        _q_up_kernel,
        grid=(N_TILES_ALL,),
        in_specs=[pl.BlockSpec((ROW_TILE, Q_LORA), lambda i: (i, COL_Q // Q_LORA)),
                  _vec_spec(layer)(Q_LORA),
                  pl.BlockSpec((Q_LORA, N_HEADS * K_WIDTH), lambda i: (0, 0)),
                  pl.BlockSpec((ROW_TILE, 128), lambda i: (i, 0))],
        out_specs=pl.BlockSpec((None, N_HEADS, ROW_TILE, K_WIDTH), _head_index),
        out_shape=jax.ShapeDtypeStruct((BATCH, N_HEADS, T_ROWS, K_WIDTH), BF16),
        compiler_params=_cparams(("arbitrary",), VMEM_LIMIT),
        name="q_up",
    )(p, g_q, w_q2, cs)


def _kv_up_kernel(p_ref, g_ref, w_ref, kr_ref, k_ref, v_ref):
    n = _rmsnorm(p_ref[...].astype(F32), g_ref[...]).astype(BF16)
    y = jnp.dot(n, w_ref[...], preferred_element_type=F32)
    kr = kr_ref[...]
    for h in range(N_HEADS):
        base = h * (QK_NOPE + V_DIM)
        k_ref[h, :, :QK_NOPE] = y[:, base:base + QK_NOPE].astype(k_ref.dtype)
        k_ref[h, :, QK_NOPE:] = kr
        v_ref[h] = y[:, base + QK_NOPE:base + QK_NOPE + V_DIM].astype(v_ref.dtype)


def _kv_up(p, g_kv, w_kv, kr, layer):
    return pl.pallas_call(
        _kv_up_kernel,
        grid=(N_TILES_ALL,),
        in_specs=[pl.BlockSpec((ROW_TILE, KV_LORA), lambda i: (i, COL_KV // KV_LORA)),
                  _vec_spec(layer)(KV_LORA),
                  pl.BlockSpec((KV_LORA, N_HEADS * (QK_NOPE + V_DIM)), lambda i: (0, 0)),
                  pl.BlockSpec((ROW_TILE, 128), lambda i: (i, 0))],
        out_specs=[pl.BlockSpec((None, N_HEADS, ROW_TILE, K_WIDTH), _head_index),
                   pl.BlockSpec((None, N_HEADS, ROW_TILE, V_DIM), _head_index)],
        out_shape=[jax.ShapeDtypeStruct((BATCH, N_HEADS, T_ROWS, K_WIDTH), BF16),
                   jax.ShapeDtypeStruct((BATCH, N_HEADS, T_ROWS, V_DIM), BF16)],
        compiler_params=_cparams(("arbitrary",), VMEM_LIMIT),
        name="kv_up",
    )(p, g_kv, w_kv, kr)


def _attend(q, k, v):
    s = lax.dot_general(q, k, (((1,), (1,)), ((), ())), preferred_element_type=F32) * ATTN_SCALE
    m = jnp.max(s, axis=-1, keepdims=True)
    e = jnp.exp(s - m)
    denom = jnp.sum(e, axis=-1, keepdims=True)
    o = jnp.dot(e.astype(BF16), v, preferred_element_type=F32)
    return o / denom


def _attention_kernel(with_ctx, q_ref, k_ref, v_ref, o_ref):
    def latent():
        o_ref[...] = _attend(q_ref[...], k_ref[...], v_ref[...]).astype(o_ref.dtype)

    if not with_ctx:
        latent()
        return

    @pl.when(pl.program_id(2) == 0)
    def _():
        o_ref[...] = _attend(q_ref[...], k_ref[:CTX_LEN, :], v_ref[:CTX_LEN, :]).astype(o_ref.dtype)

    pl.when(pl.program_id(2) > 0)(latent)


def _attention(q, k, v, with_ctx):
    q0 = 0 if with_ctx else 1
    n_q = TILES_PER_BATCH - q0
    return pl.pallas_call(
        functools.partial(_attention_kernel, with_ctx),
        grid=(BATCH, N_HEADS, n_q),
        in_specs=[pl.BlockSpec((None, None, ROW_TILE, K_WIDTH), lambda b, h, i: (b, h, i + q0, 0)),
                  pl.BlockSpec((None, None, T_ROWS, K_WIDTH), lambda b, h, i: (b, h, 0, 0)),
                  pl.BlockSpec((None, None, T_ROWS, V_DIM), lambda b, h, i: (b, h, 0, 0))],
        out_specs=pl.BlockSpec((None, ROW_TILE, V_DIM), lambda b, h, i: (b, i, h)),
        out_shape=jax.ShapeDtypeStruct((BATCH, n_q * ROW_TILE, D_MODEL), BF16),
        compiler_params=_cparams(("arbitrary", "arbitrary", "arbitrary"), VMEM_LIMIT),
        name="attention",
    )(q, k, v)


def _pool_kernel(tile_fn, prev_ref, cur_ref, next_ref, w_ref, ps_ref, o_ref, buf_ref):
    t = tile_fn(pl.program_id(0)) % TILES_PER_BATCH
    is_ctx = t == 0
    seg_len = jnp.where(is_ctx, CTX_LEN, SEQ)
    pos0 = jnp.where(is_ctx, 0, (t - 1) * ROW_TILE)
    has_prev = pos0 > 0
    has_next = pos0 + ROW_TILE < seg_len
    buf_ref[0:HALO, :] = jnp.where(has_prev, prev_ref[...].astype(F32), 0.0)
    buf_ref[HALO:HALO + ROW_TILE, :] = cur_ref[...].astype(F32)
    buf_ref[HALO + ROW_TILE:, :] = jnp.where(has_next, next_ref[...].astype(F32), 0.0)

    pos = pos0 + lax.broadcasted_iota(jnp.int32, (ROW_TILE, 1), 0)
    for g, win in enumerate(POOL_WINDOWS):
        cols = slice(g * POOL_GROUP, (g + 1) * POOL_GROUP)
        half = win // 2
        acc = buf_ref[HALO - half:HALO - half + ROW_TILE, cols]
        for j in range(1, win):
            acc = acc + buf_ref[HALO - half + j:HALO - half + j + ROW_TILE, cols]
        lo = jnp.maximum(pos - half, 0)
        hi = jnp.minimum(pos - half + win, seg_len)
        mean = acc / (hi - lo).astype(F32)
        pooled = (mean - buf_ref[HALO:HALO + ROW_TILE, cols]).astype(BF16)
        out = jnp.dot(pooled, w_ref[g], preferred_element_type=F32)
        ocols = slice(g * POOL_OUT_GROUP, (g + 1) * POOL_OUT_GROUP)
        o_ref[:, ocols] = (out * ps_ref[:, ocols]).astype(o_ref.dtype)


def _pool(p, w_pool, pool_scale, layer, tile_fn, n_tiles):
    per16 = ROW_TILE // HALO
    last16 = ROWS // HALO - 1
    col = COL_POOL // POOL_WIDTH
    return pl.pallas_call(
        functools.partial(_pool_kernel, tile_fn),
        grid=(n_tiles,),
        in_specs=[pl.BlockSpec((HALO, POOL_WIDTH), lambda i: (jnp.maximum(tile_fn(i) * per16 - 1, 0), col)),
                  pl.BlockSpec((ROW_TILE, POOL_WIDTH), lambda i: (tile_fn(i), col)),
                  pl.BlockSpec((HALO, POOL_WIDTH), lambda i: (jnp.minimum((tile_fn(i) + 1) * per16, last16), col)),
                  pl.BlockSpec((N_POOL_GROUPS, POOL_GROUP, POOL_OUT_GROUP), lambda i: (0, 0, 0)),
                  _vec_spec(layer)(D_MODEL)],
        out_specs=pl.BlockSpec((ROW_TILE, D_MODEL), lambda i: (i, 0)),
        out_shape=jax.ShapeDtypeStruct((n_tiles * ROW_TILE, D_MODEL), BF16),
        scratch_shapes=[pltpu.VMEM((ROW_TILE + 2 * HALO, POOL_WIDTH), F32)],
        compiler_params=_cparams(("arbitrary",)),
        name="pool",
    )(p, p, p, w_pool, pool_scale)


def _merge_kernel(with_router, ga_ref, gb_ref, attn_ref, pool_ref, ba_ref, bb_ref, w_ref, x_ref,
                  gpost_ref, g1_ref, gpre_ref, sh_ref, sc_ref, *rest):
    if with_router:
        wr_ref, br_ref, xo_ref, h_ref, lg_ref = rest
    else:
        xo_ref, h_ref = rest
    ga = jax.nn.sigmoid(ga_ref[...].astype(F32) + ba_ref[...])
    gb = jax.nn.sigmoid(gb_ref[...].astype(F32) + bb_ref[...])
    mixed = ga * attn_ref[...].astype(F32) + gb * pool_ref[...].astype(F32)
    y = jnp.dot(mixed.astype(BF16), w_ref[...], preferred_element_type=F32)
    x = x_ref[...] + g1_ref[...] * _rmsnorm(y, gpost_ref[...])
    xo_ref[...] = x
    h = _rmsnorm(x, gpre_ref[...]) * (1.0 + sc_ref[...]) + sh_ref[...]
    h_ref[...] = h.astype(h_ref.dtype)
    if with_router:
        lg_ref[...] = jnp.dot(h, wr_ref[...], preferred_element_type=F32,
                              precision=lax.Precision.HIGHEST) + br_ref[...]


def _merge(p, attn, pool, b_gate2, w_out, x, g_post, g_pre, mods3, layer, tile_fn, n_tiles,
           out_rows, out_tile_fn, router=None):
    vec = _vec_spec(layer)
    in_row = lambda i: (tile_fn(i), 0)
    out_row = lambda i: (out_tile_fn(i), 0)
    gate_a = COL_GATE // D_MODEL
    in_specs = [pl.BlockSpec((ROW_TILE, D_MODEL), lambda i: (tile_fn(i), gate_a)),
                pl.BlockSpec((ROW_TILE, D_MODEL), lambda i: (tile_fn(i), gate_a + 1)),
                pl.BlockSpec((ROW_TILE, D_MODEL), lambda i: (i, 0)),
                pl.BlockSpec((ROW_TILE, D_MODEL), lambda i: (i, 0)),
                pl.BlockSpec((None, 1, D_MODEL), lambda i: (layer * 2, 0, 0)),
                pl.BlockSpec((None, 1, D_MODEL), lambda i: (layer * 2 + 1, 0, 0)),
                pl.BlockSpec((D_MODEL, D_MODEL), lambda i: (0, 0)),
                pl.BlockSpec((ROW_TILE, D_MODEL), in_row),
                vec(D_MODEL), _mod_spec(layer, 2, tile_fn), vec(D_MODEL),
                _mod_spec(layer, 3, tile_fn), _mod_spec(layer, 4, tile_fn)]
    args = [p, p, attn, pool, b_gate2, b_gate2, w_out, x, g_post, mods3, g_pre, mods3, mods3]
    out_specs = [pl.BlockSpec((ROW_TILE, D_MODEL), out_row), pl.BlockSpec((ROW_TILE, D_MODEL), out_row)]
    out_shape = [jax.ShapeDtypeStruct((out_rows, D_MODEL), F32),
                 jax.ShapeDtypeStruct((out_rows, D_MODEL), BF16)]
    if router is not None:
        in_specs += [pl.BlockSpec((D_MODEL, 128), lambda i: (0, 0)), pl.BlockSpec((1, 128), lambda i: (0, 0))]
        args += list(router)
        out_specs.append(pl.BlockSpec((ROW_TILE, 128), out_row))
        out_shape.append(jax.ShapeDtypeStruct((out_rows, 128), F32))
    return pl.pallas_call(
        functools.partial(_merge_kernel, router is not None),
        grid=(n_tiles,),
        in_specs=in_specs,
        out_specs=out_specs,
        out_shape=out_shape,
        compiler_params=_cparams(("arbitrary",), VMEM_LIMIT),
        name="merge_out_proj",
    )(*args)


def _ffn_kernel(se_ref, st_ref, x_ref, wg_ref, wu_ref, wd_ref, o_ref):
    del se_ref
    s = pl.program_id(0)

    @pl.when(pl.program_id(1) == 0)
    def _():
        o_ref[...] = jnp.zeros_like(o_ref)

    def tile(i, carry):
        rows = pl.ds(pl.multiple_of(i * FFN_TILE, 16), FFN_TILE)
        x = x_ref[rows, :].astype(F32)
        g = jnp.dot(x, wg_ref[...], preferred_element_type=F32)
        u = jnp.dot(x, wu_ref[...], preferred_element_type=F32)
        a = (g * jax.nn.sigmoid(g)) * u
        o_ref[rows, :] += jnp.dot(a, wd_ref[...], preferred_element_type=F32)
        return carry

    lax.fori_loop(0, st_ref[s], tile, 0)


def _ffn(seg_expert, seg_tiles, xs, wg, wu, wd):
    n_seg = xs.shape[0] // SEG_ROWS
    grid_spec = pltpu.PrefetchScalarGridSpec(
        num_scalar_prefetch=2,
        grid=(n_seg, D_FF // FFN_CHUNK),
        in_specs=[pl.BlockSpec((SEG_ROWS, D_MODEL), lambda s, f, se, st: (s, 0)),
                  pl.BlockSpec((None, D_MODEL, FFN_CHUNK), lambda s, f, se, st: (se[s], 0, f)),
                  pl.BlockSpec((None, D_MODEL, FFN_CHUNK), lambda s, f, se, st: (se[s], 0, f)),
                  pl.BlockSpec((None, FFN_CHUNK, D_MODEL), lambda s, f, se, st: (se[s], f, 0))],
        out_specs=pl.BlockSpec((SEG_ROWS, D_MODEL), lambda s, f, se, st: (s, 0)),
    )
    return pl.pallas_call(
        _ffn_kernel,
        grid_spec=grid_spec,
        out_shape=jax.ShapeDtypeStruct((n_seg * SEG_ROWS, D_MODEL), F32),
        compiler_params=_cparams(("arbitrary", "arbitrary"), VMEM_LIMIT),
        name="swiglu_ffn",
    )(seg_expert, seg_tiles, xs, wg, wu, wd)


def _post_kernel(with_next, y_ref, x_ref, gpost_ref, g2_ref, *rest):
    if with_next:
        gpre_ref, sh_ref, sc_ref, xo_ref, h_ref = rest
    else:
        (xo_ref,) = rest
    x = x_ref[...] + g2_ref[...] * _rmsnorm(y_ref[...], gpost_ref[...])
    xo_ref[...] = x
    if with_next:
        h = _rmsnorm(x, gpre_ref[...]) * (1.0 + sc_ref[...]) + sh_ref[...]
        h_ref[...] = h.astype(h_ref.dtype)


def _post(y, x, g_post, mods3, layer, tile_fn, n_tiles, g_pre_next=None):
    rows = n_tiles * ROW_TILE
    row = pl.BlockSpec((ROW_TILE, D_MODEL), lambda i: (i, 0))
    in_specs = [row, row, _vec_spec(layer)(D_MODEL), _mod_spec(layer, 5, tile_fn)]
    args = [y, x, g_post, mods3]
    out_specs = [row]
    out_shape = [jax.ShapeDtypeStruct((rows, D_MODEL), F32)]
    if g_pre_next is not None:
        in_specs += [_vec_spec(layer + 1)(D_MODEL), _mod_spec(layer + 1, 0, tile_fn),
                     _mod_spec(layer + 1, 1, tile_fn)]
        args += [g_pre_next, mods3, mods3]
        out_specs.append(row)
        out_shape.append(jax.ShapeDtypeStruct((rows, D_MODEL), BF16))
    return pl.pallas_call(
        functools.partial(_post_kernel, g_pre_next is not None),
        grid=(n_tiles,),
        in_specs=in_specs,
        out_specs=out_specs,
        out_shape=out_shape,
        compiler_params=_cparams(("arbitrary",)),
        name="post_ffn",
    )(*args)


def _rope_table():
    pos = jnp.arange(SEQ, dtype=jnp.int32)
    row_ids = (pos // GRID_W).astype(F32)
    col_ids = (pos % GRID_W).astype(F32)
    n_freq = QK_ROPE // 4
    inv = ROPE_THETA ** (-jnp.arange(n_freq, dtype=F32) / n_freq)
    ang = jnp.concatenate([row_ids[:, None] * inv, col_ids[:, None] * inv], axis=-1)
    cos, sin = jnp.cos(ang), jnp.sin(ang)
    latent = jnp.concatenate([cos, cos, -sin, sin], axis=-1)
    ctx = jnp.concatenate([jnp.ones((CTX_LEN, QK_ROPE), F32), jnp.zeros((CTX_LEN, QK_ROPE), F32)], axis=-1)
    one = jnp.concatenate([ctx, latent], axis=0)
    return jnp.tile(one, (BATCH, 1))


def _swap_halves(w):
    half = w.shape[-1] // 2
    return jnp.concatenate([w[..., half:], w[..., :half]], axis=-1)


def _q_weights(w_q_up):
    w = w_q_up.reshape(Q_LORA, N_HEADS, QK_NOPE + QK_ROPE)
    rope = w[..., QK_NOPE:]
    w2 = jnp.concatenate([w[..., :QK_NOPE], rope, _swap_halves(rope)], axis=-1)
    return w2.reshape(Q_LORA, N_HEADS * K_WIDTH).astype(BF16)


def _routing(logits, b_router_unused=None):
    probs = jax.nn.softmax(logits, axis=-1)
    top_p, top_i = lax.top_k(probs, TOP_K)
    top_p = top_p / jnp.sum(top_p, axis=-1, keepdims=True)
    n_assign = top_i.size
    expert = top_i.reshape(n_assign)
    order = jnp.argsort(expert, stable=True)
    sorted_expert = expert[order]
    counts = jnp.sum(jax.nn.one_hot(expert, N_EXPERTS, dtype=jnp.int32), axis=0)
    group_start = jnp.cumsum(counts) - counts
    n_segs = (counts + SEG_ROWS - 1) // SEG_ROWS
    seg_start = jnp.cumsum(n_segs) - n_segs
    rank = jnp.arange(n_assign, dtype=jnp.int32) - group_start[sorted_expert]
    slot_sorted = seg_start[sorted_expert] * SEG_ROWS + rank
    src_token = jnp.zeros((MOE_SEGS * SEG_ROWS,), jnp.int32).at[slot_sorted].set(order // TOP_K)
    slot = jnp.zeros((n_assign,), jnp.int32).at[order].set(slot_sorted)
    seg_ids = jnp.arange(MOE_SEGS, dtype=jnp.int32)
    used = jnp.sum(n_segs)
    seg_expert = jnp.sum((seg_ids[:, None] >= (seg_start + n_segs)[None, :]).astype(jnp.int32), axis=1)
    last_used_expert = jnp.max(jnp.where(counts > 0, jnp.arange(N_EXPERTS), 0))
    seg_expert = jnp.where(seg_ids < used, jnp.minimum(seg_expert, N_EXPERTS - 1), last_used_expert)
    rows_left = counts[seg_expert] - (seg_ids - seg_start[seg_expert]) * SEG_ROWS
    seg_tiles = jnp.clip((rows_left + FFN_TILE - 1) // FFN_TILE, 0, FFN_TILES_PER_SEG)
    seg_tiles = jnp.where(seg_ids < used, seg_tiles, 0)
    return (seg_expert.astype(jnp.int32), seg_tiles.astype(jnp.int32), src_token,
            slot.reshape(-1, TOP_K), top_p)


def kernel(x, c, ctx, c_ctx, w_mod, b_mod, g_mix_pre, g_mix_post, g_ffn_pre, g_ffn_post, w_in, b_gate, g_q_lat, g_kv_lat, w_q_up, w_kv_up, w_pool, pool_scale, w_out, w_ff_gate, w_ff_up, w_ff_down, w_router, b_router, w_exp_gate, w_exp_up, w_exp_down):
    xr = jnp.concatenate([ctx, x], axis=1).reshape(ROWS, D_MODEL)

    c8 = jnp.concatenate([c, c_ctx[None, :], jnp.zeros((8 - BATCH - 1, D_MODEL), F32)], axis=0)
    mods = _modulation(c8, w_mod, b_mod)
    mods3 = mods[:, :BATCH + 1].reshape(DEPTH * 3 * 6, 1, D_MODEL)

    as_vec = lambda a: a.reshape(DEPTH, 1, a.shape[-1])
    g_mix_pre, g_mix_post, g_ffn_pre, g_ffn_post = map(as_vec, (g_mix_pre, g_mix_post, g_ffn_pre, g_ffn_post))
    g_q_lat, g_kv_lat, pool_scale = map(as_vec, (g_q_lat, g_kv_lat, pool_scale))
    b_gate2 = b_gate.reshape(DEPTH * 2, 1, D_MODEL)
    cs = _rope_table()

    h = _norm_mod(xr, g_mix_pre, mods3, 0, 0, 1)
    out = None
    for l in range(DEPTH):
        last = l == DEPTH - 1
        w_main = jnp.concatenate([w_in[l][:, :OFF_KR], w_in[l][:, OFF_POOL:]], axis=1).astype(BF16)
        w_kr = w_in[l][:, OFF_KR:OFF_POOL]
        w_kr2 = jnp.concatenate([w_kr, _swap_halves(w_kr)], axis=1).astype(BF16)

        p = _matmul(h, w_main, 512, 1024)
        kr = _rope_key(h, w_kr2, cs)
        q = _q_up(p, g_q_lat, _q_weights(w_q_up[l]), cs, l)
        k, v = _kv_up(p, g_kv_lat, w_kv_up[l].astype(BF16), kr, l)

        attn = _attention(q, k, v, with_ctx=not last)
        attn = attn.reshape(-1, D_MODEL)

        tile_fn, n_tiles = (_tile_latent, N_TILES_LATENT) if last else (_tile_all, N_TILES_ALL)
        pool = _pool(p, w_pool[l].astype(BF16), pool_scale, l, tile_fn, n_tiles)

        if not last:
            xr, h2 = _merge(p, attn, pool, b_gate2, w_out[l].astype(BF16), xr, g_mix_post, g_ffn_pre, mods3,
                            l, tile_fn, n_tiles, ROWS, _tile_all)
            n_seg = ROWS // SEG_ROWS
            seg_expert = jnp.zeros((n_seg,), jnp.int32)
            seg_tiles = jnp.full((n_seg,), FFN_TILES_PER_SEG, jnp.int32)
            y = _ffn(seg_expert, seg_tiles, h2, w_ff_gate, w_ff_up, w_ff_down)
            xr, h = _post(y, xr, g_ffn_post, mods3, l, _tile_all, N_TILES_ALL, g_pre_next=g_mix_pre)
        else:
            w_r = jnp.pad(w_router[0], ((0, 0), (0, 128 - N_EXPERTS)))
            b_r = jnp.pad(b_router[0], (0, 128 - N_EXPERTS)).reshape(1, 128)
            xl, h2, logits = _merge(p, attn, pool, b_gate2, w_out[l].astype(BF16), xr, g_mix_post, g_ffn_pre,
                                    mods3, l, tile_fn, n_tiles, BATCH * SEQ, _tile_all, router=(w_r, b_r))
            seg_expert, seg_tiles, src_token, slot, weight = _routing(logits[:, :N_EXPERTS])
            xs = jnp.take(h2, src_token, axis=0)
            ys = _ffn(seg_expert, seg_tiles, xs, w_exp_gate.reshape(N_EXPERTS, D_MODEL, D_FF),
                      w_exp_up.reshape(N_EXPERTS, D_MODEL, D_FF), w_exp_down.reshape(N_EXPERTS, D_FF, D_MODEL))
            y = (weight[:, 0:1] * jnp.take(ys, slot[:, 0], axis=0)
                 + weight[:, 1:2] * jnp.take(ys, slot[:, 1], axis=0))
            (out,) = _post(y, xl, g_ffn_post, mods3, l, _tile_latent, N_TILES_LATENT)
    return out.reshape(BATCH, SEQ, D_MODEL)
```

```python
import functools
import math

import jax
import jax.numpy as jnp
from jax import lax
from jax.experimental import pallas as pl
from jax.experimental.pallas import tpu as pltpu

F32 = jnp.float32
BF16 = jnp.bfloat16

D_MODEL = 2048
BATCH = 2
SEQ = 4096
DEPTH = 2
GRID_W = 64
CTX_LEN = 256
EPS = 1e-6

QK_NOPE = 128
QK_ROPE = 64
V_DIM = 128
N_HEADS = 16
Q_LORA = 512
KV_LORA = 512
ROPE_THETA = 10000.0
ATTN_SCALE = (QK_NOPE + QK_ROPE) ** -0.5
Q_PRESCALE = ATTN_SCALE * math.log2(math.e)

POOL_WINDOWS = (2, 4, 8, 16)
N_POOL_GROUPS = 4
POOL_WIDTH = 1024
POOL_GROUP = 256
POOL_OUT_GROUP = 512

OFF_KR = 1024
OFF_POOL = 1088

D_FF = 7168
N_EXPERTS = 8
TOP_K = 2

T_ROWS = CTX_LEN + SEQ
ROWS = BATCH * T_ROWS
ROW_TILE = 256
TILES_PER_BATCH = T_ROWS // ROW_TILE
LATENT_TILES = SEQ // ROW_TILE
N_TILES_ALL = BATCH * TILES_PER_BATCH
N_TILES_LATENT = BATCH * LATENT_TILES
HALO = 16
K_WIDTH = 256
HEADS_PER_STEP = 4

COL_Q, COL_KV, COL_POOL, COL_GATE = 0, 512, 1024, 2048

FFN_TILE = 272
FFN_TILES_PER_SEG = 4
SEG_ROWS = FFN_TILE * FFN_TILES_PER_SEG
FFN_CHUNK = 256
N_ASSIGN = BATCH * SEQ * TOP_K
MOE_SEGS = N_ASSIGN // SEG_ROWS + N_EXPERTS

VMEM_LIMIT = 56 * 1024 * 1024


def _cparams(sem, vmem=None):
    return pltpu.CompilerParams(dimension_semantics=sem, vmem_limit_bytes=vmem)


def _tile_all(i):
    return i


def _tile_latent(i):
    return (i // LATENT_TILES) * TILES_PER_BATCH + 1 + i % LATENT_TILES


def _group_of_tile(t):
    return jnp.where(t % TILES_PER_BATCH == 0, BATCH, t // TILES_PER_BATCH)


def _mod_spec(layer, chunk, tile_fn):
    def index(i):
        return ((layer * 3 + _group_of_tile(tile_fn(i))) * 6 + chunk, 0, 0)
    return pl.BlockSpec((None, 1, D_MODEL), index)


def _vec_spec(layer, width=D_MODEL):
    return pl.BlockSpec((None, 1, width), lambda i: (layer, 0, 0))


def _input_stream_specs():
    def x_index(t):
        return ((t // TILES_PER_BATCH) * LATENT_TILES + jnp.maximum(t % TILES_PER_BATCH - 1, 0), 0)

    def ctx_index(t):
        return (t // TILES_PER_BATCH, 0)

    return (pl.BlockSpec((ROW_TILE, D_MODEL), x_index), pl.BlockSpec((CTX_LEN, D_MODEL), ctx_index))


def _stream_tile(x_ref, ctx_ref):
    is_ctx = pl.program_id(0) % TILES_PER_BATCH == 0
    return jnp.where(is_ctx, ctx_ref[...], x_ref[...])


def _rmsnorm(x, g):
    return x * lax.rsqrt(jnp.mean(x * x, axis=-1, keepdims=True) + EPS) * g


def _mod_kernel(c_ref, w_ref, b_ref, o_ref):
    c = c_ref[...]
    a = c * jax.nn.sigmoid(c)
    o_ref[...] = jnp.dot(a, w_ref[...], preferred_element_type=F32) + b_ref[...]


def _modulation(c8, w_mod, b_mod):
    n = w_mod.shape[2]
    tn = 1024
    return pl.pallas_call(
        _mod_kernel,
        grid=(DEPTH, n // tn),
        in_specs=[pl.BlockSpec((8, D_MODEL), lambda l, j: (0, 0)),
                  pl.BlockSpec((None, D_MODEL, tn), lambda l, j: (l, 0, j)),
                  pl.BlockSpec((None, 1, tn), lambda l, j: (l, 0, j))],
        out_specs=pl.BlockSpec((None, 8, tn), lambda l, j: (l, 0, j)),
        out_shape=jax.ShapeDtypeStruct((DEPTH, 8, n), F32),
        compiler_params=_cparams(("arbitrary", "arbitrary"), VMEM_LIMIT),
        name="modulation",
    )(c8, w_mod, b_mod.reshape(DEPTH, 1, n))


def _norm_mod_kernel(x_ref, ctx_ref, g_ref, sh_ref, sc_ref, o_ref):
    y = _rmsnorm(_stream_tile(x_ref, ctx_ref), g_ref[...])
    o_ref[...] = (y * (1.0 + sc_ref[...]) + sh_ref[...]).astype(o_ref.dtype)


def _norm_mod(x2, ctx2, g, mods3, layer):
    x_spec, ctx_spec = _input_stream_specs()
    return pl.pallas_call(
        _norm_mod_kernel,
        grid=(N_TILES_ALL,),
        in_specs=[x_spec, ctx_spec, _vec_spec(layer),
                  _mod_spec(layer, 0, _tile_all), _mod_spec(layer, 1, _tile_all)],
        out_specs=pl.BlockSpec((ROW_TILE, D_MODEL), lambda i: (i, 0)),
        out_shape=jax.ShapeDtypeStruct((ROWS, D_MODEL), BF16),
        compiler_params=_cparams(("arbitrary",)),
        name="norm_mod",
    )(x2, ctx2, g, mods3, mods3)


def _matmul_kernel(a_ref, w_ref, o_ref):
    o_ref[...] = jnp.dot(a_ref[...], w_ref[...], preferred_element_type=F32).astype(o_ref.dtype)


def _matmul(a, w, tm, tn):
    m, k = a.shape
    n = w.shape[1]
    return pl.pallas_call(
        _matmul_kernel,
        grid=(n // tn, m // tm),
        in_specs=[pl.BlockSpec((tm, k), lambda j, i: (i, 0)),
                  pl.BlockSpec((k, tn), lambda j, i: (0, j))],
        out_specs=pl.BlockSpec((tm, tn), lambda j, i: (i, j)),
        out_shape=jax.ShapeDtypeStruct((m, n), BF16),
        compiler_params=_cparams(("arbitrary", "arbitrary"), VMEM_LIMIT),
        name="in_proj",
    )(a, w)


def _rope_mix(t):
    lane = lax.broadcasted_iota(jnp.int32, t.shape, 1)
    return jnp.where(lane < QK_ROPE, t + pltpu.roll(t, QK_ROPE, 1), 0.0)


def _rope_key_kernel(a_ref, w_ref, cs_ref, o_ref):
    y = jnp.dot(a_ref[...], w_ref[...], preferred_element_type=F32)
    o_ref[...] = _rope_mix(y * cs_ref[...]).astype(o_ref.dtype)


def _rope_key(h, w_kr2, cs):
    tm = 512
    return pl.pallas_call(
        _rope_key_kernel,
        grid=(ROWS // tm,),
        in_specs=[pl.BlockSpec((tm, D_MODEL), lambda i: (i, 0)),
                  pl.BlockSpec((D_MODEL, 128), lambda i: (0, 0)),
                  pl.BlockSpec((tm, 128), lambda i: (i, 0))],
        out_specs=pl.BlockSpec((tm, 128), lambda i: (i, 0)),
        out_shape=jax.ShapeDtypeStruct((ROWS, 128), BF16),
        compiler_params=_cparams(("arbitrary",)),
        name="rope_key",
    )(h, w_kr2, cs)


def _q_up_kernel(p_ref, g_ref, w_ref, cs_ref, q_ref):
    n = _rmsnorm(p_ref[...].astype(F32), g_ref[...]).astype(BF16)
    y = jnp.dot(n, w_ref[...], preferred_element_type=F32) * Q_PRESCALE
    cs = cs_ref[...]
    for h in range(N_HEADS):
        yh = y[:, h * K_WIDTH:(h + 1) * K_WIDTH]
        q_ref[h, :, :QK_NOPE] = yh[:, :QK_NOPE].astype(q_ref.dtype)
        q_ref[h, :, QK_NOPE:] = _rope_mix(yh[:, QK_NOPE:] * cs).astype(q_ref.dtype)


def _head_index(i):
    return (i // TILES_PER_BATCH, 0, i % TILES_PER_BATCH, 0)


def _q_up(p, g_q, w_q2, cs, layer):
    return pl.pallas_call(
        _q_up_kernel,
        grid=(N_TILES_ALL,),
        in_specs=[pl.BlockSpec((ROW_TILE, Q_LORA), lambda i: (i, COL_Q // Q_LORA)),
                  _vec_spec(layer, Q_LORA),
                  pl.BlockSpec((Q_LORA, N_HEADS * K_WIDTH), lambda i: (0, 0)),
                  pl.BlockSpec((ROW_TILE, 128), lambda i: (i, 0))],
        out_specs=pl.BlockSpec((None, N_HEADS, ROW_TILE, K_WIDTH), _head_index),
        out_shape=jax.ShapeDtypeStruct((BATCH, N_HEADS, T_ROWS, K_WIDTH), BF16),
        compiler_params=_cparams(("arbitrary",), VMEM_LIMIT),
        name="q_up",
    )(p, g_q, w_q2, cs)


def _kv_up_kernel(p_ref, g_ref, w_ref, kr_ref, k_ref, v_ref):
    n = _rmsnorm(p_ref[...].astype(F32), g_ref[...]).astype(BF16)
    y = jnp.dot(n, w_ref[...], preferred_element_type=F32)
    kr = kr_ref[...]
    for h in range(N_HEADS):
        base = h * (QK_NOPE + V_DIM)
        k_ref[h, :, :QK_NOPE] = y[:, base:base + QK_NOPE].astype(k_ref.dtype)
        k_ref[h, :, QK_NOPE:] = kr
        v_ref[h] = y[:, base + QK_NOPE:base + QK_NOPE + V_DIM].astype(v_ref.dtype)


def _kv_up(p, g_kv, w_kv, kr, layer):
    return pl.pallas_call(
        _kv_up_kernel,
        grid=(N_TILES_ALL,),
        in_specs=[pl.BlockSpec((ROW_TILE, KV_LORA), lambda i: (i, COL_KV // KV_LORA)),
                  _vec_spec(layer, KV_LORA),
                  pl.BlockSpec((KV_LORA, N_HEADS * (QK_NOPE + V_DIM)), lambda i: (0, 0)),
                  pl.BlockSpec((ROW_TILE, 128), lambda i: (i, 0))],
        out_specs=[pl.BlockSpec((None, N_HEADS, ROW_TILE, K_WIDTH), _head_index),
                   pl.BlockSpec((None, N_HEADS, ROW_TILE, V_DIM), _head_index)],
        out_shape=[jax.ShapeDtypeStruct((BATCH, N_HEADS, T_ROWS, K_WIDTH), BF16),
                   jax.ShapeDtypeStruct((BATCH, N_HEADS, T_ROWS, V_DIM), BF16)],
        compiler_params=_cparams(("arbitrary",), VMEM_LIMIT),
        name="kv_up",
    )(p, g_kv, w_kv, kr)


def _attend(q, k, v):
    s = lax.dot_general(q, k, (((1,), (1,)), ((), ())), preferred_element_type=F32)
    m = jnp.max(s, axis=-1, keepdims=True)
    e = jnp.exp2(s - m)
    denom = jnp.sum(e, axis=-1, keepdims=True)
    o = jnp.dot(e.astype(BF16), v, preferred_element_type=F32)
    return o / denom


def _attention_kernel(with_ctx, q_ref, k_ref, v_ref, o_ref):
    def run(n_keys):
        for h in range(HEADS_PER_STEP):
            o = _attend(q_ref[h], k_ref[h, :n_keys, :], v_ref[h, :n_keys, :])
            o_ref[:, h * V_DIM:(h + 1) * V_DIM] = o.astype(o_ref.dtype)

    if not with_ctx:
        run(T_ROWS)
        return

    pl.when(pl.program_id(2) == 0)(functools.partial(run, CTX_LEN))
    pl.when(pl.program_id(2) > 0)(functools.partial(run, T_ROWS))


def _attention(q, k, v, with_ctx):
    q0 = 0 if with_ctx else 1
    n_q = TILES_PER_BATCH - q0
    hps = HEADS_PER_STEP
    return pl.pallas_call(
        functools.partial(_attention_kernel, with_ctx),
        grid=(BATCH, N_HEADS // hps, n_q),
        in_specs=[pl.BlockSpec((None, hps, ROW_TILE, K_WIDTH), lambda b, h, i: (b, h, i + q0, 0)),
                  pl.BlockSpec((None, hps, T_ROWS, K_WIDTH), lambda b, h, i: (b, h, 0, 0)),
                  pl.BlockSpec((None, hps, T_ROWS, V_DIM), lambda b, h, i: (b, h, 0, 0))],
        out_specs=pl.BlockSpec((None, ROW_TILE, hps * V_DIM), lambda b, h, i: (b, i, h)),
        out_shape=jax.ShapeDtypeStruct((BATCH, n_q * ROW_TILE, D_MODEL), BF16),
        compiler_params=_cparams(("arbitrary", "arbitrary", "arbitrary"), VMEM_LIMIT),
        name="attention",
    )(q, k, v)


def _pool_kernel(tile_fn, prev_ref, cur_ref, next_ref, w_ref, ps_ref, o_ref, buf_ref):
    t = tile_fn(pl.program_id(0)) % TILES_PER_BATCH
    is_ctx = t == 0
    seg_len = jnp.where(is_ctx, CTX_LEN, SEQ)
    pos0 = jnp.where(is_ctx, 0, (t - 1) * ROW_TILE)
    has_prev = pos0 > 0
    has_next = pos0 + ROW_TILE < seg_len
    buf_ref[0:HALO, :] = jnp.where(has_prev, prev_ref[...].astype(F32), 0.0)
    buf_ref[HALO:HALO + ROW_TILE, :] = cur_ref[...].astype(F32)
    buf_ref[HALO + ROW_TILE:, :] = jnp.where(has_next, next_ref[...].astype(F32), 0.0)

    pos = pos0 + lax.broadcasted_iota(jnp.int32, (ROW_TILE, 1), 0)
    for g, win in enumerate(POOL_WINDOWS):
        cols = slice(g * POOL_GROUP, (g + 1) * POOL_GROUP)
        half = win // 2
        acc = buf_ref[HALO - half:HALO - half + ROW_TILE, cols]
        for j in range(1, win):
            acc = acc + buf_ref[HALO - half + j:HALO - half + j + ROW_TILE, cols]
        lo = jnp.maximum(pos - half, 0)
        hi = jnp.minimum(pos - half + win, seg_len)
        mean = acc / (hi - lo).astype(F32)
        pooled = (mean - buf_ref[HALO:HALO + ROW_TILE, cols]).astype(BF16)
        out = jnp.dot(pooled, w_ref[g], preferred_element_type=F32)
        ocols = slice(g * POOL_OUT_GROUP, (g + 1) * POOL_OUT_GROUP)
        o_ref[:, ocols] = (out * ps_ref[:, ocols]).astype(o_ref.dtype)


def _pool(p, w_pool, pool_scale, layer, tile_fn, n_tiles):
    per16 = ROW_TILE // HALO
    last16 = ROWS // HALO - 1
    col = COL_POOL // POOL_WIDTH
    return pl.pallas_call(
        functools.partial(_pool_kernel, tile_fn),
        grid=(n_tiles,),
        in_specs=[pl.BlockSpec((HALO, POOL_WIDTH), lambda i: (jnp.maximum(tile_fn(i) * per16 - 1, 0), col)),
                  pl.BlockSpec((ROW_TILE, POOL_WIDTH), lambda i: (tile_fn(i), col)),
                  pl.BlockSpec((HALO, POOL_WIDTH), lambda i: (jnp.minimum((tile_fn(i) + 1) * per16, last16), col)),
                  pl.BlockSpec((N_POOL_GROUPS, POOL_GROUP, POOL_OUT_GROUP), lambda i: (0, 0, 0)),
                  _vec_spec(layer)],
        out_specs=pl.BlockSpec((ROW_TILE, D_MODEL), lambda i: (i, 0)),
        out_shape=jax.ShapeDtypeStruct((n_tiles * ROW_TILE, D_MODEL), BF16),
        scratch_shapes=[pltpu.VMEM((ROW_TILE + 2 * HALO, POOL_WIDTH), F32)],
        compiler_params=_cparams(("arbitrary",)),
        name="pool",
    )(p, p, p, w_pool, pool_scale)


def _split_bf16(a):
    hi = a.astype(BF16)
    return hi, (a - hi.astype(F32)).astype(BF16)


def _merge_kernel(first, with_router, *refs):
    refs = list(refs)
    ga_ref, gb_ref, attn_ref, pool_ref, ba_ref, bb_ref, w_ref = refs[:7]
    del refs[:7]
    if first:
        x = _stream_tile(refs[0], refs[1])
        del refs[:2]
    else:
        x = refs.pop(0)[...]
    gpost_ref, g1_ref, gpre_ref, sh_ref, sc_ref = refs[:5]
    del refs[:5]
    if with_router:
        wr_ref, br_ref, xo_ref, h_ref, lg_ref = refs
    else:
        xo_ref, h_ref = refs

    ga = jax.nn.sigmoid(ga_ref[...].astype(F32) + ba_ref[...])
    gb = jax.nn.sigmoid(gb_ref[...].astype(F32) + bb_ref[...])
    mixed = ga * attn_ref[...].astype(F32) + gb * pool_ref[...].astype(F32)
    y = jnp.dot(mixed.astype(BF16), w_ref[...], preferred_element_type=F32)
    x = x + g1_ref[...] * _rmsnorm(y, gpost_ref[...])
    xo_ref[...] = x
    h = _rmsnorm(x, gpre_ref[...]) * (1.0 + sc_ref[...]) + sh_ref[...]
    h_ref[...] = h.astype(BF16).astype(h_ref.dtype)
    if with_router:
        h_hi, h_lo = _split_bf16(h)
        w_hi, w_lo = _split_bf16(wr_ref[...])
        dot = functools.partial(jnp.dot, preferred_element_type=F32)
        lg_ref[...] = dot(h_hi, w_hi) + (dot(h_lo, w_hi) + dot(h_hi, w_lo)) + br_ref[...]


def _merge(p, attn, pool, b_gate2, w_out, x_args, g_post, g_pre, mods3, layer, tile_fn, n_tiles,
           router=None):
    first = len(x_args) == 2
    gate_a = COL_GATE // D_MODEL
    act = pl.BlockSpec((ROW_TILE, D_MODEL), lambda i: (i, 0))
    in_specs = [pl.BlockSpec((ROW_TILE, D_MODEL), lambda i: (tile_fn(i), gate_a)),
                pl.BlockSpec((ROW_TILE, D_MODEL), lambda i: (tile_fn(i), gate_a + 1)),
                act, act,
                pl.BlockSpec((None, 1, D_MODEL), lambda i: (layer * 2, 0, 0)),
                pl.BlockSpec((None, 1, D_MODEL), lambda i: (layer * 2 + 1, 0, 0)),
                pl.BlockSpec((D_MODEL, D_MODEL), lambda i: (0, 0))]
    if first:
        in_specs += list(_input_stream_specs())
    else:
        in_specs.append(pl.BlockSpec((ROW_TILE, D_MODEL), lambda i: (tile_fn(i), 0)))
    in_specs += [_vec_spec(layer), _mod_spec(layer, 2, tile_fn), _vec_spec(layer),
                 _mod_spec(layer, 3, tile_fn), _mod_spec(layer, 4, tile_fn)]
    args = [p, p, attn, pool, b_gate2, b_gate2, w_out, *x_args, g_post, mods3, g_pre, mods3, mods3]
    rows = n_tiles * ROW_TILE
    out_specs = [act, act]
    h_dtype = BF16 if router is None else F32
    out_shape = [jax.ShapeDtypeStruct((rows, D_MODEL), F32), jax.ShapeDtypeStruct((rows, D_MODEL), h_dtype)]
    if router is not None:
        in_specs += [pl.BlockSpec((D_MODEL, 128), lambda i: (0, 0)), pl.BlockSpec((1, 128), lambda i: (0, 0))]
        args += list(router)
        out_specs.append(pl.BlockSpec((ROW_TILE, 128), lambda i: (i, 0)))
        out_shape.append(jax.ShapeDtypeStruct((rows, 128), F32))
    return pl.pallas_call(
        functools.partial(_merge_kernel, first, router is not None),
        grid=(n_tiles,),
        in_specs=in_specs,
        out_specs=out_specs,
        out_shape=out_shape,
        compiler_params=_cparams(("arbitrary",), VMEM_LIMIT),
        name="merge_out_proj",
    )(*args)


def _row_copy_wait(n, make_copy):
    def body(_, carry):
        make_copy().wait()
        return carry
    lax.fori_loop(0, n, body, 0)


def _dispatch_kernel(slot_ref, h_ref, xs_in_ref, xs_ref, sem):
    del xs_in_ref

    def copy(r, dst):
        return pltpu.make_async_copy(h_ref.at[pl.ds(r, 1)], xs_ref.at[pl.ds(dst, 1)], sem)

    def start(r, carry):
        for k in range(TOP_K):
            copy(r, slot_ref[0, r * TOP_K + k]).start()
        return carry

    lax.fori_loop(0, ROW_TILE, start, 0)
    _row_copy_wait(ROW_TILE * TOP_K, lambda: copy(0, 0))


def _dispatch(slot3, h2, xs_init):
    return pl.pallas_call(
        _dispatch_kernel,
        grid=(N_TILES_LATENT,),
        in_specs=[pl.BlockSpec((None, 1, ROW_TILE * TOP_K), lambda i: (i, 0, 0), memory_space=pltpu.SMEM),
                  pl.BlockSpec((ROW_TILE, D_MODEL), lambda i: (i, 0)),
                  pl.BlockSpec(memory_space=pl.ANY)],
        out_specs=pl.BlockSpec(memory_space=pl.ANY),
        out_shape=jax.ShapeDtypeStruct(xs_init.shape, xs_init.dtype),
        input_output_aliases={2: 0},
        scratch_shapes=[pltpu.SemaphoreType.DMA(())],
        compiler_params=_cparams(("arbitrary",)),
        name="moe_dispatch",
    )(slot3, h2, xs_init)


def _ffn_kernel(se_ref, st_ref, x_ref, wg_ref, wu_ref, wd_ref, o_ref):
    del se_ref
    n_tiles = st_ref[pl.program_id(0)]

    @pl.when(pl.program_id(1) == 0)
    def _():
        o_ref[...] = jnp.zeros_like(o_ref)

    def rows_step(row0, n_rows):
        rows = pl.ds(pl.multiple_of(row0, 16), n_rows)
        x = x_ref[rows, :].astype(F32)
        g = jnp.dot(x, wg_ref[...], preferred_element_type=F32)
        u = jnp.dot(x, wu_ref[...], preferred_element_type=F32)
        a = (g * jax.nn.sigmoid(g)) * u
        o_ref[rows, :] += jnp.dot(a, wd_ref[...], preferred_element_type=F32)

    def pair(i, carry):
        rows_step(i * (2 * FFN_TILE), 2 * FFN_TILE)
        return carry

    lax.fori_loop(0, lax.shift_right_logical(n_tiles, 1), pair, 0)

    @pl.when((n_tiles & 1) == 1)
    def _():
        rows_step((n_tiles - 1) * FFN_TILE, FFN_TILE)


def _ffn(seg_expert, seg_tiles, xs, wg, wu, wd):
    n_seg = xs.shape[0] // SEG_ROWS
    n_chunks = D_FF // FFN_CHUNK

    def chunk(s, f, st):
        return jnp.where(st[s] > 0, f, n_chunks - 1)

    grid_spec = pltpu.PrefetchScalarGridSpec(
        num_scalar_prefetch=2,
        grid=(n_seg, n_chunks),
        in_specs=[pl.BlockSpec((SEG_ROWS, D_MODEL), lambda s, f, se, st: (s, 0)),
                  pl.BlockSpec((None, D_MODEL, FFN_CHUNK), lambda s, f, se, st: (se[s], 0, chunk(s, f, st))),
                  pl.BlockSpec((None, D_MODEL, FFN_CHUNK), lambda s, f, se, st: (se[s], 0, chunk(s, f, st))),
                  pl.BlockSpec((None, FFN_CHUNK, D_MODEL), lambda s, f, se, st: (se[s], chunk(s, f, st), 0))],
        out_specs=pl.BlockSpec((SEG_ROWS, D_MODEL), lambda s, f, se, st: (s, 0)),
    )
    return pl.pallas_call(
        _ffn_kernel,
        grid_spec=grid_spec,
        out_shape=jax.ShapeDtypeStruct((n_seg * SEG_ROWS, D_MODEL), F32),
        compiler_params=_cparams(("arbitrary", "arbitrary"), VMEM_LIMIT),
        name="swiglu_ffn",
    )(seg_expert, seg_tiles, xs, wg, wu, wd)


def _post_kernel(y_ref, x_ref, gpost_ref, g2_ref, gpre_ref, sh_ref, sc_ref, xo_ref, h_ref):
    x = x_ref[...] + g2_ref[...] * _rmsnorm(y_ref[...], gpost_ref[...])
    xo_ref[...] = x
    h = _rmsnorm(x, gpre_ref[...]) * (1.0 + sc_ref[...]) + sh_ref[...]
    h_ref[...] = h.astype(h_ref.dtype)


def _post(y, x, g_post, g_pre_next, mods3, layer):
    row = pl.BlockSpec((ROW_TILE, D_MODEL), lambda i: (i, 0))
    return pl.pallas_call(
        _post_kernel,
        grid=(N_TILES_ALL,),
        in_specs=[row, row, _vec_spec(layer), _mod_spec(layer, 5, _tile_all),
                  _vec_spec(layer + 1), _mod_spec(layer + 1, 0, _tile_all), _mod_spec(layer + 1, 1, _tile_all)],
        out_specs=[row, row],
        out_shape=[jax.ShapeDtypeStruct((ROWS, D_MODEL), F32), jax.ShapeDtypeStruct((ROWS, D_MODEL), BF16)],
        compiler_params=_cparams(("arbitrary",)),
        name="post_ffn",
    )(y, x, g_post, mods3, g_pre_next, mods3, mods3)


def _combine_kernel(slot_ref, w_ref, x_ref, gpost_ref, g2_ref, ys_ref, o_ref, buf_ref, sem):
    def copy(r, k, src):
        return pltpu.make_async_copy(ys_ref.at[pl.ds(src, 1)], buf_ref.at[k, pl.ds(r, 1)], sem)

    def start(r, carry):
        for k in range(TOP_K):
            copy(r, k, slot_ref[0, r * TOP_K + k]).start()
        return carry

    lax.fori_loop(0, ROW_TILE, start, 0)
    _row_copy_wait(ROW_TILE * TOP_K, lambda: copy(0, 0, 0))
    y = w_ref[:, 0:1] * buf_ref[0] + w_ref[:, 1:2] * buf_ref[1]
    o_ref[...] = x_ref[...] + g2_ref[...] * _rmsnorm(y, gpost_ref[...])


def _combine(slot3, weight, x, g_post, mods3, layer, ys):
    row = pl.BlockSpec((ROW_TILE, D_MODEL), lambda i: (i, 0))
    return pl.pallas_call(
        _combine_kernel,
        grid=(N_TILES_LATENT,),
        in_specs=[pl.BlockSpec((None, 1, ROW_TILE * TOP_K), lambda i: (i, 0, 0), memory_space=pltpu.SMEM),
                  pl.BlockSpec((ROW_TILE, TOP_K), lambda i: (i, 0)),
                  row, _vec_spec(layer), _mod_spec(layer, 5, _tile_latent),
                  pl.BlockSpec(memory_space=pl.ANY)],
        out_specs=row,
        out_shape=jax.ShapeDtypeStruct((BATCH * SEQ, D_MODEL), F32),
        scratch_shapes=[pltpu.VMEM((TOP_K, ROW_TILE, D_MODEL), F32), pltpu.SemaphoreType.DMA(())],
        compiler_params=_cparams(("arbitrary",), VMEM_LIMIT),
        name="moe_combine",
    )(slot3, weight, x, g_post, mods3, ys)


def _rope_table():
    pos = jnp.arange(SEQ, dtype=jnp.int32)
    row_ids = (pos // GRID_W).astype(F32)
    col_ids = (pos % GRID_W).astype(F32)
    n_freq = QK_ROPE // 4
    inv = ROPE_THETA ** (-jnp.arange(n_freq, dtype=F32) / n_freq)
    ang = jnp.concatenate([row_ids[:, None] * inv, col_ids[:, None] * inv], axis=-1)
    cos, sin = jnp.cos(ang), jnp.sin(ang)
    latent = jnp.concatenate([cos, cos, -sin, sin], axis=-1)
    ctx = jnp.concatenate([jnp.ones((CTX_LEN, QK_ROPE), F32), jnp.zeros((CTX_LEN, QK_ROPE), F32)], axis=-1)
    one = jnp.concatenate([ctx, latent], axis=0)
    return jnp.tile(one, (BATCH, 1))


def _swap_halves(w):
    half = w.shape[-1] // 2
    return jnp.concatenate([w[..., half:], w[..., :half]], axis=-1)


def _q_weights(w_q_up):
    w = w_q_up.reshape(Q_LORA, N_HEADS, QK_NOPE + QK_ROPE)
    rope = w[..., QK_NOPE:]
    w2 = jnp.concatenate([w[..., :QK_NOPE], rope, _swap_halves(rope)], axis=-1)
    return w2.reshape(Q_LORA, N_HEADS * K_WIDTH).astype(BF16)


def _routing(logits):
    probs = jax.nn.softmax(logits, axis=-1)
    top_p, top_i = lax.top_k(probs, TOP_K)
    top_p = top_p / jnp.sum(top_p, axis=-1, keepdims=True)
    onehot = jax.nn.one_hot(top_i.reshape(N_ASSIGN), N_EXPERTS, dtype=jnp.int32)
    csum = jnp.cumsum(onehot, axis=0)
    rank = jnp.sum((csum - onehot) * onehot, axis=1)
    counts = csum[-1]
    n_segs = (counts + SEG_ROWS - 1) // SEG_ROWS
    seg_start = jnp.cumsum(n_segs) - n_segs
    slot = jnp.sum(onehot * seg_start[None, :], axis=1) * SEG_ROWS + rank
    seg_ids = jnp.arange(MOE_SEGS, dtype=jnp.int32)
    used = jnp.sum(n_segs)
    seg_expert = jnp.sum((seg_ids[:, None] >= (seg_start + n_segs)[None, :]).astype(jnp.int32), axis=1)
    last_used_expert = jnp.max(jnp.where(counts > 0, jnp.arange(N_EXPERTS), 0))
    seg_expert = jnp.where(seg_ids < used, jnp.minimum(seg_expert, N_EXPERTS - 1), last_used_expert)
    rows_left = counts[seg_expert] - (seg_ids - seg_start[seg_expert]) * SEG_ROWS
    seg_tiles = jnp.clip((rows_left + FFN_TILE - 1) // FFN_TILE, 0, FFN_TILES_PER_SEG)
    seg_tiles = jnp.where(seg_ids < used, seg_tiles, 0)
    return (seg_expert.astype(jnp.int32), seg_tiles.astype(jnp.int32),
            slot.astype(jnp.int32).reshape(BATCH * SEQ, TOP_K), top_p)


def kernel(x, c, ctx, c_ctx, w_mod, b_mod, g_mix_pre, g_mix_post, g_ffn_pre, g_ffn_post, w_in, b_gate, g_q_lat, g_kv_lat, w_q_up, w_kv_up, w_pool, pool_scale, w_out, w_ff_gate, w_ff_up, w_ff_down, w_router, b_router, w_exp_gate, w_exp_up, w_exp_down):
    x2 = x.reshape(BATCH * SEQ, D_MODEL)
    ctx2 = ctx.reshape(BATCH * CTX_LEN, D_MODEL)

    c8 = jnp.concatenate([c, c_ctx[None, :], jnp.zeros((8 - BATCH - 1, D_MODEL), F32)], axis=0)
    mods = _modulation(c8, w_mod, b_mod)
    mods3 = mods[:, :BATCH + 1].reshape(DEPTH * 3 * 6, 1, D_MODEL)

    as_vec = lambda a: a.reshape(DEPTH, 1, a.shape[-1])
    g_mix_pre, g_mix_post, g_ffn_pre, g_ffn_post = map(as_vec, (g_mix_pre, g_mix_post, g_ffn_pre, g_ffn_post))
    g_q_lat, g_kv_lat, pool_scale = map(as_vec, (g_q_lat, g_kv_lat, pool_scale))
    b_gate2 = b_gate.reshape(DEPTH * 2, 1, D_MODEL)
    cs = _rope_table()

    h = _norm_mod(x2, ctx2, g_mix_pre, mods3, 0)
    xr = None
    out = None
    for l in range(DEPTH):
        last = l == DEPTH - 1
        w_main = jnp.concatenate([w_in[l][:, :OFF_KR], w_in[l][:, OFF_POOL:]], axis=1).astype(BF16)
        w_kr = w_in[l][:, OFF_KR:OFF_POOL]
        w_kr2 = jnp.concatenate([w_kr, _swap_halves(w_kr)], axis=1).astype(BF16)

        p = _matmul(h, w_main, 512, 1024)
        kr = _rope_key(h, w_kr2, cs)
        q = _q_up(p, g_q_lat, _q_weights(w_q_up[l]), cs, l)
        k, v = _kv_up(p, g_kv_lat, w_kv_up[l].astype(BF16), kr, l)

        attn = _attention(q, k, v, with_ctx=not last).reshape(-1, D_MODEL)
        tile_fn, n_tiles = (_tile_latent, N_TILES_LATENT) if last else (_tile_all, N_TILES_ALL)
        pool = _pool(p, w_pool[l].astype(BF16), pool_scale, l, tile_fn, n_tiles)
        x_args = (x2, ctx2) if l == 0 else (xr,)
        merge = functools.partial(_merge, p, attn, pool, b_gate2, w_out[l].astype(BF16), x_args, g_mix_post,
                                  g_ffn_pre, mods3, l, tile_fn, n_tiles)

        if not last:
            xr, h2 = merge()
            n_seg = ROWS // SEG_ROWS
            seg_expert = jnp.zeros((n_seg,), jnp.int32)
            seg_tiles = jnp.full((n_seg,), FFN_TILES_PER_SEG, jnp.int32)
            y = _ffn(seg_expert, seg_tiles, h2, w_ff_gate, w_ff_up, w_ff_down)
            xr, h = _post(y, xr, g_ffn_post, g_mix_pre, mods3, l)
        else:
            w_r = jnp.pad(w_router[0], ((0, 0), (0, 128 - N_EXPERTS)))
            b_r = jnp.pad(b_router[0], (0, 128 - N_EXPERTS)).reshape(1, 128)
            xl, h2, logits = merge(router=(w_r, b_r))
            seg_expert, seg_tiles, slot, weight = _routing(logits[:, :N_EXPERTS])
            slot3 = slot.reshape(N_TILES_LATENT, 1, ROW_TILE * TOP_K)
            xs = _dispatch(slot3, h2, jnp.zeros((MOE_SEGS * SEG_ROWS, D_MODEL), F32))
            ys = _ffn(seg_expert, seg_tiles, xs, w_exp_gate.reshape(N_EXPERTS, D_MODEL, D_FF),
                      w_exp_up.reshape(N_EXPERTS, D_MODEL, D_FF), w_exp_down.reshape(N_EXPERTS, D_FF, D_MODEL))
            out = _combine(slot3, weight, xl, g_ffn_post, mods3, l, ys)
    return out.reshape(BATCH, SEQ, D_MODEL)
```

```python
import functools
import math

import jax
import jax.numpy as jnp
from jax import lax
from jax.experimental import pallas as pl
from jax.experimental.pallas import tpu as pltpu

F32 = jnp.float32
BF16 = jnp.bfloat16

D_MODEL = 2048
BATCH = 2
SEQ = 4096
DEPTH = 2
GRID_W = 64
CTX_LEN = 256
EPS = 1e-6

QK_NOPE = 128
QK_ROPE = 64
V_DIM = 128
N_HEADS = 16
Q_LORA = 512
KV_LORA = 512
ROPE_THETA = 10000.0
ATTN_SCALE = (QK_NOPE + QK_ROPE) ** -0.5
Q_PRESCALE = ATTN_SCALE * math.log2(math.e)

POOL_WINDOWS = (2, 4, 8, 16)
N_POOL_GROUPS = 4
POOL_WIDTH = 1024
POOL_GROUP = 256
POOL_OUT_GROUP = 512

OFF_KR = 1024
OFF_POOL = 1088

D_FF = 7168
N_EXPERTS = 8
TOP_K = 2

T_ROWS = CTX_LEN + SEQ
ROWS = BATCH * T_ROWS
ROW_TILE = 256
TILES_PER_BATCH = T_ROWS // ROW_TILE
LATENT_TILES = SEQ // ROW_TILE
N_TILES_ALL = BATCH * TILES_PER_BATCH
N_TILES_LATENT = BATCH * LATENT_TILES
HALO = 16
K_WIDTH = 256
HEADS_PER_STEP = 8

COL_Q, COL_KV, COL_POOL, COL_GATE = 0, 512, 1024, 2048

FFN_TILE = 272
FFN_TILES_PER_SEG = 4
SEG_ROWS = FFN_TILE * FFN_TILES_PER_SEG
FFN_CHUNK = 256
N_ASSIGN = BATCH * SEQ * TOP_K
MOE_SEGS = N_ASSIGN // SEG_ROWS + N_EXPERTS

VMEM_LIMIT = 56 * 1024 * 1024


def _cparams(sem, vmem=None):
    return pltpu.CompilerParams(dimension_semantics=sem, vmem_limit_bytes=vmem)


def _tile_all(i):
    return i


def _tile_latent(i):
    return (i // LATENT_TILES) * TILES_PER_BATCH + 1 + i % LATENT_TILES


def _group_of_tile(t):
    return jnp.where(t % TILES_PER_BATCH == 0, BATCH, t // TILES_PER_BATCH)


def _mod_spec(layer, chunk, tile_fn):
    def index(i):
        return ((layer * 3 + _group_of_tile(tile_fn(i))) * 6 + chunk, 0, 0)
    return pl.BlockSpec((None, 1, D_MODEL), index)


def _vec_spec(layer, width=D_MODEL):
    return pl.BlockSpec((None, 1, width), lambda i: (layer, 0, 0))


def _input_stream_specs():
    def x_index(t):
        return ((t // TILES_PER_BATCH) * LATENT_TILES + jnp.maximum(t % TILES_PER_BATCH - 1, 0), 0)

    def ctx_index(t):
        return (t // TILES_PER_BATCH, 0)

    return (pl.BlockSpec((ROW_TILE, D_MODEL), x_index), pl.BlockSpec((CTX_LEN, D_MODEL), ctx_index))


def _stream_tile(x_ref, ctx_ref):
    is_ctx = pl.program_id(0) % TILES_PER_BATCH == 0
    return jnp.where(is_ctx, ctx_ref[...], x_ref[...])


def _rmsnorm(x, g):
    return x * lax.rsqrt(jnp.mean(x * x, axis=-1, keepdims=True) + EPS) * g


def _mod_kernel(c_ref, w_ref, b_ref, o_ref):
    c = c_ref[...]
    a = c * jax.nn.sigmoid(c)
    o_ref[...] = jnp.dot(a, w_ref[...], preferred_element_type=F32) + b_ref[...]


def _modulation(c8, w_mod, b_mod):
    n = w_mod.shape[2]
    tn = 1024
    return pl.pallas_call(
        _mod_kernel,
        grid=(DEPTH, n // tn),
        in_specs=[pl.BlockSpec((8, D_MODEL), lambda l, j: (0, 0)),
                  pl.BlockSpec((None, D_MODEL, tn), lambda l, j: (l, 0, j)),
                  pl.BlockSpec((None, 1, tn), lambda l, j: (l, 0, j))],
        out_specs=pl.BlockSpec((None, 8, tn), lambda l, j: (l, 0, j)),
        out_shape=jax.ShapeDtypeStruct((DEPTH, 8, n), F32),
        compiler_params=_cparams(("arbitrary", "arbitrary"), VMEM_LIMIT),
        name="modulation",
    )(c8, w_mod, b_mod.reshape(DEPTH, 1, n))


def _norm_mod_kernel(x_ref, ctx_ref, g_ref, sh_ref, sc_ref, o_ref):
    y = _rmsnorm(_stream_tile(x_ref, ctx_ref), g_ref[...])
    o_ref[...] = (y * (1.0 + sc_ref[...]) + sh_ref[...]).astype(o_ref.dtype)


def _norm_mod(x2, ctx2, g, mods3, layer):
    x_spec, ctx_spec = _input_stream_specs()
    return pl.pallas_call(
        _norm_mod_kernel,
        grid=(N_TILES_ALL,),
        in_specs=[x_spec, ctx_spec, _vec_spec(layer),
                  _mod_spec(layer, 0, _tile_all), _mod_spec(layer, 1, _tile_all)],
        out_specs=pl.BlockSpec((ROW_TILE, D_MODEL), lambda i: (i, 0)),
        out_shape=jax.ShapeDtypeStruct((ROWS, D_MODEL), BF16),
        compiler_params=_cparams(("arbitrary",)),
        name="norm_mod",
    )(x2, ctx2, g, mods3, mods3)


def _matmul_kernel(a_ref, w_ref, o_ref):
    o_ref[...] = jnp.dot(a_ref[...], w_ref[...], preferred_element_type=F32).astype(o_ref.dtype)


def _matmul(a, w, tm, tn):
    m, k = a.shape
    n = w.shape[1]
    return pl.pallas_call(
        _matmul_kernel,
        grid=(n // tn, m // tm),
        in_specs=[pl.BlockSpec((tm, k), lambda j, i: (i, 0)),
                  pl.BlockSpec((k, tn), lambda j, i: (0, j))],
        out_specs=pl.BlockSpec((tm, tn), lambda j, i: (i, j)),
        out_shape=jax.ShapeDtypeStruct((m, n), BF16),
        compiler_params=_cparams(("arbitrary", "arbitrary"), VMEM_LIMIT),
        name="in_proj",
    )(a, w)


def _rope_mix(t):
    lane = lax.broadcasted_iota(jnp.int32, t.shape, 1)
    return jnp.where(lane < QK_ROPE, t + pltpu.roll(t, QK_ROPE, 1), 0.0)


def _rope_key_kernel(a_ref, w_ref, cs_ref, o_ref):
    y = jnp.dot(a_ref[...], w_ref[...], preferred_element_type=F32)
    o_ref[...] = _rope_mix(y * cs_ref[...]).astype(o_ref.dtype)


def _rope_key(h, w_kr2, cs):
    tm = 512
    return pl.pallas_call(
        _rope_key_kernel,
        grid=(ROWS // tm,),
        in_specs=[pl.BlockSpec((tm, D_MODEL), lambda i: (i, 0)),
                  pl.BlockSpec((D_MODEL, 128), lambda i: (0, 0)),
                  pl.BlockSpec((tm, 128), lambda i: (i, 0))],
        out_specs=pl.BlockSpec((tm, 128), lambda i: (i, 0)),
        out_shape=jax.ShapeDtypeStruct((ROWS, 128), BF16),
        compiler_params=_cparams(("arbitrary",)),
        name="rope_key",
    )(h, w_kr2, cs)


def _q_up_kernel(p_ref, g_ref, w_ref, cs_ref, q_ref):
    n = _rmsnorm(p_ref[...].astype(F32), g_ref[...]).astype(BF16)
    y = jnp.dot(n, w_ref[...], preferred_element_type=F32) * Q_PRESCALE
    cs = cs_ref[...]
    for h in range(N_HEADS):
        yh = y[:, h * K_WIDTH:(h + 1) * K_WIDTH]
        q_ref[h, :, :QK_NOPE] = yh[:, :QK_NOPE].astype(q_ref.dtype)
        q_ref[h, :, QK_NOPE:] = _rope_mix(yh[:, QK_NOPE:] * cs).astype(q_ref.dtype)


def _head_index(i):
    return (i // TILES_PER_BATCH, 0, i % TILES_PER_BATCH, 0)


def _q_up(p, g_q, w_q2, cs, layer):
    return pl.pallas_call(
        _q_up_kernel,
        grid=(N_TILES_ALL,),
        in_specs=[pl.BlockSpec((ROW_TILE, Q_LORA), lambda i: (i, COL_Q // Q_LORA)),
                  _vec_spec(layer, Q_LORA),
                  pl.BlockSpec((Q_LORA, N_HEADS * K_WIDTH), lambda i: (0, 0)),
                  pl.BlockSpec((ROW_TILE, 128), lambda i: (i, 0))],
        out_specs=pl.BlockSpec((None, N_HEADS, ROW_TILE, K_WIDTH), _head_index),
        out_shape=jax.ShapeDtypeStruct((BATCH, N_HEADS, T_ROWS, K_WIDTH), BF16),
        compiler_params=_cparams(("arbitrary",), VMEM_LIMIT),
        name="q_up",
    )(p, g_q, w_q2, cs)


def _kv_up_kernel(p_ref, g_ref, w_ref, kr_ref, k_ref, v_ref):
    n = _rmsnorm(p_ref[...].astype(F32), g_ref[...]).astype(BF16)
    y = jnp.dot(n, w_ref[...], preferred_element_type=F32)
    kr = kr_ref[...]
    for h in range(N_HEADS):
        base = h * (QK_NOPE + V_DIM)
        k_ref[h, :, :QK_NOPE] = y[:, base:base + QK_NOPE].astype(k_ref.dtype)
        k_ref[h, :, QK_NOPE:] = kr
        v_ref[h] = y[:, base + QK_NOPE:base + QK_NOPE + V_DIM].astype(v_ref.dtype)


def _kv_up(p, g_kv, w_kv, kr, layer):
    return pl.pallas_call(
        _kv_up_kernel,
        grid=(N_TILES_ALL,),
        in_specs=[pl.BlockSpec((ROW_TILE, KV_LORA), lambda i: (i, COL_KV // KV_LORA)),
                  _vec_spec(layer, KV_LORA),
                  pl.BlockSpec((KV_LORA, N_HEADS * (QK_NOPE + V_DIM)), lambda i: (0, 0)),
                  pl.BlockSpec((ROW_TILE, 128), lambda i: (i, 0))],
        out_specs=[pl.BlockSpec((None, N_HEADS, ROW_TILE, K_WIDTH), _head_index),
                   pl.BlockSpec((None, N_HEADS, ROW_TILE, V_DIM), _head_index)],
        out_shape=[jax.ShapeDtypeStruct((BATCH, N_HEADS, T_ROWS, K_WIDTH), BF16),
                   jax.ShapeDtypeStruct((BATCH, N_HEADS, T_ROWS, V_DIM), BF16)],
        compiler_params=_cparams(("arbitrary",), VMEM_LIMIT),
        name="kv_up",
    )(p, g_kv, w_kv, kr)


def _attend(q, k, v):
    s = lax.dot_general(q, k, (((1,), (1,)), ((), ())), preferred_element_type=F32)
    m = jnp.max(s, axis=-1, keepdims=True)
    e = jnp.exp2(s - m)
    denom = jnp.sum(e, axis=-1, keepdims=True)
    o = jnp.dot(e.astype(BF16), v, preferred_element_type=F32)
    return o / denom


def _attention_kernel(with_ctx, q_ref, k_ref, v_ref, o_ref):
    def run(n_keys):
        for h in range(HEADS_PER_STEP):
            o = _attend(q_ref[h], k_ref[h, :n_keys, :], v_ref[h, :n_keys, :])
            o_ref[:, h * V_DIM:(h + 1) * V_DIM] = o.astype(o_ref.dtype)

    if not with_ctx:
        run(T_ROWS)
        return

    pl.when(pl.program_id(2) == 0)(functools.partial(run, CTX_LEN))
    pl.when(pl.program_id(2) > 0)(functools.partial(run, T_ROWS))


def _attention(q, k, v, with_ctx):
    q0 = 0 if with_ctx else 1
    n_q = TILES_PER_BATCH - q0
    hps = HEADS_PER_STEP
    return pl.pallas_call(
        functools.partial(_attention_kernel, with_ctx),
        grid=(BATCH, N_HEADS // hps, n_q),
        in_specs=[pl.BlockSpec((None, hps, ROW_TILE, K_WIDTH), lambda b, h, i: (b, h, i + q0, 0)),
                  pl.BlockSpec((None, hps, T_ROWS, K_WIDTH), lambda b, h, i: (b, h, 0, 0),
                               pipeline_mode=pl.Buffered(1)),
                  pl.BlockSpec((None, hps, T_ROWS, V_DIM), lambda b, h, i: (b, h, 0, 0),
                               pipeline_mode=pl.Buffered(1))],
        out_specs=pl.BlockSpec((None, ROW_TILE, hps * V_DIM), lambda b, h, i: (b, i, h)),
        out_shape=jax.ShapeDtypeStruct((BATCH, n_q * ROW_TILE, D_MODEL), BF16),
        compiler_params=_cparams(("arbitrary", "arbitrary", "arbitrary"), VMEM_LIMIT),
        name="attention",
    )(q, k, v)


def _pool_kernel(tile_fn, prev_ref, cur_ref, next_ref, w_ref, ps_ref, o_ref, buf_ref):
    t = tile_fn(pl.program_id(0)) % TILES_PER_BATCH
    is_ctx = t == 0
    seg_len = jnp.where(is_ctx, CTX_LEN, SEQ)
    pos0 = jnp.where(is_ctx, 0, (t - 1) * ROW_TILE)
    has_prev = pos0 > 0
    has_next = pos0 + ROW_TILE < seg_len
    buf_ref[0:HALO, :] = jnp.where(has_prev, prev_ref[...].astype(F32), 0.0)
    buf_ref[HALO:HALO + ROW_TILE, :] = cur_ref[...].astype(F32)
    buf_ref[HALO + ROW_TILE:, :] = jnp.where(has_next, next_ref[...].astype(F32), 0.0)

    pos = pos0 + lax.broadcasted_iota(jnp.int32, (ROW_TILE, 1), 0)
    for g, win in enumerate(POOL_WINDOWS):
        cols = slice(g * POOL_GROUP, (g + 1) * POOL_GROUP)
        half = win // 2
        acc = buf_ref[HALO - half:HALO - half + ROW_TILE, cols]
        for j in range(1, win):
            acc = acc + buf_ref[HALO - half + j:HALO - half + j + ROW_TILE, cols]
        lo = jnp.maximum(pos - half, 0)
        hi = jnp.minimum(pos - half + win, seg_len)
        mean = acc / (hi - lo).astype(F32)
        pooled = (mean - buf_ref[HALO:HALO + ROW_TILE, cols]).astype(BF16)
        out = jnp.dot(pooled, w_ref[g], preferred_element_type=F32)
        ocols = slice(g * POOL_OUT_GROUP, (g + 1) * POOL_OUT_GROUP)
        o_ref[:, ocols] = (out * ps_ref[:, ocols]).astype(o_ref.dtype)


def _pool(p, w_pool, pool_scale, layer, tile_fn, n_tiles):
    per16 = ROW_TILE // HALO
    last16 = ROWS // HALO - 1
    col = COL_POOL // POOL_WIDTH
    return pl.pallas_call(
        functools.partial(_pool_kernel, tile_fn),
        grid=(n_tiles,),
        in_specs=[pl.BlockSpec((HALO, POOL_WIDTH), lambda i: (jnp.maximum(tile_fn(i) * per16 - 1, 0), col)),
                  pl.BlockSpec((ROW_TILE, POOL_WIDTH), lambda i: (tile_fn(i), col)),
                  pl.BlockSpec((HALO, POOL_WIDTH), lambda i: (jnp.minimum((tile_fn(i) + 1) * per16, last16), col)),
                  pl.BlockSpec((N_POOL_GROUPS, POOL_GROUP, POOL_OUT_GROUP), lambda i: (0, 0, 0)),
                  _vec_spec(layer)],
        out_specs=pl.BlockSpec((ROW_TILE, D_MODEL), lambda i: (i, 0)),
        out_shape=jax.ShapeDtypeStruct((n_tiles * ROW_TILE, D_MODEL), BF16),
        scratch_shapes=[pltpu.VMEM((ROW_TILE + 2 * HALO, POOL_WIDTH), F32)],
        compiler_params=_cparams(("arbitrary",)),
        name="pool",
    )(p, p, p, w_pool, pool_scale)


def _split_bf16(a):
    hi = a.astype(BF16)
    return hi, (a - hi.astype(F32)).astype(BF16)


def _merge_kernel(first, with_router, *refs):
    refs = list(refs)
    ga_ref, gb_ref, attn_ref, pool_ref, ba_ref, bb_ref, w_ref = refs[:7]
    del refs[:7]
    if first:
        x = _stream_tile(refs[0], refs[1])
        del refs[:2]
    else:
        x = refs.pop(0)[...]
    gpost_ref, g1_ref, gpre_ref, sh_ref, sc_ref = refs[:5]
    del refs[:5]
    if with_router:
        wr_ref, br_ref, xo_ref, h_ref, lg_ref = refs
    else:
        xo_ref, h_ref = refs

    ga = jax.nn.sigmoid(ga_ref[...].astype(F32) + ba_ref[...])
    gb = jax.nn.sigmoid(gb_ref[...].astype(F32) + bb_ref[...])
    mixed = ga * attn_ref[...].astype(F32) + gb * pool_ref[...].astype(F32)
    y = jnp.dot(mixed.astype(BF16), w_ref[...], preferred_element_type=F32)
    x = x + g1_ref[...] * _rmsnorm(y, gpost_ref[...])
    xo_ref[...] = x
    h = _rmsnorm(x, gpre_ref[...]) * (1.0 + sc_ref[...]) + sh_ref[...]
    h_ref[...] = h.astype(BF16).astype(h_ref.dtype)
    if with_router:
        h_hi, h_lo = _split_bf16(h)
        w_hi, w_lo = _split_bf16(wr_ref[...])
        dot = functools.partial(jnp.dot, preferred_element_type=F32)
        lg_ref[...] = dot(h_hi, w_hi) + (dot(h_lo, w_hi) + dot(h_hi, w_lo)) + br_ref[...]


def _merge(p, attn, pool, b_gate2, w_out, x_args, g_post, g_pre, mods3, layer, tile_fn, n_tiles,
           router=None):
    first = len(x_args) == 2
    gate_a = COL_GATE // D_MODEL
    act = pl.BlockSpec((ROW_TILE, D_MODEL), lambda i: (i, 0))
    in_specs = [pl.BlockSpec((ROW_TILE, D_MODEL), lambda i: (tile_fn(i), gate_a)),
                pl.BlockSpec((ROW_TILE, D_MODEL), lambda i: (tile_fn(i), gate_a + 1)),
                act, act,
                pl.BlockSpec((None, 1, D_MODEL), lambda i: (layer * 2, 0, 0)),
                pl.BlockSpec((None, 1, D_MODEL), lambda i: (layer * 2 + 1, 0, 0)),
                pl.BlockSpec((D_MODEL, D_MODEL), lambda i: (0, 0))]
    if first:
        in_specs += list(_input_stream_specs())
    else:
        in_specs.append(pl.BlockSpec((ROW_TILE, D_MODEL), lambda i: (tile_fn(i), 0)))
    in_specs += [_vec_spec(layer), _mod_spec(layer, 2, tile_fn), _vec_spec(layer),
                 _mod_spec(layer, 3, tile_fn), _mod_spec(layer, 4, tile_fn)]
    args = [p, p, attn, pool, b_gate2, b_gate2, w_out, *x_args, g_post, mods3, g_pre, mods3, mods3]
    rows = n_tiles * ROW_TILE
    out_specs = [act, act]
    h_dtype = BF16 if router is None else F32
    out_shape = [jax.ShapeDtypeStruct((rows, D_MODEL), F32), jax.ShapeDtypeStruct((rows, D_MODEL), h_dtype)]
    if router is not None:
        in_specs += [pl.BlockSpec((D_MODEL, 128), lambda i: (0, 0)), pl.BlockSpec((1, 128), lambda i: (0, 0))]
        args += list(router)
        out_specs.append(pl.BlockSpec((ROW_TILE, 128), lambda i: (i, 0)))
        out_shape.append(jax.ShapeDtypeStruct((rows, 128), F32))
    return pl.pallas_call(
        functools.partial(_merge_kernel, first, router is not None),
        grid=(n_tiles,),
        in_specs=in_specs,
        out_specs=out_specs,
        out_shape=out_shape,
        compiler_params=_cparams(("arbitrary",), VMEM_LIMIT),
        name="merge_out_proj",
    )(*args)


DMA_UNROLL = 8


def _dispatch_kernel(slot_ref, h_ref, xs_in_ref, xs_ref, sem):
    del xs_in_ref

    def start(r, carry):
        for k in range(TOP_K):
            dst = slot_ref[0, r * TOP_K + k]
            pltpu.make_async_copy(h_ref.at[pl.ds(r, 1)], xs_ref.at[pl.ds(dst, 1)], sem).start()
        return carry

    lax.fori_loop(0, ROW_TILE, start, 0, unroll=DMA_UNROLL)
    for _ in range(TOP_K):
        pltpu.make_async_copy(h_ref, xs_ref.at[pl.ds(0, ROW_TILE)], sem).wait()


def _dispatch(slot3, h2):
    xs_init = jnp.zeros((MOE_SEGS * SEG_ROWS, D_MODEL), h2.dtype)
    return pl.pallas_call(
        _dispatch_kernel,
        grid=(N_TILES_LATENT,),
        in_specs=[pl.BlockSpec((None, 1, ROW_TILE * TOP_K), lambda i: (i, 0, 0), memory_space=pltpu.SMEM),
                  pl.BlockSpec((ROW_TILE, D_MODEL), lambda i: (i, 0)),
                  pl.BlockSpec(memory_space=pl.ANY)],
        out_specs=pl.BlockSpec(memory_space=pl.ANY),
        out_shape=jax.ShapeDtypeStruct(xs_init.shape, xs_init.dtype),
        input_output_aliases={2: 0},
        scratch_shapes=[pltpu.SemaphoreType.DMA(())],
        compiler_params=_cparams(("arbitrary",)),
        name="moe_dispatch",
    )(slot3, h2, xs_init)


def _ffn_kernel(se_ref, st_ref, x_ref, wg_ref, wu_ref, wd_ref, o_ref):
    del se_ref
    n_tiles = st_ref[pl.program_id(0)]

    @pl.when(pl.program_id(1) == 0)
    def _():
        o_ref[...] = jnp.zeros_like(o_ref)

    def rows_step(row0, n_rows):
        rows = pl.ds(row0 if isinstance(row0, int) else pl.multiple_of(row0, 16), n_rows)
        x = x_ref[rows, :].astype(BF16)
        g = jnp.dot(x, wg_ref[...].astype(BF16), preferred_element_type=F32)
        u = jnp.dot(x, wu_ref[...].astype(BF16), preferred_element_type=F32)
        a = (g * jax.nn.sigmoid(g)) * u
        o_ref[rows, :] += jnp.dot(a.astype(BF16), wd_ref[...].astype(BF16), preferred_element_type=F32)

    @pl.when(n_tiles == FFN_TILES_PER_SEG)
    def _():
        rows_step(0, SEG_ROWS)

    @pl.when(n_tiles < FFN_TILES_PER_SEG)
    def _():
        def pair(i, carry):
            rows_step(i * (2 * FFN_TILE), 2 * FFN_TILE)
            return carry

        lax.fori_loop(0, lax.shift_right_logical(n_tiles, 1), pair, 0)

        @pl.when((n_tiles & 1) == 1)
        def _():
            rows_step((n_tiles - 1) * FFN_TILE, FFN_TILE)


def _ffn(seg_expert, seg_tiles, xs, wg, wu, wd):
    n_seg = xs.shape[0] // SEG_ROWS
    n_chunks = D_FF // FFN_CHUNK

    def chunk(s, f, st):
        return jnp.where(st[s] > 0, f, n_chunks - 1)

    grid_spec = pltpu.PrefetchScalarGridSpec(
        num_scalar_prefetch=2,
        grid=(n_seg, n_chunks),
        in_specs=[pl.BlockSpec((SEG_ROWS, D_MODEL), lambda s, f, se, st: (s, 0)),
                  pl.BlockSpec((None, D_MODEL, FFN_CHUNK), lambda s, f, se, st: (se[s], 0, chunk(s, f, st))),
                  pl.BlockSpec((None, D_MODEL, FFN_CHUNK), lambda s, f, se, st: (se[s], 0, chunk(s, f, st))),
                  pl.BlockSpec((None, FFN_CHUNK, D_MODEL), lambda s, f, se, st: (se[s], chunk(s, f, st), 0))],
        out_specs=pl.BlockSpec((SEG_ROWS, D_MODEL), lambda s, f, se, st: (s, 0)),
    )
    return pl.pallas_call(
        _ffn_kernel,
        grid_spec=grid_spec,
        out_shape=jax.ShapeDtypeStruct((n_seg * SEG_ROWS, D_MODEL), F32),
        compiler_params=_cparams(("arbitrary", "arbitrary"), VMEM_LIMIT),
        name="swiglu_ffn",
    )(seg_expert, seg_tiles, xs, wg, wu, wd)


def _post_kernel(y_ref, x_ref, gpost_ref, g2_ref, gpre_ref, sh_ref, sc_ref, xo_ref, h_ref):
    x = x_ref[...] + g2_ref[...] * _rmsnorm(y_ref[...], gpost_ref[...])
    xo_ref[...] = x
    h = _rmsnorm(x, gpre_ref[...]) * (1.0 + sc_ref[...]) + sh_ref[...]
    h_ref[...] = h.astype(h_ref.dtype)


def _post(y, x, g_post, g_pre_next, mods3, layer):
    row = pl.BlockSpec((ROW_TILE, D_MODEL), lambda i: (i, 0))
    return pl.pallas_call(
        _post_kernel,
        grid=(N_TILES_ALL,),
        in_specs=[row, row, _vec_spec(layer), _mod_spec(layer, 5, _tile_all),
                  _vec_spec(layer + 1), _mod_spec(layer + 1, 0, _tile_all), _mod_spec(layer + 1, 1, _tile_all)],
        out_specs=[row, row],
        out_shape=[jax.ShapeDtypeStruct((ROWS, D_MODEL), F32), jax.ShapeDtypeStruct((ROWS, D_MODEL), BF16)],
        compiler_params=_cparams(("arbitrary",)),
        name="post_ffn",
    )(y, x, g_post, mods3, g_pre_next, mods3, mods3)


def _combine_kernel(slot_ref, next_slot_ref, w_ref, x_ref, gpost_ref, g2_ref, ys_ref, o_ref, buf_ref, sem):
    i = pl.program_id(0)
    cur = i % 2

    def gather(slots, b):
        def start(r, carry):
            for k in range(TOP_K):
                src = slots[0, r * TOP_K + k]
                pltpu.make_async_copy(ys_ref.at[pl.ds(src, 1)], buf_ref.at[b, pl.ds(k * ROW_TILE + r, 1)],
                                      sem.at[b]).start()
            return carry
        lax.fori_loop(0, ROW_TILE, start, 0, unroll=DMA_UNROLL)

    @pl.when(i == 0)
    def _():
        gather(slot_ref, 0)

    @pl.when(i + 1 < pl.num_programs(0))
    def _():
        gather(next_slot_ref, 1 - cur)

    pltpu.make_async_copy(ys_ref.at[pl.ds(0, TOP_K * ROW_TILE)], buf_ref.at[cur], sem.at[cur]).wait()
    y = w_ref[:, 0:1] * buf_ref[cur, :ROW_TILE, :] + w_ref[:, 1:2] * buf_ref[cur, ROW_TILE:, :]
    o_ref[...] = x_ref[...] + g2_ref[...] * _rmsnorm(y, gpost_ref[...])


def _combine(slot3, weight, x, g_post, mods3, layer, ys):
    row = pl.BlockSpec((ROW_TILE, D_MODEL), lambda i: (i, 0))
    slots = lambda index: pl.BlockSpec((None, 1, ROW_TILE * TOP_K), index, memory_space=pltpu.SMEM)
    return pl.pallas_call(
        _combine_kernel,
        grid=(N_TILES_LATENT,),
        in_specs=[slots(lambda i: (i, 0, 0)),
                  slots(lambda i: (jnp.minimum(i + 1, N_TILES_LATENT - 1), 0, 0)),
                  pl.BlockSpec((ROW_TILE, TOP_K), lambda i: (i, 0)),
                  row, _vec_spec(layer), _mod_spec(layer, 5, _tile_latent),
                  pl.BlockSpec(memory_space=pl.ANY)],
        out_specs=row,
        out_shape=jax.ShapeDtypeStruct((BATCH * SEQ, D_MODEL), F32),
        scratch_shapes=[pltpu.VMEM((2, TOP_K * ROW_TILE, D_MODEL), F32), pltpu.SemaphoreType.DMA((2,))],
        compiler_params=_cparams(("arbitrary",), VMEM_LIMIT),
        name="moe_combine",
    )(slot3, slot3, weight, x, g_post, mods3, ys)


def _rope_table():
    pos = jnp.arange(SEQ, dtype=jnp.int32)
    row_ids = (pos // GRID_W).astype(F32)
    col_ids = (pos % GRID_W).astype(F32)
    n_freq = QK_ROPE // 4
    inv = ROPE_THETA ** (-jnp.arange(n_freq, dtype=F32) / n_freq)
    ang = jnp.concatenate([row_ids[:, None] * inv, col_ids[:, None] * inv], axis=-1)
    cos, sin = jnp.cos(ang), jnp.sin(ang)
    latent = jnp.concatenate([cos, cos, -sin, sin], axis=-1)
    ctx = jnp.concatenate([jnp.ones((CTX_LEN, QK_ROPE), F32), jnp.zeros((CTX_LEN, QK_ROPE), F32)], axis=-1)
    one = jnp.concatenate([ctx, latent], axis=0)
    return jnp.tile(one, (BATCH, 1))


def _swap_halves(w):
    half = w.shape[-1] // 2
    return jnp.concatenate([w[..., half:], w[..., :half]], axis=-1)


def _q_weights(w_q_up):
    w = w_q_up.reshape(Q_LORA, N_HEADS, QK_NOPE + QK_ROPE)
    rope = w[..., QK_NOPE:]
    w2 = jnp.concatenate([w[..., :QK_NOPE], rope, _swap_halves(rope)], axis=-1)
    return w2.reshape(Q_LORA, N_HEADS * K_WIDTH).astype(BF16)


def _routing(logits):
    probs = jax.nn.softmax(logits, axis=-1)
    top_p, top_i = lax.top_k(probs, TOP_K)
    top_p = top_p / jnp.sum(top_p, axis=-1, keepdims=True)
    onehot = jax.nn.one_hot(top_i.reshape(N_ASSIGN), N_EXPERTS, dtype=jnp.int32)
    csum = jnp.cumsum(onehot, axis=0)
    rank = jnp.sum((csum - onehot) * onehot, axis=1)
    counts = csum[-1]
    n_segs = (counts + SEG_ROWS - 1) // SEG_ROWS
    seg_start = jnp.cumsum(n_segs) - n_segs
    slot = jnp.sum(onehot * seg_start[None, :], axis=1) * SEG_ROWS + rank
    seg_ids = jnp.arange(MOE_SEGS, dtype=jnp.int32)
    used = jnp.sum(n_segs)
    seg_expert = jnp.sum((seg_ids[:, None] >= (seg_start + n_segs)[None, :]).astype(jnp.int32), axis=1)
    last_used_expert = jnp.max(jnp.where(counts > 0, jnp.arange(N_EXPERTS), 0))
    seg_expert = jnp.where(seg_ids < used, jnp.minimum(seg_expert, N_EXPERTS - 1), last_used_expert)
    rows_left = counts[seg_expert] - (seg_ids - seg_start[seg_expert]) * SEG_ROWS
    seg_tiles = jnp.clip((rows_left + FFN_TILE - 1) // FFN_TILE, 0, FFN_TILES_PER_SEG)
    seg_tiles = jnp.where(seg_ids < used, seg_tiles, 0)
    return (seg_expert.astype(jnp.int32), seg_tiles.astype(jnp.int32),
            slot.astype(jnp.int32).reshape(BATCH * SEQ, TOP_K), top_p)


def kernel(x, c, ctx, c_ctx, w_mod, b_mod, g_mix_pre, g_mix_post, g_ffn_pre, g_ffn_post, w_in, b_gate, g_q_lat, g_kv_lat, w_q_up, w_kv_up, w_pool, pool_scale, w_out, w_ff_gate, w_ff_up, w_ff_down, w_router, b_router, w_exp_gate, w_exp_up, w_exp_down):
    x2 = x.reshape(BATCH * SEQ, D_MODEL)
    ctx2 = ctx.reshape(BATCH * CTX_LEN, D_MODEL)

    c8 = jnp.concatenate([c, c_ctx[None, :], jnp.zeros((8 - BATCH - 1, D_MODEL), F32)], axis=0)
    mods = _modulation(c8, w_mod, b_mod)
    mods3 = mods[:, :BATCH + 1].reshape(DEPTH * 3 * 6, 1, D_MODEL)

    as_vec = lambda a: a.reshape(DEPTH, 1, a.shape[-1])
    g_mix_pre, g_mix_post, g_ffn_pre, g_ffn_post = map(as_vec, (g_mix_pre, g_mix_post, g_ffn_pre, g_ffn_post))
    g_q_lat, g_kv_lat, pool_scale = map(as_vec, (g_q_lat, g_kv_lat, pool_scale))
    b_gate2 = b_gate.reshape(DEPTH * 2, 1, D_MODEL)
    cs = _rope_table()

    h = _norm_mod(x2, ctx2, g_mix_pre, mods3, 0)
    xr = None
    out = None
    for l in range(DEPTH):
        last = l == DEPTH - 1
        w_main = jnp.concatenate([w_in[l][:, :OFF_KR], w_in[l][:, OFF_POOL:]], axis=1).astype(BF16)
        w_kr = w_in[l][:, OFF_KR:OFF_POOL]
        w_kr2 = jnp.concatenate([w_kr, _swap_halves(w_kr)], axis=1).astype(BF16)

        p = _matmul(h, w_main, 512, 1024)
        kr = _rope_key(h, w_kr2, cs)
        q = _q_up(p, g_q_lat, _q_weights(w_q_up[l]), cs, l)
        k, v = _kv_up(p, g_kv_lat, w_kv_up[l].astype(BF16), kr, l)

        attn = _attention(q, k, v, with_ctx=not last).reshape(-1, D_MODEL)
        tile_fn, n_tiles = (_tile_latent, N_TILES_LATENT) if last else (_tile_all, N_TILES_ALL)
        pool = _pool(p, w_pool[l].astype(BF16), pool_scale, l, tile_fn, n_tiles)
        x_args = (x2, ctx2) if l == 0 else (xr,)
        merge = functools.partial(_merge, p, attn, pool, b_gate2, w_out[l].astype(BF16), x_args, g_mix_post,
                                  g_ffn_pre, mods3, l, tile_fn, n_tiles)

        if not last:
            xr, h2 = merge()
            n_seg = ROWS // SEG_ROWS
            seg_expert = jnp.zeros((n_seg,), jnp.int32)
            seg_tiles = jnp.full((n_seg,), FFN_TILES_PER_SEG, jnp.int32)
            y = _ffn(seg_expert, seg_tiles, h2, w_ff_gate, w_ff_up, w_ff_down)
            xr, h = _post(y, xr, g_ffn_post, g_mix_pre, mods3, l)
        else:
            w_r = jnp.pad(w_router[0], ((0, 0), (0, 128 - N_EXPERTS)))
            b_r = jnp.pad(b_router[0], (0, 128 - N_EXPERTS)).reshape(1, 128)
            xl, h2, logits = merge(router=(w_r, b_r))
            seg_expert, seg_tiles, slot, weight = _routing(logits[:, :N_EXPERTS])
            slot3 = slot.reshape(N_TILES_LATENT, 1, ROW_TILE * TOP_K)
            xs = _dispatch(slot3, h2)
            ys = _ffn(seg_expert, seg_tiles, xs, w_exp_gate.reshape(N_EXPERTS, D_MODEL, D_FF),
                      w_exp_up.reshape(N_EXPERTS, D_MODEL, D_FF), w_exp_down.reshape(N_EXPERTS, D_FF, D_MODEL))
            out = _combine(slot3, weight, xl, g_ffn_post, mods3, l, ys)
    return out.reshape(BATCH, SEQ, D_MODEL)
```

```python
import functools
import math

import jax
import jax.numpy as jnp
from jax import lax
from jax.experimental import pallas as pl
from jax.experimental.pallas import tpu as pltpu

F32 = jnp.float32
BF16 = jnp.bfloat16

D_MODEL = 2048
BATCH = 2
SEQ = 4096
DEPTH = 2
GRID_W = 64
CTX_LEN = 256
EPS = 1e-6

QK_NOPE = 128
QK_ROPE = 64
V_DIM = 128
N_HEADS = 16
Q_LORA = 512
KV_LORA = 512
ROPE_THETA = 10000.0
ATTN_SCALE = (QK_NOPE + QK_ROPE) ** -0.5
Q_PRESCALE = ATTN_SCALE * math.log2(math.e)

POOL_WINDOWS = (2, 4, 8, 16)
N_POOL_GROUPS = 4
POOL_WIDTH = 1024
POOL_GROUP = 256
POOL_OUT_GROUP = 512

OFF_KR = 1024
OFF_POOL = 1088

D_FF = 7168
N_EXPERTS = 8
TOP_K = 2

T_ROWS = CTX_LEN + SEQ
ROWS = BATCH * T_ROWS
ROW_TILE = 256
TILES_PER_BATCH = T_ROWS // ROW_TILE
LATENT_TILES = SEQ // ROW_TILE
N_TILES_ALL = BATCH * TILES_PER_BATCH
N_TILES_LATENT = BATCH * LATENT_TILES
HALO = 16
K_WIDTH = 256
HEADS_PER_STEP = 8

COL_Q, COL_KV, COL_POOL, COL_GATE = 0, 512, 1024, 2048

FFN_TILE = 272
FFN_TILES_PER_SEG = 4
SEG_ROWS = FFN_TILE * FFN_TILES_PER_SEG
FFN_CHUNK = 256
N_ASSIGN = BATCH * SEQ * TOP_K
MOE_SEGS = N_ASSIGN // SEG_ROWS + N_EXPERTS

VMEM_LIMIT = 56 * 1024 * 1024


def _cparams(sem, vmem=None):
    return pltpu.CompilerParams(dimension_semantics=sem, vmem_limit_bytes=vmem)


def _tile_all(i):
    return i


def _tile_latent(i):
    return (i // LATENT_TILES) * TILES_PER_BATCH + 1 + i % LATENT_TILES


def _group_of_tile(t):
    return jnp.where(t % TILES_PER_BATCH == 0, BATCH, t // TILES_PER_BATCH)


def _mod_spec(layer, chunk, tile_fn):
    def index(i):
        return ((layer * 3 + _group_of_tile(tile_fn(i))) * 6 + chunk, 0, 0)
    return pl.BlockSpec((None, 1, D_MODEL), index)


def _vec_spec(layer, width=D_MODEL):
    return pl.BlockSpec((None, 1, width), lambda i: (layer, 0, 0))


def _input_stream_specs():
    def x_index(t):
        return ((t // TILES_PER_BATCH) * LATENT_TILES + jnp.maximum(t % TILES_PER_BATCH - 1, 0), 0)

    def ctx_index(t):
        return (t // TILES_PER_BATCH, 0)

    return (pl.BlockSpec((ROW_TILE, D_MODEL), x_index), pl.BlockSpec((CTX_LEN, D_MODEL), ctx_index))


def _stream_tile(x_ref, ctx_ref, tile):
    return jnp.where(tile % TILES_PER_BATCH == 0, ctx_ref[...], x_ref[...])


def _rmsnorm(x, g):
    return x * lax.rsqrt(jnp.mean(x * x, axis=-1, keepdims=True) + EPS) * g


def _mod_kernel(c_ref, w_ref, b_ref, o_ref):
    c = c_ref[...]
    a = c * jax.nn.sigmoid(c)
    o_ref[...] = jnp.dot(a, w_ref[...], preferred_element_type=F32) + b_ref[...]


def _modulation(c8, w_mod, b_mod):
    n = w_mod.shape[2]
    tn = 1024
    return pl.pallas_call(
        _mod_kernel,
        grid=(DEPTH, n // tn),
        in_specs=[pl.BlockSpec((8, D_MODEL), lambda l, j: (0, 0)),
                  pl.BlockSpec((None, D_MODEL, tn), lambda l, j: (l, 0, j)),
                  pl.BlockSpec((None, 1, tn), lambda l, j: (l, 0, j))],
        out_specs=pl.BlockSpec((None, 8, tn), lambda l, j: (l, 0, j)),
        out_shape=jax.ShapeDtypeStruct((DEPTH, 8, n), F32),
        compiler_params=_cparams(("arbitrary", "arbitrary"), VMEM_LIMIT),
        name="modulation",
    )(c8, w_mod, b_mod.reshape(DEPTH, 1, n))


def _norm_mod_kernel(x_ref, ctx_ref, g_ref, sh_ref, sc_ref, o_ref):
    y = _rmsnorm(_stream_tile(x_ref, ctx_ref, pl.program_id(0)), g_ref[...])
    o_ref[...] = (y * (1.0 + sc_ref[...]) + sh_ref[...]).astype(o_ref.dtype)


def _norm_mod(x2, ctx2, g, mods3, layer):
    x_spec, ctx_spec = _input_stream_specs()
    return pl.pallas_call(
        _norm_mod_kernel,
        grid=(N_TILES_ALL,),
        in_specs=[x_spec, ctx_spec, _vec_spec(layer),
                  _mod_spec(layer, 0, _tile_all), _mod_spec(layer, 1, _tile_all)],
        out_specs=pl.BlockSpec((ROW_TILE, D_MODEL), lambda i: (i, 0)),
        out_shape=jax.ShapeDtypeStruct((ROWS, D_MODEL), BF16),
        compiler_params=_cparams(("arbitrary",)),
        name="norm_mod",
    )(x2, ctx2, g, mods3, mods3)


def _matmul_kernel(a_ref, w_ref, o_ref):
    o_ref[...] = jnp.dot(a_ref[...], w_ref[...], preferred_element_type=F32).astype(o_ref.dtype)


def _matmul(a, w, layer, tm, tn):
    m, k = a.shape
    n = w.shape[2]
    return pl.pallas_call(
        _matmul_kernel,
        grid=(n // tn, m // tm),
        in_specs=[pl.BlockSpec((tm, k), lambda j, i: (i, 0)),
                  pl.BlockSpec((None, k, tn), lambda j, i: (layer, 0, j))],
        out_specs=pl.BlockSpec((tm, tn), lambda j, i: (i, j)),
        out_shape=jax.ShapeDtypeStruct((m, n), BF16),
        compiler_params=_cparams(("arbitrary", "arbitrary"), VMEM_LIMIT),
        name="in_proj",
    )(a, w)


def _rope_mix(t):
    lane = lax.broadcasted_iota(jnp.int32, t.shape, 1)
    return jnp.where(lane < QK_ROPE, t + pltpu.roll(t, QK_ROPE, 1), 0.0)


def _rope_key_kernel(a_ref, w_ref, cs_ref, o_ref):
    y = jnp.dot(a_ref[...], w_ref[...], preferred_element_type=F32)
    o_ref[...] = _rope_mix(y * cs_ref[...]).astype(o_ref.dtype)


def _rope_key(h, w_kr2, cs, layer):
    tm = 512
    return pl.pallas_call(
        _rope_key_kernel,
        grid=(ROWS // tm,),
        in_specs=[pl.BlockSpec((tm, D_MODEL), lambda i: (i, 0)),
                  pl.BlockSpec((None, D_MODEL, 128), lambda i: (layer, 0, 0)),
                  pl.BlockSpec((tm, 128), lambda i: (i, 0))],
        out_specs=pl.BlockSpec((tm, 128), lambda i: (i, 0)),
        out_shape=jax.ShapeDtypeStruct((ROWS, 128), BF16),
        compiler_params=_cparams(("arbitrary",)),
        name="rope_key",
    )(h, w_kr2, cs)


def _q_up_kernel(p_ref, g_ref, w_ref, cs_ref, q_ref):
    n = _rmsnorm(p_ref[...].astype(F32), g_ref[...]).astype(BF16)
    y = jnp.dot(n, w_ref[...], preferred_element_type=F32) * Q_PRESCALE
    cs = cs_ref[...]
    for h in range(N_HEADS):
        yh = y[:, h * K_WIDTH:(h + 1) * K_WIDTH]
        q_ref[h, :, :QK_NOPE] = yh[:, :QK_NOPE].astype(q_ref.dtype)
        q_ref[h, :, QK_NOPE:] = _rope_mix(yh[:, QK_NOPE:] * cs).astype(q_ref.dtype)


def _head_index(i):
    return (i // TILES_PER_BATCH, 0, i % TILES_PER_BATCH, 0)


def _q_up(p, g_q, w_q2, cs, layer):
    return pl.pallas_call(
        _q_up_kernel,
        grid=(N_TILES_ALL,),
        in_specs=[pl.BlockSpec((ROW_TILE, Q_LORA), lambda i: (i, COL_Q // Q_LORA)),
                  _vec_spec(layer, Q_LORA),
                  pl.BlockSpec((None, Q_LORA, N_HEADS * K_WIDTH), lambda i: (layer, 0, 0)),
                  pl.BlockSpec((ROW_TILE, 128), lambda i: (i, 0))],
        out_specs=pl.BlockSpec((None, N_HEADS, ROW_TILE, K_WIDTH), _head_index),
        out_shape=jax.ShapeDtypeStruct((BATCH, N_HEADS, T_ROWS, K_WIDTH), BF16),
        compiler_params=_cparams(("arbitrary",), VMEM_LIMIT),
        name="q_up",
    )(p, g_q, w_q2, cs)


def _kv_up_kernel(p_ref, g_ref, w_ref, kr_ref, k_ref, v_ref):
    n = _rmsnorm(p_ref[...].astype(F32), g_ref[...]).astype(BF16)
    y = jnp.dot(n, w_ref[...], preferred_element_type=F32)
    kr = kr_ref[...]
    for h in range(N_HEADS):
        base = h * (QK_NOPE + V_DIM)
        k_ref[h, :, :QK_NOPE] = y[:, base:base + QK_NOPE].astype(k_ref.dtype)
        k_ref[h, :, QK_NOPE:] = kr
        v_ref[h] = y[:, base + QK_NOPE:base + QK_NOPE + V_DIM].astype(v_ref.dtype)


def _kv_up(p, g_kv, w_kv, kr, layer):
    return pl.pallas_call(
        _kv_up_kernel,
        grid=(N_TILES_ALL,),
        in_specs=[pl.BlockSpec((ROW_TILE, KV_LORA), lambda i: (i, COL_KV // KV_LORA)),
                  _vec_spec(layer, KV_LORA),
                  pl.BlockSpec((None, KV_LORA, N_HEADS * (QK_NOPE + V_DIM)), lambda i: (layer, 0, 0)),
                  pl.BlockSpec((ROW_TILE, 128), lambda i: (i, 0))],
        out_specs=[pl.BlockSpec((None, N_HEADS, ROW_TILE, K_WIDTH), _head_index),
                   pl.BlockSpec((None, N_HEADS, ROW_TILE, V_DIM), _head_index)],
        out_shape=[jax.ShapeDtypeStruct((BATCH, N_HEADS, T_ROWS, K_WIDTH), BF16),
                   jax.ShapeDtypeStruct((BATCH, N_HEADS, T_ROWS, V_DIM), BF16)],
        compiler_params=_cparams(("arbitrary",), VMEM_LIMIT),
        name="kv_up",
    )(p, g_kv, w_kv, kr)


def _attend(q, k, v):
    s = lax.dot_general(q, k, (((1,), (1,)), ((), ())), preferred_element_type=F32)
    m = jnp.max(s, axis=-1, keepdims=True)
    e = jnp.exp2(s - m)
    denom = jnp.sum(e, axis=-1, keepdims=True)
    o = jnp.dot(e.astype(BF16), v, preferred_element_type=F32)
    return o / denom


def _attention_kernel(with_ctx, q_ref, k_ref, v_ref, o_ref):
    def run(n_keys):
        for h in range(HEADS_PER_STEP):
            o = _attend(q_ref[h], k_ref[h, :n_keys, :], v_ref[h, :n_keys, :])
            o_ref[:, h * V_DIM:(h + 1) * V_DIM] = o.astype(o_ref.dtype)

    if not with_ctx:
        run(T_ROWS)
        return

    pl.when(pl.program_id(2) == 0)(functools.partial(run, CTX_LEN))
    pl.when(pl.program_id(2) > 0)(functools.partial(run, T_ROWS))


def _attention(q, k, v, with_ctx):
    q0 = 0 if with_ctx else 1
    n_q = TILES_PER_BATCH - q0
    hps = HEADS_PER_STEP
    return pl.pallas_call(
        functools.partial(_attention_kernel, with_ctx),
        grid=(BATCH, N_HEADS // hps, n_q),
        in_specs=[pl.BlockSpec((None, hps, ROW_TILE, K_WIDTH), lambda b, h, i: (b, h, i + q0, 0)),
                  pl.BlockSpec((None, hps, T_ROWS, K_WIDTH), lambda b, h, i: (b, h, 0, 0),
                               pipeline_mode=pl.Buffered(1)),
                  pl.BlockSpec((None, hps, T_ROWS, V_DIM), lambda b, h, i: (b, h, 0, 0),
                               pipeline_mode=pl.Buffered(1))],
        out_specs=pl.BlockSpec((None, ROW_TILE, hps * V_DIM), lambda b, h, i: (b, i, h)),
        out_shape=jax.ShapeDtypeStruct((BATCH, n_q * ROW_TILE, D_MODEL), BF16),
        compiler_params=_cparams(("arbitrary", "arbitrary", "arbitrary"), VMEM_LIMIT),
        name="attention",
    )(q, k, v)


def _pool_kernel(tile_fn, prev_ref, cur_ref, next_ref, w_ref, ps_ref, o_ref, buf_ref):
    t = tile_fn(pl.program_id(0)) % TILES_PER_BATCH
    is_ctx = t == 0
    seg_len = jnp.where(is_ctx, CTX_LEN, SEQ)
    pos0 = jnp.where(is_ctx, 0, (t - 1) * ROW_TILE)
    has_prev = pos0 > 0
    has_next = pos0 + ROW_TILE < seg_len
    buf_ref[0:HALO, :] = jnp.where(has_prev, prev_ref[...].astype(F32), 0.0)
    buf_ref[HALO:HALO + ROW_TILE, :] = cur_ref[...].astype(F32)
    buf_ref[HALO + ROW_TILE:, :] = jnp.where(has_next, next_ref[...].astype(F32), 0.0)

    pos = pos0 + lax.broadcasted_iota(jnp.int32, (ROW_TILE, 1), 0)
    for g, win in enumerate(POOL_WINDOWS):
        cols = slice(g * POOL_GROUP, (g + 1) * POOL_GROUP)
        half = win // 2
        acc = buf_ref[HALO - half:HALO - half + ROW_TILE, cols]
        for j in range(1, win):
            acc = acc + buf_ref[HALO - half + j:HALO - half + j + ROW_TILE, cols]
        lo = jnp.maximum(pos - half, 0)
        hi = jnp.minimum(pos - half + win, seg_len)
        mean = acc / (hi - lo).astype(F32)
        pooled = (mean - buf_ref[HALO:HALO + ROW_TILE, cols]).astype(BF16)
        out = jnp.dot(pooled, w_ref[g], preferred_element_type=F32)
        ocols = slice(g * POOL_OUT_GROUP, (g + 1) * POOL_OUT_GROUP)
        o_ref[:, ocols] = (out * ps_ref[:, ocols]).astype(o_ref.dtype)


def _pool(p, w_pool, pool_scale, layer, tile_fn, n_tiles):
    per16 = ROW_TILE // HALO
    last16 = ROWS // HALO - 1
    col = COL_POOL // POOL_WIDTH
    return pl.pallas_call(
        functools.partial(_pool_kernel, tile_fn),
        grid=(n_tiles,),
        in_specs=[pl.BlockSpec((HALO, POOL_WIDTH), lambda i: (jnp.maximum(tile_fn(i) * per16 - 1, 0), col)),
                  pl.BlockSpec((ROW_TILE, POOL_WIDTH), lambda i: (tile_fn(i), col)),
                  pl.BlockSpec((HALO, POOL_WIDTH), lambda i: (jnp.minimum((tile_fn(i) + 1) * per16, last16), col)),
                  pl.BlockSpec((None, N_POOL_GROUPS, POOL_GROUP, POOL_OUT_GROUP), lambda i: (layer, 0, 0, 0)),
                  _vec_spec(layer)],
        out_specs=pl.BlockSpec((ROW_TILE, D_MODEL), lambda i: (i, 0)),
        out_shape=jax.ShapeDtypeStruct((n_tiles * ROW_TILE, D_MODEL), BF16),
        scratch_shapes=[pltpu.VMEM((ROW_TILE + 2 * HALO, POOL_WIDTH), F32)],
        compiler_params=_cparams(("arbitrary",)),
        name="pool",
    )(p, p, p, w_pool, pool_scale)


def _split_bf16(a):
    hi = a.astype(BF16)
    return hi, (a - hi.astype(F32)).astype(BF16)


def _merge_kernel(first, with_router, *refs):
    refs = list(refs)
    ga_ref, gb_ref, attn_ref, pool_ref, ba_ref, bb_ref, w_ref = refs[:7]
    del refs[:7]
    if first:
        x = _stream_tile(refs[0], refs[1], pl.program_id(0))
        del refs[:2]
    else:
        x = refs.pop(0)[...]
    gpost_ref, g1_ref, gpre_ref, sh_ref, sc_ref = refs[:5]
    del refs[:5]
    if with_router:
        wr_ref, br_ref, xo_ref, h_ref, lg_ref = refs
    else:
        xo_ref, h_ref = refs

    ga = jax.nn.sigmoid(ga_ref[...].astype(F32) + ba_ref[...])
    gb = jax.nn.sigmoid(gb_ref[...].astype(F32) + bb_ref[...])
    mixed = ga * attn_ref[...].astype(F32) + gb * pool_ref[...].astype(F32)
    y = jnp.dot(mixed.astype(BF16), w_ref[...], preferred_element_type=F32)
    x = x + g1_ref[...] * _rmsnorm(y, gpost_ref[...])
    xo_ref[...] = x
    h = _rmsnorm(x, gpre_ref[...]) * (1.0 + sc_ref[...]) + sh_ref[...]
    h_ref[...] = h.astype(BF16).astype(h_ref.dtype)
    if with_router:
        h_hi, h_lo = _split_bf16(h)
        w_hi, w_lo = _split_bf16(wr_ref[...])
        dot = functools.partial(jnp.dot, preferred_element_type=F32)
        lg_ref[...] = dot(h_hi, w_hi) + (dot(h_lo, w_hi) + dot(h_hi, w_lo)) + br_ref[...]


def _merge(p, attn, pool, b_gate2, w_out, x_args, g_post, g_pre, mods3, layer, tile_fn, n_tiles,
           router=None):
    first = len(x_args) == 2
    gate_a = COL_GATE // D_MODEL
    act = pl.BlockSpec((ROW_TILE, D_MODEL), lambda i: (i, 0))
    in_specs = [pl.BlockSpec((ROW_TILE, D_MODEL), lambda i: (tile_fn(i), gate_a)),
                pl.BlockSpec((ROW_TILE, D_MODEL), lambda i: (tile_fn(i), gate_a + 1)),
                act, act,
                pl.BlockSpec((None, 1, D_MODEL), lambda i: (layer * 2, 0, 0)),
                pl.BlockSpec((None, 1, D_MODEL), lambda i: (layer * 2 + 1, 0, 0)),
                pl.BlockSpec((None, D_MODEL, D_MODEL), lambda i: (layer, 0, 0))]
    if first:
        in_specs += list(_input_stream_specs())
    else:
        in_specs.append(pl.BlockSpec((ROW_TILE, D_MODEL), lambda i: (tile_fn(i), 0)))
    in_specs += [_vec_spec(layer), _mod_spec(layer, 2, tile_fn), _vec_spec(layer),
                 _mod_spec(layer, 3, tile_fn), _mod_spec(layer, 4, tile_fn)]
    args = [p, p, attn, pool, b_gate2, b_gate2, w_out, *x_args, g_post, mods3, g_pre, mods3, mods3]
    rows = n_tiles * ROW_TILE
    out_specs = [act, act]
    h_dtype = BF16 if router is None else F32
    out_shape = [jax.ShapeDtypeStruct((rows, D_MODEL), F32), jax.ShapeDtypeStruct((rows, D_MODEL), h_dtype)]
    if router is not None:
        in_specs += [pl.BlockSpec((D_MODEL, 128), lambda i: (0, 0)), pl.BlockSpec((1, 128), lambda i: (0, 0))]
        args += list(router)
        out_specs.append(pl.BlockSpec((ROW_TILE, 128), lambda i: (i, 0)))
        out_shape.append(jax.ShapeDtypeStruct((rows, 128), F32))
    return pl.pallas_call(
        functools.partial(_merge_kernel, first, router is not None),
        grid=(n_tiles,),
        in_specs=in_specs,
        out_specs=out_specs,
        out_shape=out_shape,
        compiler_params=_cparams(("arbitrary",), VMEM_LIMIT),
        name="merge_out_proj",
    )(*args)


DMA_UNROLL = 8


def _dispatch_kernel(slot_ref, h_ref, xs_in_ref, xs_ref, sem):
    del xs_in_ref

    def start(r, carry):
        for k in range(TOP_K):
            dst = slot_ref[0, r * TOP_K + k]
            pltpu.make_async_copy(h_ref.at[pl.ds(r, 1)], xs_ref.at[pl.ds(dst, 1)], sem).start()
        return carry

    lax.fori_loop(0, ROW_TILE, start, 0, unroll=DMA_UNROLL)
    for _ in range(TOP_K):
        pltpu.make_async_copy(h_ref, xs_ref.at[pl.ds(0, ROW_TILE)], sem).wait()


def _dispatch(slot3, h2):
    xs_init = jnp.zeros((MOE_SEGS * SEG_ROWS, D_MODEL), h2.dtype)
    return pl.pallas_call(
        _dispatch_kernel,
        grid=(N_TILES_LATENT,),
        in_specs=[pl.BlockSpec((None, 1, ROW_TILE * TOP_K), lambda i: (i, 0, 0), memory_space=pltpu.SMEM),
                  pl.BlockSpec((ROW_TILE, D_MODEL), lambda i: (i, 0)),
                  pl.BlockSpec(memory_space=pl.ANY)],
        out_specs=pl.BlockSpec(memory_space=pl.ANY),
        out_shape=jax.ShapeDtypeStruct(xs_init.shape, xs_init.dtype),
        input_output_aliases={2: 0},
        scratch_shapes=[pltpu.SemaphoreType.DMA(())],
        compiler_params=_cparams(("arbitrary",)),
        name="moe_dispatch",
    )(slot3, h2, xs_init)


def _ffn_kernel(se_ref, st_ref, x_ref, wg_ref, wu_ref, wd_ref, o_ref):
    del se_ref
    n_tiles = st_ref[pl.program_id(0)]

    @pl.when(pl.program_id(1) == 0)
    def _():
        o_ref[...] = jnp.zeros_like(o_ref)

    def rows_step(row0, n_rows):
        rows = pl.ds(row0 if isinstance(row0, int) else pl.multiple_of(row0, 16), n_rows)
        x = x_ref[rows, :].astype(BF16)
        g = jnp.dot(x, wg_ref[...].astype(BF16), preferred_element_type=F32)
        u = jnp.dot(x, wu_ref[...].astype(BF16), preferred_element_type=F32)
        a = (g * jax.nn.sigmoid(g)) * u
        o_ref[rows, :] += jnp.dot(a.astype(BF16), wd_ref[...].astype(BF16), preferred_element_type=F32)

    @pl.when(n_tiles == FFN_TILES_PER_SEG)
    def _():
        rows_step(0, SEG_ROWS)

    @pl.when(n_tiles < FFN_TILES_PER_SEG)
    def _():
        def pair(i, carry):
            rows_step(i * (2 * FFN_TILE), 2 * FFN_TILE)
            return carry

        lax.fori_loop(0, lax.shift_right_logical(n_tiles, 1), pair, 0)

        @pl.when((n_tiles & 1) == 1)
        def _():
            rows_step((n_tiles - 1) * FFN_TILE, FFN_TILE)


def _ffn(seg_expert, seg_tiles, xs, wg, wu, wd):
    n_seg = xs.shape[0] // SEG_ROWS
    tf = FFN_CHUNK * (jnp.dtype(F32).itemsize // jnp.dtype(xs.dtype).itemsize)
    n_chunks = D_FF // tf

    def chunk(s, f, st):
        return jnp.where(st[s] > 0, f, n_chunks - 1)

    grid_spec = pltpu.PrefetchScalarGridSpec(
        num_scalar_prefetch=2,
        grid=(n_seg, n_chunks),
        in_specs=[pl.BlockSpec((SEG_ROWS, D_MODEL), lambda s, f, se, st: (s, 0)),
                  pl.BlockSpec((None, D_MODEL, tf), lambda s, f, se, st: (se[s], 0, chunk(s, f, st))),
                  pl.BlockSpec((None, D_MODEL, tf), lambda s, f, se, st: (se[s], 0, chunk(s, f, st))),
                  pl.BlockSpec((None, tf, D_MODEL), lambda s, f, se, st: (se[s], chunk(s, f, st), 0))],
        out_specs=pl.BlockSpec((SEG_ROWS, D_MODEL), lambda s, f, se, st: (s, 0)),
    )
    return pl.pallas_call(
        _ffn_kernel,
        grid_spec=grid_spec,
        out_shape=jax.ShapeDtypeStruct((n_seg * SEG_ROWS, D_MODEL), F32),
        compiler_params=_cparams(("arbitrary", "arbitrary"), VMEM_LIMIT),
        name="swiglu_ffn",
    )(seg_expert, seg_tiles, xs, wg, wu, wd)


def _post_kernel(y_ref, x_ref, gpost_ref, g2_ref, gpre_ref, sh_ref, sc_ref, xo_ref, h_ref):
    x = x_ref[...] + g2_ref[...] * _rmsnorm(y_ref[...], gpost_ref[...])
    xo_ref[...] = x
    h = _rmsnorm(x, gpre_ref[...]) * (1.0 + sc_ref[...]) + sh_ref[...]
    h_ref[...] = h.astype(h_ref.dtype)


def _post(y, x, g_post, g_pre_next, mods3, layer):
    row = pl.BlockSpec((ROW_TILE, D_MODEL), lambda i: (i, 0))
    return pl.pallas_call(
        _post_kernel,
        grid=(N_TILES_ALL,),
        in_specs=[row, row, _vec_spec(layer), _mod_spec(layer, 5, _tile_all),
                  _vec_spec(layer + 1), _mod_spec(layer + 1, 0, _tile_all), _mod_spec(layer + 1, 1, _tile_all)],
        out_specs=[row, row],
        out_shape=[jax.ShapeDtypeStruct((ROWS, D_MODEL), F32), jax.ShapeDtypeStruct((ROWS, D_MODEL), BF16)],
        compiler_params=_cparams(("arbitrary",)),
        name="post_ffn",
    )(y, x, g_post, mods3, g_pre_next, mods3, mods3)


def _combine_kernel(slot_ref, next_slot_ref, w_ref, x_ref, gpost_ref, g2_ref, ys_ref, o_ref, buf_ref, sem):
    i = pl.program_id(0)
    cur = i % 2

    def gather(slots, b):
        def start(r, carry):
            for k in range(TOP_K):
                src = slots[0, r * TOP_K + k]
                pltpu.make_async_copy(ys_ref.at[pl.ds(src, 1)], buf_ref.at[b, pl.ds(k * ROW_TILE + r, 1)],
                                      sem.at[b]).start()
            return carry
        lax.fori_loop(0, ROW_TILE, start, 0, unroll=DMA_UNROLL)

    @pl.when(i == 0)
    def _():
        gather(slot_ref, 0)

    @pl.when(i + 1 < pl.num_programs(0))
    def _():
        gather(next_slot_ref, 1 - cur)

    pltpu.make_async_copy(ys_ref.at[pl.ds(0, TOP_K * ROW_TILE)], buf_ref.at[cur], sem.at[cur]).wait()
    y = w_ref[:, 0:1] * buf_ref[cur, :ROW_TILE, :] + w_ref[:, 1:2] * buf_ref[cur, ROW_TILE:, :]
    o_ref[...] = x_ref[...] + g2_ref[...] * _rmsnorm(y, gpost_ref[...])


def _combine(slot3, weight, x, g_post, mods3, layer, ys):
    row = pl.BlockSpec((ROW_TILE, D_MODEL), lambda i: (i, 0))
    slots = lambda index: pl.BlockSpec((None, 1, ROW_TILE * TOP_K), index, memory_space=pltpu.SMEM)
    return pl.pallas_call(
        _combine_kernel,
        grid=(N_TILES_LATENT,),
        in_specs=[slots(lambda i: (i, 0, 0)),
                  slots(lambda i: (jnp.minimum(i + 1, N_TILES_LATENT - 1), 0, 0)),
                  pl.BlockSpec((ROW_TILE, TOP_K), lambda i: (i, 0)),
                  row, _vec_spec(layer), _mod_spec(layer, 5, _tile_latent),
                  pl.BlockSpec(memory_space=pl.ANY)],
        out_specs=row,
        out_shape=jax.ShapeDtypeStruct((BATCH * SEQ, D_MODEL), F32),
        scratch_shapes=[pltpu.VMEM((2, TOP_K * ROW_TILE, D_MODEL), F32), pltpu.SemaphoreType.DMA((2,))],
        compiler_params=_cparams(("arbitrary",), VMEM_LIMIT),
        name="moe_combine",
    )(slot3, slot3, weight, x, g_post, mods3, ys)


def _rope_table():
    pos = jnp.arange(SEQ, dtype=jnp.int32)
    row_ids = (pos // GRID_W).astype(F32)
    col_ids = (pos % GRID_W).astype(F32)
    n_freq = QK_ROPE // 4
    inv = ROPE_THETA ** (-jnp.arange(n_freq, dtype=F32) / n_freq)
    ang = jnp.concatenate([row_ids[:, None] * inv, col_ids[:, None] * inv], axis=-1)
    cos, sin = jnp.cos(ang), jnp.sin(ang)
    latent = jnp.concatenate([cos, cos, -sin, sin], axis=-1)
    ctx = jnp.concatenate([jnp.ones((CTX_LEN, QK_ROPE), F32), jnp.zeros((CTX_LEN, QK_ROPE), F32)], axis=-1)
    one = jnp.concatenate([ctx, latent], axis=0)
    return jnp.tile(one, (BATCH, 1))


def _swap_halves(w):
    half = w.shape[-1] // 2
    return jnp.concatenate([w[..., half:], w[..., :half]], axis=-1)


def _q_weights(w_q_up):
    w = w_q_up.reshape(DEPTH, Q_LORA, N_HEADS, QK_NOPE + QK_ROPE)
    rope = w[..., QK_NOPE:]
    w2 = jnp.concatenate([w[..., :QK_NOPE], rope, _swap_halves(rope)], axis=-1)
    return w2.reshape(DEPTH, Q_LORA, N_HEADS * K_WIDTH).astype(BF16)


def _routing(logits):
    probs = jax.nn.softmax(logits, axis=-1)
    top_p, top_i = lax.top_k(probs, TOP_K)
    top_p = top_p / jnp.sum(top_p, axis=-1, keepdims=True)
    onehot = jax.nn.one_hot(top_i.reshape(N_ASSIGN), N_EXPERTS, dtype=jnp.int32)
    csum = jnp.cumsum(onehot, axis=0)
    rank = jnp.sum((csum - onehot) * onehot, axis=1)
    counts = csum[-1]
    n_segs = (counts + SEG_ROWS - 1) // SEG_ROWS
    seg_start = jnp.cumsum(n_segs) - n_segs
    slot = jnp.sum(onehot * seg_start[None, :], axis=1) * SEG_ROWS + rank
    seg_ids = jnp.arange(MOE_SEGS, dtype=jnp.int32)
    used = jnp.sum(n_segs)
    seg_expert = jnp.sum((seg_ids[:, None] >= (seg_start + n_segs)[None, :]).astype(jnp.int32), axis=1)
    last_used_expert = jnp.max(jnp.where(counts > 0, jnp.arange(N_EXPERTS), 0))
    seg_expert = jnp.where(seg_ids < used, jnp.minimum(seg_expert, N_EXPERTS - 1), last_used_expert)
    rows_left = counts[seg_expert] - (seg_ids - seg_start[seg_expert]) * SEG_ROWS
    seg_tiles = jnp.clip((rows_left + FFN_TILE - 1) // FFN_TILE, 0, FFN_TILES_PER_SEG)
    seg_tiles = jnp.where(seg_ids < used, seg_tiles, 0)
    return (seg_expert.astype(jnp.int32), seg_tiles.astype(jnp.int32),
            slot.astype(jnp.int32).reshape(BATCH * SEQ, TOP_K), top_p)


def kernel(x, c, ctx, c_ctx, w_mod, b_mod, g_mix_pre, g_mix_post, g_ffn_pre, g_ffn_post, w_in, b_gate, g_q_lat, g_kv_lat, w_q_up, w_kv_up, w_pool, pool_scale, w_out, w_ff_gate, w_ff_up, w_ff_down, w_router, b_router, w_exp_gate, w_exp_up, w_exp_down):
    x2 = x.reshape(BATCH * SEQ, D_MODEL)
    ctx2 = ctx.reshape(BATCH * CTX_LEN, D_MODEL)

    c8 = jnp.concatenate([c, c_ctx[None, :], jnp.zeros((8 - BATCH - 1, D_MODEL), F32)], axis=0)
    mods = _modulation(c8, w_mod, b_mod)
    mods3 = mods[:, :BATCH + 1].reshape(DEPTH * 3 * 6, 1, D_MODEL)

    as_vec = lambda a: a.reshape(DEPTH, 1, a.shape[-1])
    g_mix_pre, g_mix_post, g_ffn_pre, g_ffn_post = map(as_vec, (g_mix_pre, g_mix_post, g_ffn_pre, g_ffn_post))
    g_q_lat, g_kv_lat, pool_scale = map(as_vec, (g_q_lat, g_kv_lat, pool_scale))
    b_gate2 = b_gate.reshape(DEPTH * 2, 1, D_MODEL)
    cs = _rope_table()

    w_main = jnp.concatenate([w_in[:, :, :OFF_KR], w_in[:, :, OFF_POOL:]], axis=2).astype(BF16)
    w_kr = w_in[:, :, OFF_KR:OFF_POOL]
    w_kr2 = jnp.concatenate([w_kr, _swap_halves(w_kr)], axis=2).astype(BF16)
    w_q2 = _q_weights(w_q_up)
    w_kv, w_pool, w_out = (w.astype(BF16) for w in (w_kv_up, w_pool, w_out))

    h = _norm_mod(x2, ctx2, g_mix_pre, mods3, 0)
    xr = None
    out = None
    for l in range(DEPTH):
        last = l == DEPTH - 1
        p = _matmul(h, w_main, l, 1088, 1024)
        kr = _rope_key(h, w_kr2, cs, l)
        q = _q_up(p, g_q_lat, w_q2, cs, l)
        k, v = _kv_up(p, g_kv_lat, w_kv, kr, l)

        attn = _attention(q, k, v, with_ctx=not last).reshape(-1, D_MODEL)
        tile_fn, n_tiles = (_tile_latent, N_TILES_LATENT) if last else (_tile_all, N_TILES_ALL)
        pool = _pool(p, w_pool, pool_scale, l, tile_fn, n_tiles)
        x_args = (x2, ctx2) if l == 0 else (xr,)
        merge = functools.partial(_merge, p, attn, pool, b_gate2, w_out, x_args, g_mix_post,
                                  g_ffn_pre, mods3, l, tile_fn, n_tiles)

        if not last:
            xr, h2 = merge()
            n_seg = ROWS // SEG_ROWS
            seg_expert = jnp.zeros((n_seg,), jnp.int32)
            seg_tiles = jnp.full((n_seg,), FFN_TILES_PER_SEG, jnp.int32)
            y = _ffn(seg_expert, seg_tiles, h2, w_ff_gate, w_ff_up, w_ff_down)
            xr, h = _post(y, xr, g_ffn_post, g_mix_pre, mods3, l)
        else:
            w_r = jnp.pad(w_router[0], ((0, 0), (0, 128 - N_EXPERTS)))
            b_r = jnp.pad(b_router[0], (0, 128 - N_EXPERTS)).reshape(1, 128)
            xl, h2, logits = merge(router=(w_r, b_r))
            seg_expert, seg_tiles, slot, weight = _routing(logits[:, :N_EXPERTS])
            slot3 = slot.reshape(N_TILES_LATENT, 1, ROW_TILE * TOP_K)
            xs = _dispatch(slot3, h2)
            ys = _ffn(seg_expert, seg_tiles, xs, w_exp_gate.reshape(N_EXPERTS, D_MODEL, D_FF),
                      w_exp_up.reshape(N_EXPERTS, D_MODEL, D_FF), w_exp_down.reshape(N_EXPERTS, D_FF, D_MODEL))
            out = _combine(slot3, weight, xl, g_ffn_post, mods3, l, ys)
    return out.reshape(BATCH, SEQ, D_MODEL)
```

```python
import functools
import math

import jax
import jax.numpy as jnp
from jax import lax
from jax.experimental import pallas as pl
from jax.experimental.pallas import tpu as pltpu

F32 = jnp.float32
BF16 = jnp.bfloat16

D_MODEL = 2048
BATCH = 2
SEQ = 4096
DEPTH = 2
GRID_W = 64
CTX_LEN = 256
EPS = 1e-6

QK_NOPE = 128
QK_ROPE = 64
V_DIM = 128
N_HEADS = 16
Q_LORA = 512
KV_LORA = 512
ROPE_THETA = 10000.0
ATTN_SCALE = (QK_NOPE + QK_ROPE) ** -0.5
Q_PRESCALE = ATTN_SCALE * math.log2(math.e)

POOL_WINDOWS = (2, 4, 8, 16)
N_POOL_GROUPS = 4
POOL_WIDTH = 1024
POOL_GROUP = 256
POOL_OUT_GROUP = 512

OFF_KR = 1024
OFF_POOL = 1088

D_FF = 7168
N_EXPERTS = 8
TOP_K = 2

T_ROWS = CTX_LEN + SEQ
ROWS = BATCH * T_ROWS
ROW_TILE = 256
TILES_PER_BATCH = T_ROWS // ROW_TILE
LATENT_TILES = SEQ // ROW_TILE
N_TILES_ALL = BATCH * TILES_PER_BATCH
N_TILES_LATENT = BATCH * LATENT_TILES
HALO = 16
K_WIDTH = 256
HEADS_PER_STEP = 8

COL_Q, COL_KV, COL_POOL, COL_GATE = 0, 512, 1024, 2048

FFN_TILE = 272
FFN_TILES_PER_SEG = 4
SEG_ROWS = FFN_TILE * FFN_TILES_PER_SEG
FFN_CHUNK = 256
N_ASSIGN = BATCH * SEQ * TOP_K
MOE_SEGS = N_ASSIGN // SEG_ROWS + N_EXPERTS

VMEM_LIMIT = 56 * 1024 * 1024


def _cparams(sem, vmem=None):
    return pltpu.CompilerParams(dimension_semantics=sem, vmem_limit_bytes=vmem)


def _tile_all(i):
    return i


def _tile_latent(i):
    return (i // LATENT_TILES) * TILES_PER_BATCH + 1 + i % LATENT_TILES


def _group_of_tile(t):
    return jnp.where(t % TILES_PER_BATCH == 0, BATCH, t // TILES_PER_BATCH)


def _mod_spec(layer, chunk, tile_fn):
    def index(i):
        return ((layer * 3 + _group_of_tile(tile_fn(i))) * 6 + chunk, 0, 0)
    return pl.BlockSpec((None, 1, D_MODEL), index)


def _vec_spec(layer, width=D_MODEL):
    return pl.BlockSpec((None, 1, width), lambda i: (layer, 0, 0))


def _input_stream_specs():
    def x_index(t):
        return ((t // TILES_PER_BATCH) * LATENT_TILES + jnp.maximum(t % TILES_PER_BATCH - 1, 0), 0)

    def ctx_index(t):
        return (t // TILES_PER_BATCH, 0)

    return (pl.BlockSpec((ROW_TILE, D_MODEL), x_index), pl.BlockSpec((CTX_LEN, D_MODEL), ctx_index))


def _stream_tile(x_ref, ctx_ref, tile):
    return jnp.where(tile % TILES_PER_BATCH == 0, ctx_ref[...], x_ref[...])


def _rmsnorm(x, g):
    return x * lax.rsqrt(jnp.mean(x * x, axis=-1, keepdims=True) + EPS) * g


def _mod_kernel(c_ref, w_ref, b_ref, o_ref):
    c = c_ref[...]
    a = c * jax.nn.sigmoid(c)
    o_ref[...] = jnp.dot(a, w_ref[...], preferred_element_type=F32) + b_ref[...]


def _modulation(c8, w_mod, b_mod):
    n = w_mod.shape[2]
    tn = 1024
    return pl.pallas_call(
        _mod_kernel,
        grid=(DEPTH, n // tn),
        in_specs=[pl.BlockSpec((8, D_MODEL), lambda l, j: (0, 0)),
                  pl.BlockSpec((None, D_MODEL, tn), lambda l, j: (l, 0, j)),
                  pl.BlockSpec((None, 1, tn), lambda l, j: (l, 0, j))],
        out_specs=pl.BlockSpec((None, 8, tn), lambda l, j: (l, 0, j)),
        out_shape=jax.ShapeDtypeStruct((DEPTH, 8, n), F32),
        compiler_params=_cparams(("arbitrary", "arbitrary"), VMEM_LIMIT),
        name="modulation",
    )(c8, w_mod, b_mod.reshape(DEPTH, 1, n))


def _norm_mod_kernel(x_ref, ctx_ref, g_ref, sh_ref, sc_ref, o_ref):
    y = _rmsnorm(_stream_tile(x_ref, ctx_ref, pl.program_id(0)), g_ref[...])
    o_ref[...] = (y * (1.0 + sc_ref[...]) + sh_ref[...]).astype(o_ref.dtype)


def _norm_mod(x2, ctx2, g, mods3, layer):
    x_spec, ctx_spec = _input_stream_specs()
    return pl.pallas_call(
        _norm_mod_kernel,
        grid=(N_TILES_ALL,),
        in_specs=[x_spec, ctx_spec, _vec_spec(layer),
                  _mod_spec(layer, 0, _tile_all), _mod_spec(layer, 1, _tile_all)],
        out_specs=pl.BlockSpec((ROW_TILE, D_MODEL), lambda i: (i, 0)),
        out_shape=jax.ShapeDtypeStruct((ROWS, D_MODEL), BF16),
        compiler_params=_cparams(("arbitrary",)),
        name="norm_mod",
    )(x2, ctx2, g, mods3, mods3)


_CONTRACT_LAST = (((1,), (1,)), ((), ()))
IN_TILE = 1024
IN_ROWS = 1088
MAIN_WIDTH = Q_LORA + KV_LORA + POOL_WIDTH + 2 * D_MODEL


def _in_proj_kernel(layer, a_ref, wt_ref, o_ref, wbuf_ref, wcast_ref, sem):
    j = pl.program_id(0)

    def fetch(tile):
        row0 = tile * IN_TILE + jnp.where(tile * IN_TILE >= OFF_KR, QK_ROPE, 0)
        rows = pl.ds(pl.multiple_of(row0, QK_ROPE), IN_TILE)
        slot = tile % 2
        return pltpu.make_async_copy(wt_ref.at[layer, rows], wbuf_ref.at[slot], sem.at[slot])

    @pl.when(pl.program_id(1) == 0)
    def _():
        @pl.when(j == 0)
        def _():
            fetch(j).start()

        @pl.when(j + 1 < pl.num_programs(0))
        def _():
            fetch(j + 1).start()

        fetch(j).wait()
        wcast_ref[...] = wbuf_ref[j % 2].astype(wcast_ref.dtype)

    y = lax.dot_general(a_ref[...], wcast_ref[...], _CONTRACT_LAST, preferred_element_type=F32)
    o_ref[...] = y.astype(o_ref.dtype)


def _in_proj(a, w_in_t, layer):
    m, k = a.shape
    assert OFF_KR % IN_TILE == 0 and MAIN_WIDTH % IN_TILE == 0 and m % IN_ROWS == 0
    return pl.pallas_call(
        functools.partial(_in_proj_kernel, layer),
        grid=(MAIN_WIDTH // IN_TILE, m // IN_ROWS),
        in_specs=[pl.BlockSpec((IN_ROWS, k), lambda j, i: (i, 0)),
                  pl.BlockSpec(memory_space=pl.ANY)],
        out_specs=pl.BlockSpec((IN_ROWS, IN_TILE), lambda j, i: (i, j)),
        out_shape=jax.ShapeDtypeStruct((m, MAIN_WIDTH), BF16),
        scratch_shapes=[pltpu.VMEM((2, IN_TILE, k), F32), pltpu.VMEM((IN_TILE, k), BF16),
                        pltpu.SemaphoreType.DMA((2,))],
        compiler_params=_cparams(("arbitrary", "arbitrary"), VMEM_LIMIT),
        name="in_proj",
    )(a, w_in_t)


def _rope_rotate(t, cs):
    half = QK_ROPE // 2
    lane = lax.broadcasted_iota(jnp.int32, t.shape, 1)
    swapped = jnp.where(lane < half, pltpu.roll(t, 128 - half, 1), pltpu.roll(t, half, 1))
    s_tab = pltpu.roll(cs, QK_ROPE, 1)
    return jnp.where(lane < QK_ROPE, t * cs + swapped * s_tab, 0.0)


def _rope_key_kernel(a_ref, w_ref, cs_ref, o_ref):
    y = lax.dot_general(a_ref[...], w_ref[...].astype(BF16), _CONTRACT_LAST, preferred_element_type=F32)
    o_ref[...] = _rope_rotate(y, cs_ref[...]).astype(o_ref.dtype)


def _rope_key(h, w_in_t, cs, layer):
    tm = 512
    return pl.pallas_call(
        _rope_key_kernel,
        grid=(ROWS // tm,),
        in_specs=[pl.BlockSpec((tm, D_MODEL), lambda i: (i, 0)),
                  pl.BlockSpec((None, 128, D_MODEL), lambda i: (layer, OFF_KR // 128, 0)),
                  pl.BlockSpec((tm, 128), lambda i: (i, 0))],
        out_specs=pl.BlockSpec((tm, 128), lambda i: (i, 0)),
        out_shape=jax.ShapeDtypeStruct((ROWS, 128), BF16),
        compiler_params=_cparams(("arbitrary",)),
        name="rope_key",
    )(h, w_in_t, cs)


def _q_up_kernel(p_ref, g_ref, w_ref, cs_ref, q_ref):
    n = _rmsnorm(p_ref[...].astype(F32), g_ref[...]).astype(BF16)
    y = jnp.dot(n, w_ref[...].astype(BF16), preferred_element_type=F32) * Q_PRESCALE
    cs = cs_ref[...]
    head = QK_NOPE + QK_ROPE
    for h in range(N_HEADS):
        base = h * head
        q_ref[h, :, :QK_NOPE] = y[:, base:base + QK_NOPE].astype(q_ref.dtype)
        tail = pltpu.roll(y[:, base + head - 128:base + head], QK_ROPE, 1)
        q_ref[h, :, QK_NOPE:] = _rope_rotate(tail, cs).astype(q_ref.dtype)


def _head_index(i):
    return (i // TILES_PER_BATCH, 0, i % TILES_PER_BATCH, 0)


def _q_up(p, g_q, w_q_up, cs, layer):
    return pl.pallas_call(
        _q_up_kernel,
        grid=(N_TILES_ALL,),
        in_specs=[pl.BlockSpec((ROW_TILE, Q_LORA), lambda i: (i, COL_Q // Q_LORA)),
                  _vec_spec(layer, Q_LORA),
                  pl.BlockSpec((None, Q_LORA, N_HEADS * (QK_NOPE + QK_ROPE)), lambda i: (layer, 0, 0)),
                  pl.BlockSpec((ROW_TILE, 128), lambda i: (i, 0))],
        out_specs=pl.BlockSpec((None, N_HEADS, ROW_TILE, K_WIDTH), _head_index),
        out_shape=jax.ShapeDtypeStruct((BATCH, N_HEADS, T_ROWS, K_WIDTH), BF16),
        compiler_params=_cparams(("arbitrary",), VMEM_LIMIT),
        name="q_up",
    )(p, g_q, w_q_up, cs)


def _kv_up_kernel(p_ref, g_ref, w_ref, kr_ref, k_ref, v_ref):
    n = _rmsnorm(p_ref[...].astype(F32), g_ref[...]).astype(BF16)
    y = jnp.dot(n, w_ref[...], preferred_element_type=F32)
    kr = kr_ref[...]
    for h in range(N_HEADS):
        base = h * (QK_NOPE + V_DIM)
        k_ref[h, :, :QK_NOPE] = y[:, base:base + QK_NOPE].astype(k_ref.dtype)
        k_ref[h, :, QK_NOPE:] = kr
        v_ref[h] = y[:, base + QK_NOPE:base + QK_NOPE + V_DIM].astype(v_ref.dtype)


def _kv_up(p, g_kv, w_kv, kr, layer):
    return pl.pallas_call(
        _kv_up_kernel,
        grid=(N_TILES_ALL,),
        in_specs=[pl.BlockSpec((ROW_TILE, KV_LORA), lambda i: (i, COL_KV // KV_LORA)),
                  _vec_spec(layer, KV_LORA),
                  pl.BlockSpec((None, KV_LORA, N_HEADS * (QK_NOPE + V_DIM)), lambda i: (layer, 0, 0)),
                  pl.BlockSpec((ROW_TILE, 128), lambda i: (i, 0))],
        out_specs=[pl.BlockSpec((None, N_HEADS, ROW_TILE, K_WIDTH), _head_index),
                   pl.BlockSpec((None, N_HEADS, ROW_TILE, V_DIM), _head_index)],
        out_shape=[jax.ShapeDtypeStruct((BATCH, N_HEADS, T_ROWS, K_WIDTH), BF16),
                   jax.ShapeDtypeStruct((BATCH, N_HEADS, T_ROWS, V_DIM), BF16)],
        compiler_params=_cparams(("arbitrary",), VMEM_LIMIT),
        name="kv_up",
    )(p, g_kv, w_kv, kr)


def _attend(q, k, v):
    s = lax.dot_general(q, k, (((1,), (1,)), ((), ())), preferred_element_type=F32)
    m = jnp.max(s, axis=-1, keepdims=True)
    e = jnp.exp2(s - m)
    denom = jnp.sum(e, axis=-1, keepdims=True)
    o = jnp.dot(e.astype(BF16), v, preferred_element_type=F32)
    return o / denom


def _attention_kernel(with_ctx, q_ref, k_ref, v_ref, o_ref):
    def run(n_keys):
        for h in range(HEADS_PER_STEP):
            o = _attend(q_ref[h], k_ref[h, :n_keys, :], v_ref[h, :n_keys, :])
            o_ref[:, h * V_DIM:(h + 1) * V_DIM] = o.astype(o_ref.dtype)

    if not with_ctx:
        run(T_ROWS)
        return

    pl.when(pl.program_id(2) == 0)(functools.partial(run, CTX_LEN))
    pl.when(pl.program_id(2) > 0)(functools.partial(run, T_ROWS))


def _attention(q, k, v, with_ctx):
    q0 = 0 if with_ctx else 1
    n_q = TILES_PER_BATCH - q0
    hps = HEADS_PER_STEP
    return pl.pallas_call(
        functools.partial(_attention_kernel, with_ctx),
        grid=(BATCH, N_HEADS // hps, n_q),
        in_specs=[pl.BlockSpec((None, hps, ROW_TILE, K_WIDTH), lambda b, h, i: (b, h, i + q0, 0)),
                  pl.BlockSpec((None, hps, T_ROWS, K_WIDTH), lambda b, h, i: (b, h, 0, 0),
                               pipeline_mode=pl.Buffered(1)),
                  pl.BlockSpec((None, hps, T_ROWS, V_DIM), lambda b, h, i: (b, h, 0, 0),
                               pipeline_mode=pl.Buffered(1))],
        out_specs=pl.BlockSpec((None, ROW_TILE, hps * V_DIM), lambda b, h, i: (b, i, h)),
        out_shape=jax.ShapeDtypeStruct((BATCH, n_q * ROW_TILE, D_MODEL), BF16),
        compiler_params=_cparams(("arbitrary", "arbitrary", "arbitrary"), VMEM_LIMIT),
        name="attention",
    )(q, k, v)


def _pool_kernel(tile_fn, prev_ref, cur_ref, next_ref, w_ref, ps_ref, o_ref, buf_ref):
    t = tile_fn(pl.program_id(0)) % TILES_PER_BATCH
    is_ctx = t == 0
    seg_len = jnp.where(is_ctx, CTX_LEN, SEQ)
    pos0 = jnp.where(is_ctx, 0, (t - 1) * ROW_TILE)
    has_prev = pos0 > 0
    has_next = pos0 + ROW_TILE < seg_len
    buf_ref[0:HALO, :] = jnp.where(has_prev, prev_ref[...].astype(F32), 0.0)
    buf_ref[HALO:HALO + ROW_TILE, :] = cur_ref[...].astype(F32)
    buf_ref[HALO + ROW_TILE:, :] = jnp.where(has_next, next_ref[...].astype(F32), 0.0)

    pos = pos0 + lax.broadcasted_iota(jnp.int32, (ROW_TILE, 1), 0)
    for g, win in enumerate(POOL_WINDOWS):
        cols = slice(g * POOL_GROUP, (g + 1) * POOL_GROUP)
        half = win // 2
        acc = buf_ref[HALO - half:HALO - half + ROW_TILE, cols]
        for j in range(1, win):
            acc = acc + buf_ref[HALO - half + j:HALO - half + j + ROW_TILE, cols]
        lo = jnp.maximum(pos - half, 0)
        hi = jnp.minimum(pos - half + win, seg_len)
        mean = acc / (hi - lo).astype(F32)
        pooled = (mean - buf_ref[HALO:HALO + ROW_TILE, cols]).astype(BF16)
        out = jnp.dot(pooled, w_ref[g], preferred_element_type=F32)
        ocols = slice(g * POOL_OUT_GROUP, (g + 1) * POOL_OUT_GROUP)
        o_ref[:, ocols] = (out * ps_ref[:, ocols]).astype(o_ref.dtype)


def _pool(p, w_pool, pool_scale, layer, tile_fn, n_tiles):
    per16 = ROW_TILE // HALO
    last16 = ROWS // HALO - 1
    col = COL_POOL // POOL_WIDTH
    return pl.pallas_call(
        functools.partial(_pool_kernel, tile_fn),
        grid=(n_tiles,),
        in_specs=[pl.BlockSpec((HALO, POOL_WIDTH), lambda i: (jnp.maximum(tile_fn(i) * per16 - 1, 0), col)),
                  pl.BlockSpec((ROW_TILE, POOL_WIDTH), lambda i: (tile_fn(i), col)),
                  pl.BlockSpec((HALO, POOL_WIDTH), lambda i: (jnp.minimum((tile_fn(i) + 1) * per16, last16), col)),
                  pl.BlockSpec((None, N_POOL_GROUPS, POOL_GROUP, POOL_OUT_GROUP), lambda i: (layer, 0, 0, 0)),
                  _vec_spec(layer)],
        out_specs=pl.BlockSpec((ROW_TILE, D_MODEL), lambda i: (i, 0)),
        out_shape=jax.ShapeDtypeStruct((n_tiles * ROW_TILE, D_MODEL), BF16),
        scratch_shapes=[pltpu.VMEM((ROW_TILE + 2 * HALO, POOL_WIDTH), F32)],
        compiler_params=_cparams(("arbitrary",)),
        name="pool",
    )(p, p, p, w_pool, pool_scale)


def _split_bf16(a):
    hi = a.astype(BF16)
    return hi, (a - hi.astype(F32)).astype(BF16)


def _merge_kernel(first, with_router, *refs):
    refs = list(refs)
    ga_ref, gb_ref, attn_ref, pool_ref, ba_ref, bb_ref, w_ref = refs[:7]
    del refs[:7]
    if first:
        x = _stream_tile(refs[0], refs[1], pl.program_id(0))
        del refs[:2]
    else:
        x = refs.pop(0)[...]
    gpost_ref, g1_ref, gpre_ref, sh_ref, sc_ref = refs[:5]
    del refs[:5]
    if with_router:
        wr_ref, br_ref, xo_ref, h_ref, lg_ref = refs
    else:
        xo_ref, h_ref = refs

    ga = jax.nn.sigmoid(ga_ref[...].astype(F32) + ba_ref[...])
    gb = jax.nn.sigmoid(gb_ref[...].astype(F32) + bb_ref[...])
    mixed = ga * attn_ref[...].astype(F32) + gb * pool_ref[...].astype(F32)
    y = jnp.dot(mixed.astype(BF16), w_ref[...], preferred_element_type=F32)
    x = x + g1_ref[...] * _rmsnorm(y, gpost_ref[...])
    xo_ref[...] = x
    h = _rmsnorm(x, gpre_ref[...]) * (1.0 + sc_ref[...]) + sh_ref[...]
    h_ref[...] = h.astype(BF16).astype(h_ref.dtype)
    if with_router:
        h_hi, h_lo = _split_bf16(h)
        w_hi, w_lo = _split_bf16(wr_ref[...])
        dot = functools.partial(jnp.dot, preferred_element_type=F32)
        lg_ref[...] = dot(h_hi, w_hi) + (dot(h_lo, w_hi) + dot(h_hi, w_lo)) + br_ref[...]


def _merge(p, attn, pool, b_gate2, w_out, x_args, g_post, g_pre, mods3, layer, tile_fn, n_tiles,
           router=None):
    first = len(x_args) == 2
    gate_a = COL_GATE // D_MODEL
    act = pl.BlockSpec((ROW_TILE, D_MODEL), lambda i: (i, 0))
    in_specs = [pl.BlockSpec((ROW_TILE, D_MODEL), lambda i: (tile_fn(i), gate_a)),
                pl.BlockSpec((ROW_TILE, D_MODEL), lambda i: (tile_fn(i), gate_a + 1)),
                act, act,
                pl.BlockSpec((None, 1, D_MODEL), lambda i: (layer * 2, 0, 0)),
                pl.BlockSpec((None, 1, D_MODEL), lambda i: (layer * 2 + 1, 0, 0)),
                pl.BlockSpec((None, D_MODEL, D_MODEL), lambda i: (layer, 0, 0))]
    if first:
        in_specs += list(_input_stream_specs())
    else:
        in_specs.append(pl.BlockSpec((ROW_TILE, D_MODEL), lambda i: (tile_fn(i), 0)))
    in_specs += [_vec_spec(layer), _mod_spec(layer, 2, tile_fn), _vec_spec(layer),
                 _mod_spec(layer, 3, tile_fn), _mod_spec(layer, 4, tile_fn)]
    args = [p, p, attn, pool, b_gate2, b_gate2, w_out, *x_args, g_post, mods3, g_pre, mods3, mods3]
    rows = n_tiles * ROW_TILE
    out_specs = [act, act]
    h_dtype = BF16 if router is None else F32
    out_shape = [jax.ShapeDtypeStruct((rows, D_MODEL), F32), jax.ShapeDtypeStruct((rows, D_MODEL), h_dtype)]
    if router is not None:
        in_specs += [pl.BlockSpec((D_MODEL, 128), lambda i: (0, 0)), pl.BlockSpec((1, 128), lambda i: (0, 0))]
        args += list(router)
        out_specs.append(pl.BlockSpec((ROW_TILE, 128), lambda i: (i, 0)))
        out_shape.append(jax.ShapeDtypeStruct((rows, 128), F32))
    return pl.pallas_call(
        functools.partial(_merge_kernel, first, router is not None),
        grid=(n_tiles,),
        in_specs=in_specs,
        out_specs=out_specs,
        out_shape=out_shape,
        compiler_params=_cparams(("arbitrary",), VMEM_LIMIT),
        name="merge_out_proj",
    )(*args)


DMA_UNROLL = 8


def _dispatch_kernel(slot_ref, h_ref, xs_in_ref, xs_ref, sem):
    del xs_in_ref

    def start(r, carry):
        for k in range(TOP_K):
            dst = slot_ref[0, r * TOP_K + k]
            pltpu.make_async_copy(h_ref.at[pl.ds(r, 1)], xs_ref.at[pl.ds(dst, 1)], sem).start()
        return carry

    lax.fori_loop(0, ROW_TILE, start, 0, unroll=DMA_UNROLL)
    for _ in range(TOP_K):
        pltpu.make_async_copy(h_ref, xs_ref.at[pl.ds(0, ROW_TILE)], sem).wait()


def _dispatch(slot3, h2):
    xs_init = jnp.zeros((MOE_SEGS * SEG_ROWS, D_MODEL), h2.dtype)
    return pl.pallas_call(
        _dispatch_kernel,
        grid=(N_TILES_LATENT,),
        in_specs=[pl.BlockSpec((None, 1, ROW_TILE * TOP_K), lambda i: (i, 0, 0), memory_space=pltpu.SMEM),
                  pl.BlockSpec((ROW_TILE, D_MODEL), lambda i: (i, 0)),
                  pl.BlockSpec(memory_space=pl.ANY)],
        out_specs=pl.BlockSpec(memory_space=pl.ANY),
        out_shape=jax.ShapeDtypeStruct(xs_init.shape, xs_init.dtype),
        input_output_aliases={2: 0},
        scratch_shapes=[pltpu.SemaphoreType.DMA(())],
        compiler_params=_cparams(("arbitrary",)),
        name="moe_dispatch",
    )(slot3, h2, xs_init)


def _ffn_kernel(se_ref, st_ref, x_ref, wg_ref, wu_ref, wd_ref, o_ref):
    del se_ref
    n_tiles = st_ref[pl.program_id(0)]

    @pl.when(pl.program_id(1) == 0)
    def _():
        o_ref[...] = jnp.zeros_like(o_ref)

    def rows_step(row0, n_rows):
        rows = pl.ds(row0 if isinstance(row0, int) else pl.multiple_of(row0, 16), n_rows)
        x = x_ref[rows, :].astype(BF16)
        g = jnp.dot(x, wg_ref[...].astype(BF16), preferred_element_type=F32)
        u = jnp.dot(x, wu_ref[...].astype(BF16), preferred_element_type=F32)
        a = (g * jax.nn.sigmoid(g)) * u
        o_ref[rows, :] += jnp.dot(a.astype(BF16), wd_ref[...].astype(BF16), preferred_element_type=F32)

    @pl.when(n_tiles == FFN_TILES_PER_SEG)
    def _():
        rows_step(0, SEG_ROWS)

    @pl.when(n_tiles < FFN_TILES_PER_SEG)
    def _():
        def pair(i, carry):
            rows_step(i * (2 * FFN_TILE), 2 * FFN_TILE)
            return carry

        lax.fori_loop(0, lax.shift_right_logical(n_tiles, 1), pair, 0)

        @pl.when((n_tiles & 1) == 1)
        def _():
            rows_step((n_tiles - 1) * FFN_TILE, FFN_TILE)


def _ffn(seg_expert, seg_tiles, xs, wg, wu, wd):
    n_seg = xs.shape[0] // SEG_ROWS
    tf = FFN_CHUNK * (jnp.dtype(F32).itemsize // jnp.dtype(xs.dtype).itemsize)
    n_chunks = D_FF // tf

    def chunk(s, f, st):
        return jnp.where(st[s] > 0, f, n_chunks - 1)

    grid_spec = pltpu.PrefetchScalarGridSpec(
        num_scalar_prefetch=2,
        grid=(n_seg, n_chunks),
        in_specs=[pl.BlockSpec((SEG_ROWS, D_MODEL), lambda s, f, se, st: (s, 0)),
                  pl.BlockSpec((None, D_MODEL, tf), lambda s, f, se, st: (se[s], 0, chunk(s, f, st))),
                  pl.BlockSpec((None, D_MODEL, tf), lambda s, f, se, st: (se[s], 0, chunk(s, f, st))),
                  pl.BlockSpec((None, tf, D_MODEL), lambda s, f, se, st: (se[s], chunk(s, f, st), 0))],
        out_specs=pl.BlockSpec((SEG_ROWS, D_MODEL), lambda s, f, se, st: (s, 0)),
    )
    return pl.pallas_call(
        _ffn_kernel,
        grid_spec=grid_spec,
        out_shape=jax.ShapeDtypeStruct((n_seg * SEG_ROWS, D_MODEL), F32),
        compiler_params=_cparams(("arbitrary", "arbitrary"), VMEM_LIMIT),
        name="swiglu_ffn",
    )(seg_expert, seg_tiles, xs, wg, wu, wd)


def _post_kernel(y_ref, x_ref, gpost_ref, g2_ref, gpre_ref, sh_ref, sc_ref, xo_ref, h_ref):
    x = x_ref[...] + g2_ref[...] * _rmsnorm(y_ref[...], gpost_ref[...])
    xo_ref[...] = x
    h = _rmsnorm(x, gpre_ref[...]) * (1.0 + sc_ref[...]) + sh_ref[...]
    h_ref[...] = h.astype(h_ref.dtype)


def _post(y, x, g_post, g_pre_next, mods3, layer):
    row = pl.BlockSpec((ROW_TILE, D_MODEL), lambda i: (i, 0))
    return pl.pallas_call(
        _post_kernel,
        grid=(N_TILES_ALL,),
        in_specs=[row, row, _vec_spec(layer), _mod_spec(layer, 5, _tile_all),
                  _vec_spec(layer + 1), _mod_spec(layer + 1, 0, _tile_all), _mod_spec(layer + 1, 1, _tile_all)],
        out_specs=[row, row],
        out_shape=[jax.ShapeDtypeStruct((ROWS, D_MODEL), F32), jax.ShapeDtypeStruct((ROWS, D_MODEL), BF16)],
        compiler_params=_cparams(("arbitrary",)),
        name="post_ffn",
    )(y, x, g_post, mods3, g_pre_next, mods3, mods3)


def _combine_kernel(slot_ref, next_slot_ref, w_ref, x_ref, gpost_ref, g2_ref, ys_ref, o_ref, buf_ref, sem):
    i = pl.program_id(0)
    cur = i % 2

    def gather(slots, b):
        def start(r, carry):
            for k in range(TOP_K):
                src = slots[0, r * TOP_K + k]
                pltpu.make_async_copy(ys_ref.at[pl.ds(src, 1)], buf_ref.at[b, pl.ds(k * ROW_TILE + r, 1)],
                                      sem.at[b]).start()
            return carry
        lax.fori_loop(0, ROW_TILE, start, 0, unroll=DMA_UNROLL)

    @pl.when(i == 0)
    def _():
        gather(slot_ref, 0)

    @pl.when(i + 1 < pl.num_programs(0))
    def _():
        gather(next_slot_ref, 1 - cur)

    pltpu.make_async_copy(ys_ref.at[pl.ds(0, TOP_K * ROW_TILE)], buf_ref.at[cur], sem.at[cur]).wait()
    y = w_ref[:, 0:1] * buf_ref[cur, :ROW_TILE, :] + w_ref[:, 1:2] * buf_ref[cur, ROW_TILE:, :]
    o_ref[...] = x_ref[...] + g2_ref[...] * _rmsnorm(y, gpost_ref[...])


def _combine(slot3, weight, x, g_post, mods3, layer, ys):
    row = pl.BlockSpec((ROW_TILE, D_MODEL), lambda i: (i, 0))
    slots = lambda index: pl.BlockSpec((None, 1, ROW_TILE * TOP_K), index, memory_space=pltpu.SMEM)
    return pl.pallas_call(
        _combine_kernel,
        grid=(N_TILES_LATENT,),
        in_specs=[slots(lambda i: (i, 0, 0)),
                  slots(lambda i: (jnp.minimum(i + 1, N_TILES_LATENT - 1), 0, 0)),
                  pl.BlockSpec((ROW_TILE, TOP_K), lambda i: (i, 0)),
                  row, _vec_spec(layer), _mod_spec(layer, 5, _tile_latent),
                  pl.BlockSpec(memory_space=pl.ANY)],
        out_specs=row,
        out_shape=jax.ShapeDtypeStruct((BATCH * SEQ, D_MODEL), F32),
        scratch_shapes=[pltpu.VMEM((2, TOP_K * ROW_TILE, D_MODEL), F32), pltpu.SemaphoreType.DMA((2,))],
        compiler_params=_cparams(("arbitrary",), VMEM_LIMIT),
        name="moe_combine",
    )(slot3, slot3, weight, x, g_post, mods3, ys)


def _rope_table():
    pos = jnp.arange(SEQ, dtype=jnp.int32)
    row_ids = (pos // GRID_W).astype(F32)
    col_ids = (pos % GRID_W).astype(F32)
    n_freq = QK_ROPE // 4
    inv = ROPE_THETA ** (-jnp.arange(n_freq, dtype=F32) / n_freq)
    ang = jnp.concatenate([row_ids[:, None] * inv, col_ids[:, None] * inv], axis=-1)
    cos, sin = jnp.cos(ang), jnp.sin(ang)
    latent = jnp.concatenate([cos, cos, -sin, sin], axis=-1)
    ctx = jnp.concatenate([jnp.ones((CTX_LEN, QK_ROPE), F32), jnp.zeros((CTX_LEN, QK_ROPE), F32)], axis=-1)
    one = jnp.concatenate([ctx, latent], axis=0)
    return jnp.tile(one, (BATCH, 1))


def _routing(logits):
    probs = jax.nn.softmax(logits, axis=-1)
    top_p, top_i = lax.top_k(probs, TOP_K)
    top_p = top_p / jnp.sum(top_p, axis=-1, keepdims=True)
    onehot = jax.nn.one_hot(top_i.reshape(N_ASSIGN), N_EXPERTS, dtype=jnp.int32)
    csum = jnp.cumsum(onehot, axis=0)
    rank = jnp.sum((csum - onehot) * onehot, axis=1)
    counts = csum[-1]
    n_segs = (counts + SEG_ROWS - 1) // SEG_ROWS
    seg_start = jnp.cumsum(n_segs) - n_segs
    slot = jnp.sum(onehot * seg_start[None, :], axis=1) * SEG_ROWS + rank
    seg_ids = jnp.arange(MOE_SEGS, dtype=jnp.int32)
    used = jnp.sum(n_segs)
    seg_expert = jnp.sum((seg_ids[:, None] >= (seg_start + n_segs)[None, :]).astype(jnp.int32), axis=1)
    last_used_expert = jnp.max(jnp.where(counts > 0, jnp.arange(N_EXPERTS), 0))
    seg_expert = jnp.where(seg_ids < used, jnp.minimum(seg_expert, N_EXPERTS - 1), last_used_expert)
    rows_left = counts[seg_expert] - (seg_ids - seg_start[seg_expert]) * SEG_ROWS
    seg_tiles = jnp.clip((rows_left + FFN_TILE - 1) // FFN_TILE, 0, FFN_TILES_PER_SEG)
    seg_tiles = jnp.where(seg_ids < used, seg_tiles, 0)
    return (seg_expert.astype(jnp.int32), seg_tiles.astype(jnp.int32),
            slot.astype(jnp.int32).reshape(BATCH * SEQ, TOP_K), top_p)


def kernel(x, c, ctx, c_ctx, w_mod, b_mod, g_mix_pre, g_mix_post, g_ffn_pre, g_ffn_post, w_in, b_gate, g_q_lat, g_kv_lat, w_q_up, w_kv_up, w_pool, pool_scale, w_out, w_ff_gate, w_ff_up, w_ff_down, w_router, b_router, w_exp_gate, w_exp_up, w_exp_down):
    x2 = x.reshape(BATCH * SEQ, D_MODEL)
    ctx2 = ctx.reshape(BATCH * CTX_LEN, D_MODEL)

    c8 = jnp.concatenate([c, c_ctx[None, :], jnp.zeros((8 - BATCH - 1, D_MODEL), F32)], axis=0)
    mods = _modulation(c8, w_mod, b_mod)
    mods3 = mods[:, :BATCH + 1].reshape(DEPTH * 3 * 6, 1, D_MODEL)

    as_vec = lambda a: a.reshape(DEPTH, 1, a.shape[-1])
    g_mix_pre, g_mix_post, g_ffn_pre, g_ffn_post = map(as_vec, (g_mix_pre, g_mix_post, g_ffn_pre, g_ffn_post))
    g_q_lat, g_kv_lat, pool_scale = map(as_vec, (g_q_lat, g_kv_lat, pool_scale))
    b_gate2 = b_gate.reshape(DEPTH * 2, 1, D_MODEL)
    cs = _rope_table()

    w_in_t = jnp.swapaxes(w_in, 1, 2)
    w_kv, w_pool, w_out = (w.astype(BF16) for w in (w_kv_up, w_pool, w_out))

    h = _norm_mod(x2, ctx2, g_mix_pre, mods3, 0)
    xr = None
    out = None
    for l in range(DEPTH):
        last = l == DEPTH - 1
        p = _in_proj(h, w_in_t, l)
        kr = _rope_key(h, w_in_t, cs, l)
        q = _q_up(p, g_q_lat, w_q_up, cs, l)
        k, v = _kv_up(p, g_kv_lat, w_kv, kr, l)

        attn = _attention(q, k, v, with_ctx=not last).reshape(-1, D_MODEL)
        tile_fn, n_tiles = (_tile_latent, N_TILES_LATENT) if last else (_tile_all, N_TILES_ALL)
        pool = _pool(p, w_pool, pool_scale, l, tile_fn, n_tiles)
        x_args = (x2, ctx2) if l == 0 else (xr,)
        merge = functools.partial(_merge, p, attn, pool, b_gate2, w_out, x_args, g_mix_post,
                                  g_ffn_pre, mods3, l, tile_fn, n_tiles)

        if not last:
            xr, h2 = merge()
            n_seg = ROWS // SEG_ROWS
            seg_expert = jnp.zeros((n_seg,), jnp.int32)
            seg_tiles = jnp.full((n_seg,), FFN_TILES_PER_SEG, jnp.int32)
            y = _ffn(seg_expert, seg_tiles, h2, w_ff_gate, w_ff_up, w_ff_down)
            xr, h = _post(y, xr, g_ffn_post, g_mix_pre, mods3, l)
        else:
            w_r = jnp.pad(w_router[0], ((0, 0), (0, 128 - N_EXPERTS)))
            b_r = jnp.pad(b_router[0], (0, 128 - N_EXPERTS)).reshape(1, 128)
            xl, h2, logits = merge(router=(w_r, b_r))
            seg_expert, seg_tiles, slot, weight = _routing(logits[:, :N_EXPERTS])
            slot3 = slot.reshape(N_TILES_LATENT, 1, ROW_TILE * TOP_K)
            xs = _dispatch(slot3, h2)
            ys = _ffn(seg_expert, seg_tiles, xs, w_exp_gate.reshape(N_EXPERTS, D_MODEL, D_FF),
                      w_exp_up.reshape(N_EXPERTS, D_MODEL, D_FF), w_exp_down.reshape(N_EXPERTS, D_FF, D_MODEL))
            out = _combine(slot3, weight, xl, g_ffn_post, mods3, l, ys)
    return out.reshape(BATCH, SEQ, D_MODEL)
```

```python
import functools
import math

import jax
import jax.numpy as jnp
from jax import lax
from jax.experimental import pallas as pl
from jax.experimental.pallas import tpu as pltpu

F32 = jnp.float32
BF16 = jnp.bfloat16

D_MODEL = 2048
BATCH = 2
SEQ = 4096
DEPTH = 2
GRID_W = 64
CTX_LEN = 256
EPS = 1e-6

QK_NOPE = 128
QK_ROPE = 64
V_DIM = 128
N_HEADS = 16
Q_LORA = 512
KV_LORA = 512
ROPE_THETA = 10000.0
ATTN_SCALE = (QK_NOPE + QK_ROPE) ** -0.5
Q_PRESCALE = ATTN_SCALE * math.log2(math.e)

POOL_WINDOWS = (2, 4, 8, 16)
N_POOL_GROUPS = 4
POOL_WIDTH = 1024
POOL_GROUP = 256
POOL_OUT_GROUP = 512

OFF_KR = 1024
OFF_POOL = 1088

D_FF = 7168
N_EXPERTS = 8
TOP_K = 2

T_ROWS = CTX_LEN + SEQ
ROWS = BATCH * T_ROWS
ROW_TILE = 256
TILES_PER_BATCH = T_ROWS // ROW_TILE
LATENT_TILES = SEQ // ROW_TILE
N_TILES_ALL = BATCH * TILES_PER_BATCH
N_TILES_LATENT = BATCH * LATENT_TILES
HALO = 16
K_WIDTH = 256
HEADS_PER_STEP = 8

COL_Q, COL_KV, COL_POOL, COL_GATE = 0, 512, 1024, 2048

FFN_TILE = 272
FFN_TILES_PER_SEG = 4
SEG_ROWS = FFN_TILE * FFN_TILES_PER_SEG
FFN_CHUNK = 256
N_ASSIGN = BATCH * SEQ * TOP_K
MOE_SEGS = N_ASSIGN // SEG_ROWS + N_EXPERTS

VMEM_LIMIT = 56 * 1024 * 1024


def _cparams(sem, vmem=None):
    return pltpu.CompilerParams(dimension_semantics=sem, vmem_limit_bytes=vmem)


def _tile_all(i):
    return i


def _tile_latent(i):
    return (i // LATENT_TILES) * TILES_PER_BATCH + 1 + i % LATENT_TILES


def _group_of_tile(t):
    return jnp.where(t % TILES_PER_BATCH == 0, BATCH, t // TILES_PER_BATCH)


def _mod_spec(layer, chunk, tile_fn):
    def index(i):
        return ((layer * 3 + _group_of_tile(tile_fn(i))) * 6 + chunk, 0, 0)
    return pl.BlockSpec((None, 1, D_MODEL), index)


def _vec_spec(layer, width=D_MODEL):
    return pl.BlockSpec((None, 1, width), lambda i: (layer, 0, 0))


def _input_stream_specs():
    def x_index(t):
        return ((t // TILES_PER_BATCH) * LATENT_TILES + jnp.maximum(t % TILES_PER_BATCH - 1, 0), 0)

    def ctx_index(t):
        return (t // TILES_PER_BATCH, 0)

    return (pl.BlockSpec((ROW_TILE, D_MODEL), x_index), pl.BlockSpec((CTX_LEN, D_MODEL), ctx_index))


def _stream_tile(x_ref, ctx_ref, tile):
    return jnp.where(tile % TILES_PER_BATCH == 0, ctx_ref[...], x_ref[...])


def _rmsnorm(x, g):
    return x * lax.rsqrt(jnp.mean(x * x, axis=-1, keepdims=True) + EPS) * g


def _mod_kernel(c_ref, w_ref, b_ref, o_ref):
    c = c_ref[...]
    a = c * jax.nn.sigmoid(c)
    o_ref[...] = jnp.dot(a, w_ref[...], preferred_element_type=F32) + b_ref[...]


def _modulation(c8, w_mod, b_mod):
    n = w_mod.shape[2]
    tn = 1024
    return pl.pallas_call(
        _mod_kernel,
        grid=(DEPTH, n // tn),
        in_specs=[pl.BlockSpec((8, D_MODEL), lambda l, j: (0, 0)),
                  pl.BlockSpec((None, D_MODEL, tn), lambda l, j: (l, 0, j)),
                  pl.BlockSpec((None, 1, tn), lambda l, j: (l, 0, j))],
        out_specs=pl.BlockSpec((None, 8, tn), lambda l, j: (l, 0, j)),
        out_shape=jax.ShapeDtypeStruct((DEPTH, 8, n), F32),
        compiler_params=_cparams(("arbitrary", "arbitrary"), VMEM_LIMIT),
        name="modulation",
    )(c8, w_mod, b_mod.reshape(DEPTH, 1, n))


def _norm_mod_kernel(x_ref, ctx_ref, g_ref, sh_ref, sc_ref, o_ref):
    y = _rmsnorm(_stream_tile(x_ref, ctx_ref, pl.program_id(0)), g_ref[...])
    o_ref[...] = (y * (1.0 + sc_ref[...]) + sh_ref[...]).astype(o_ref.dtype)


def _norm_mod(x2, ctx2, g, mods3, layer):
    x_spec, ctx_spec = _input_stream_specs()
    return pl.pallas_call(
        _norm_mod_kernel,
        grid=(N_TILES_ALL,),
        in_specs=[x_spec, ctx_spec, _vec_spec(layer),
                  _mod_spec(layer, 0, _tile_all), _mod_spec(layer, 1, _tile_all)],
        out_specs=pl.BlockSpec((ROW_TILE, D_MODEL), lambda i: (i, 0)),
        out_shape=jax.ShapeDtypeStruct((ROWS, D_MODEL), BF16),
        compiler_params=_cparams(("arbitrary",)),
        name="norm_mod",
    )(x2, ctx2, g, mods3, mods3)


_CONTRACT_LAST = (((1,), (1,)), ((), ()))
IN_TILE = 1024
IN_ROWS = 1088
MAIN_WIDTH = Q_LORA + KV_LORA + POOL_WIDTH + 2 * D_MODEL


def _in_proj_kernel(layer, a_ref, wt_ref, o_ref, wbuf_ref, wcast_ref, sem):
    j = pl.program_id(0)

    def fetch(tile):
        row0 = tile * IN_TILE + jnp.where(tile * IN_TILE >= OFF_KR, QK_ROPE, 0)
        rows = pl.ds(pl.multiple_of(row0, QK_ROPE), IN_TILE)
        slot = tile % 2
        return pltpu.make_async_copy(wt_ref.at[layer, rows], wbuf_ref.at[slot], sem.at[slot])

    @pl.when(pl.program_id(1) == 0)
    def _():
        @pl.when(j == 0)
        def _():
            fetch(j).start()

        @pl.when(j + 1 < pl.num_programs(0))
        def _():
            fetch(j + 1).start()

        fetch(j).wait()
        wcast_ref[...] = wbuf_ref[j % 2].astype(wcast_ref.dtype)

    y = lax.dot_general(a_ref[...], wcast_ref[...], _CONTRACT_LAST, preferred_element_type=F32)
    o_ref[...] = y.astype(o_ref.dtype)


def _in_proj(a, w_in_t, layer):
    m, k = a.shape
    assert OFF_KR % IN_TILE == 0 and MAIN_WIDTH % IN_TILE == 0 and m % IN_ROWS == 0
    return pl.pallas_call(
        functools.partial(_in_proj_kernel, layer),
        grid=(MAIN_WIDTH // IN_TILE, m // IN_ROWS),
        in_specs=[pl.BlockSpec((IN_ROWS, k), lambda j, i: (i, 0)),
                  pl.BlockSpec(memory_space=pl.ANY)],
        out_specs=pl.BlockSpec((IN_ROWS, IN_TILE), lambda j, i: (i, j)),
        out_shape=jax.ShapeDtypeStruct((m, MAIN_WIDTH), BF16),
        scratch_shapes=[pltpu.VMEM((2, IN_TILE, k), F32), pltpu.VMEM((IN_TILE, k), BF16),
                        pltpu.SemaphoreType.DMA((2,))],
        compiler_params=_cparams(("arbitrary", "arbitrary"), VMEM_LIMIT),
        name="in_proj",
    )(a, w_in_t)


def _rope_rotate(t, cs):
    lane = lax.broadcasted_iota(jnp.int32, t.shape, 1)
    s_tab = pltpu.roll(cs, QK_ROPE, 1)
    return jnp.where(lane < QK_ROPE, t * cs + _swap_rope_halves(t) * s_tab, 0.0)


def _swap_rope_halves(t):
    half = QK_ROPE // 2
    lane = lax.broadcasted_iota(jnp.int32, t.shape, 1)
    return jnp.where(lane < half, pltpu.roll(t, 128 - half, 1), pltpu.roll(t, half, 1))


def _q_up_kernel(p_ref, g_ref, w_ref, cs_ref, q_ref, w2_ref):
    head = QK_NOPE + QK_ROPE

    @pl.when(pl.program_id(0) == 0)
    def _():
        lane = lax.broadcasted_iota(jnp.int32, (Q_LORA, 128), 1)
        w = w_ref[...]
        for h in range(N_HEADS):
            base = h * head
            w2_ref[:, h * K_WIDTH:h * K_WIDTH + QK_NOPE] = w[:, base:base + QK_NOPE].astype(w2_ref.dtype)
            tail = pltpu.roll(w[:, base + head - 128:base + head], QK_ROPE, 1)
            both = jnp.where(lane < QK_ROPE, tail, pltpu.roll(_swap_rope_halves(tail), QK_ROPE, 1))
            w2_ref[:, h * K_WIDTH + QK_NOPE:(h + 1) * K_WIDTH] = both.astype(w2_ref.dtype)

    n = _rmsnorm(p_ref[...].astype(F32), g_ref[...]).astype(BF16)
    y = jnp.dot(n, w2_ref[...], preferred_element_type=F32) * Q_PRESCALE
    cs = cs_ref[...]
    lane = lax.broadcasted_iota(jnp.int32, (ROW_TILE, 128), 1)
    for h in range(N_HEADS):
        yh = y[:, h * K_WIDTH:(h + 1) * K_WIDTH]
        q_ref[h, :, :QK_NOPE] = yh[:, :QK_NOPE].astype(q_ref.dtype)
        t = yh[:, QK_NOPE:] * cs
        q_ref[h, :, QK_NOPE:] = jnp.where(lane < QK_ROPE, t + pltpu.roll(t, QK_ROPE, 1), 0.0).astype(q_ref.dtype)


def _head_index(i):
    return (i // TILES_PER_BATCH, 0, i % TILES_PER_BATCH, 0)


def _q_up(p, g_q, w_q_up, cs, layer):
    return pl.pallas_call(
        _q_up_kernel,
        grid=(N_TILES_ALL,),
        in_specs=[pl.BlockSpec((ROW_TILE, Q_LORA), lambda i: (i, COL_Q // Q_LORA)),
                  _vec_spec(layer, Q_LORA),
                  pl.BlockSpec((None, Q_LORA, N_HEADS * (QK_NOPE + QK_ROPE)), lambda i: (layer, 0, 0)),
                  pl.BlockSpec((ROW_TILE, 128), lambda i: (i, 0))],
        out_specs=pl.BlockSpec((None, N_HEADS, ROW_TILE, K_WIDTH), _head_index),
        out_shape=jax.ShapeDtypeStruct((BATCH, N_HEADS, T_ROWS, K_WIDTH), BF16),
        scratch_shapes=[pltpu.VMEM((Q_LORA, N_HEADS * K_WIDTH), BF16)],
        compiler_params=_cparams(("arbitrary",), VMEM_LIMIT),
        name="q_up",
    )(p, g_q, w_q_up, cs)


def _kv_up_kernel(p_ref, g_ref, w_ref, h_ref, wkr_ref, cs_ref, k_ref, v_ref):
    n = _rmsnorm(p_ref[...].astype(F32), g_ref[...]).astype(BF16)
    y = jnp.dot(n, w_ref[...], preferred_element_type=F32)
    kr = lax.dot_general(h_ref[...], wkr_ref[...].astype(BF16), _CONTRACT_LAST, preferred_element_type=F32)
    kr = _rope_rotate(kr, cs_ref[...]).astype(k_ref.dtype)
    for h in range(N_HEADS):
        base = h * (QK_NOPE + V_DIM)
        k_ref[h, :, :QK_NOPE] = y[:, base:base + QK_NOPE].astype(k_ref.dtype)
        k_ref[h, :, QK_NOPE:] = kr
        v_ref[h] = y[:, base + QK_NOPE:base + QK_NOPE + V_DIM].astype(v_ref.dtype)


def _kv_up(p, g_kv, w_kv, h, w_in_t, cs, layer):
    return pl.pallas_call(
        _kv_up_kernel,
        grid=(N_TILES_ALL,),
        in_specs=[pl.BlockSpec((ROW_TILE, KV_LORA), lambda i: (i, COL_KV // KV_LORA)),
                  _vec_spec(layer, KV_LORA),
                  pl.BlockSpec((None, KV_LORA, N_HEADS * (QK_NOPE + V_DIM)), lambda i: (layer, 0, 0)),
                  pl.BlockSpec((ROW_TILE, D_MODEL), lambda i: (i, 0)),
                  pl.BlockSpec((None, 128, D_MODEL), lambda i: (layer, OFF_KR // 128, 0)),
                  pl.BlockSpec((ROW_TILE, 128), lambda i: (i, 0))],
        out_specs=[pl.BlockSpec((None, N_HEADS, ROW_TILE, K_WIDTH), _head_index),
                   pl.BlockSpec((None, N_HEADS, ROW_TILE, V_DIM), _head_index)],
        out_shape=[jax.ShapeDtypeStruct((BATCH, N_HEADS, T_ROWS, K_WIDTH), BF16),
                   jax.ShapeDtypeStruct((BATCH, N_HEADS, T_ROWS, V_DIM), BF16)],
        compiler_params=_cparams(("arbitrary",), VMEM_LIMIT),
        name="kv_up",
    )(p, g_kv, w_kv, h, w_in_t, cs)


def _attend(q, k, v):
    s = lax.dot_general(q, k, (((1,), (1,)), ((), ())), preferred_element_type=F32)
    m = jnp.max(s, axis=-1, keepdims=True)
    e = jnp.exp2(s - m)
    denom = jnp.sum(e, axis=-1, keepdims=True)
    o = jnp.dot(e.astype(BF16), v, preferred_element_type=F32)
    return o / denom


def _attention_kernel(with_ctx, q_ref, k_ref, v_ref, o_ref):
    def run(n_keys):
        for h in range(HEADS_PER_STEP):
            o = _attend(q_ref[h], k_ref[h, :n_keys, :], v_ref[h, :n_keys, :])
            o_ref[:, h * V_DIM:(h + 1) * V_DIM] = o.astype(o_ref.dtype)

    if not with_ctx:
        run(T_ROWS)
        return

    pl.when(pl.program_id(2) == 0)(functools.partial(run, CTX_LEN))
    pl.when(pl.program_id(2) > 0)(functools.partial(run, T_ROWS))


def _attention(q, k, v, with_ctx):
    q0 = 0 if with_ctx else 1
    n_q = TILES_PER_BATCH - q0
    hps = HEADS_PER_STEP
    return pl.pallas_call(
        functools.partial(_attention_kernel, with_ctx),
        grid=(BATCH, N_HEADS // hps, n_q),
        in_specs=[pl.BlockSpec((None, hps, ROW_TILE, K_WIDTH), lambda b, h, i: (b, h, i + q0, 0)),
                  pl.BlockSpec((None, hps, T_ROWS, K_WIDTH), lambda b, h, i: (b, h, 0, 0),
                               pipeline_mode=pl.Buffered(1)),
                  pl.BlockSpec((None, hps, T_ROWS, V_DIM), lambda b, h, i: (b, h, 0, 0),
                               pipeline_mode=pl.Buffered(1))],
        out_specs=pl.BlockSpec((None, ROW_TILE, hps * V_DIM), lambda b, h, i: (b, i, h)),
        out_shape=jax.ShapeDtypeStruct((BATCH, n_q * ROW_TILE, D_MODEL), BF16),
        compiler_params=_cparams(("arbitrary", "arbitrary", "arbitrary"), VMEM_LIMIT),
        name="attention",
    )(q, k, v)


def _pool_kernel(tile_fn, prev_ref, cur_ref, next_ref, w_ref, ps_ref, o_ref, buf_ref):
    t = tile_fn(pl.program_id(0)) % TILES_PER_BATCH
    is_ctx = t == 0
    seg_len = jnp.where(is_ctx, CTX_LEN, SEQ)
    pos0 = jnp.where(is_ctx, 0, (t - 1) * ROW_TILE)
    has_prev = pos0 > 0
    has_next = pos0 + ROW_TILE < seg_len
    buf_ref[0:HALO, :] = jnp.where(has_prev, prev_ref[...].astype(F32), 0.0)
    buf_ref[HALO:HALO + ROW_TILE, :] = cur_ref[...].astype(F32)
    buf_ref[HALO + ROW_TILE:, :] = jnp.where(has_next, next_ref[...].astype(F32), 0.0)

    pos = pos0 + lax.broadcasted_iota(jnp.int32, (ROW_TILE, 1), 0)
    for g, win in enumerate(POOL_WINDOWS):
        cols = slice(g * POOL_GROUP, (g + 1) * POOL_GROUP)
        half = win // 2
        acc = buf_ref[HALO - half:HALO - half + ROW_TILE, cols]
        for j in range(1, win):
            acc = acc + buf_ref[HALO - half + j:HALO - half + j + ROW_TILE, cols]
        lo = jnp.maximum(pos - half, 0)
        hi = jnp.minimum(pos - half + win, seg_len)
        mean = acc / (hi - lo).astype(F32)
        pooled = (mean - buf_ref[HALO:HALO + ROW_TILE, cols]).astype(BF16)
        out = jnp.dot(pooled, w_ref[g], preferred_element_type=F32)
        ocols = slice(g * POOL_OUT_GROUP, (g + 1) * POOL_OUT_GROUP)
        o_ref[:, ocols] = (out * ps_ref[:, ocols]).astype(o_ref.dtype)


def _pool(p, w_pool, pool_scale, layer, tile_fn, n_tiles):
    per16 = ROW_TILE // HALO
    last16 = ROWS // HALO - 1
    col = COL_POOL // POOL_WIDTH
    return pl.pallas_call(
        functools.partial(_pool_kernel, tile_fn),
        grid=(n_tiles,),
        in_specs=[pl.BlockSpec((HALO, POOL_WIDTH), lambda i: (jnp.maximum(tile_fn(i) * per16 - 1, 0), col)),
                  pl.BlockSpec((ROW_TILE, POOL_WIDTH), lambda i: (tile_fn(i), col)),
                  pl.BlockSpec((HALO, POOL_WIDTH), lambda i: (jnp.minimum((tile_fn(i) + 1) * per16, last16), col)),
                  pl.BlockSpec((None, N_POOL_GROUPS, POOL_GROUP, POOL_OUT_GROUP), lambda i: (layer, 0, 0, 0)),
                  _vec_spec(layer)],
        out_specs=pl.BlockSpec((ROW_TILE, D_MODEL), lambda i: (i, 0)),
        out_shape=jax.ShapeDtypeStruct((n_tiles * ROW_TILE, D_MODEL), BF16),
        scratch_shapes=[pltpu.VMEM((ROW_TILE + 2 * HALO, POOL_WIDTH), F32)],
        compiler_params=_cparams(("arbitrary",)),
        name="pool",
    )(p, p, p, w_pool, pool_scale)


def _split_bf16(a):
    hi = a.astype(BF16)
    return hi, (a - hi.astype(F32)).astype(BF16)


def _merge_kernel(first, with_router, *refs):
    refs = list(refs)
    ga_ref, gb_ref, attn_ref, pool_ref, ba_ref, bb_ref, w_ref = refs[:7]
    del refs[:7]
    if first:
        x = _stream_tile(refs[0], refs[1], pl.program_id(0))
        del refs[:2]
    else:
        x = refs.pop(0)[...]
    gpost_ref, g1_ref, gpre_ref, sh_ref, sc_ref = refs[:5]
    del refs[:5]
    if with_router:
        wr_ref, br_ref, xo_ref, h_ref, lg_ref = refs
    else:
        xo_ref, h_ref = refs

    ga = jax.nn.sigmoid(ga_ref[...].astype(F32) + ba_ref[...])
    gb = jax.nn.sigmoid(gb_ref[...].astype(F32) + bb_ref[...])
    mixed = ga * attn_ref[...].astype(F32) + gb * pool_ref[...].astype(F32)
    y = jnp.dot(mixed.astype(BF16), w_ref[...], preferred_element_type=F32)
    x = x + g1_ref[...] * _rmsnorm(y, gpost_ref[...])
    xo_ref[...] = x
    h = _rmsnorm(x, gpre_ref[...]) * (1.0 + sc_ref[...]) + sh_ref[...]
    h_ref[...] = h.astype(BF16).astype(h_ref.dtype)
    if with_router:
        h_hi, h_lo = _split_bf16(h)
        w_hi, w_lo = _split_bf16(wr_ref[...])
        dot = functools.partial(jnp.dot, preferred_element_type=F32)
        lg_ref[...] = dot(h_hi, w_hi) + (dot(h_lo, w_hi) + dot(h_hi, w_lo)) + br_ref[...]


def _merge(p, attn, pool, b_gate2, w_out, x_args, g_post, g_pre, mods3, layer, tile_fn, n_tiles,
           router=None):
    first = len(x_args) == 2
    gate_a = COL_GATE // D_MODEL
    act = pl.BlockSpec((ROW_TILE, D_MODEL), lambda i: (i, 0))
    in_specs = [pl.BlockSpec((ROW_TILE, D_MODEL), lambda i: (tile_fn(i), gate_a)),
                pl.BlockSpec((ROW_TILE, D_MODEL), lambda i: (tile_fn(i), gate_a + 1)),
                act, act,
                pl.BlockSpec((None, 1, D_MODEL), lambda i: (layer * 2, 0, 0)),
                pl.BlockSpec((None, 1, D_MODEL), lambda i: (layer * 2 + 1, 0, 0)),
                pl.BlockSpec((None, D_MODEL, D_MODEL), lambda i: (layer, 0, 0))]
    if first:
        in_specs += list(_input_stream_specs())
    else:
        in_specs.append(pl.BlockSpec((ROW_TILE, D_MODEL), lambda i: (tile_fn(i), 0)))
    in_specs += [_vec_spec(layer), _mod_spec(layer, 2, tile_fn), _vec_spec(layer),
                 _mod_spec(layer, 3, tile_fn), _mod_spec(layer, 4, tile_fn)]
    args = [p, p, attn, pool, b_gate2, b_gate2, w_out, *x_args, g_post, mods3, g_pre, mods3, mods3]
    rows = n_tiles * ROW_TILE
    out_specs = [act, act]
    h_dtype = BF16 if router is None else F32
    out_shape = [jax.ShapeDtypeStruct((rows, D_MODEL), F32), jax.ShapeDtypeStruct((rows, D_MODEL), h_dtype)]
    if router is not None:
        in_specs += [pl.BlockSpec((D_MODEL, 128), lambda i: (0, 0)), pl.BlockSpec((1, 128), lambda i: (0, 0))]
        args += list(router)
        out_specs.append(pl.BlockSpec((ROW_TILE, 128), lambda i: (i, 0)))
        out_shape.append(jax.ShapeDtypeStruct((rows, 128), F32))
    return pl.pallas_call(
        functools.partial(_merge_kernel, first, router is not None),
        grid=(n_tiles,),
        in_specs=in_specs,
        out_specs=out_specs,
        out_shape=out_shape,
        compiler_params=_cparams(("arbitrary",), VMEM_LIMIT),
        name="merge_out_proj",
    )(*args)


DMA_UNROLL = 8


def _dispatch_kernel(slot_ref, h_ref, xs_in_ref, xs_ref, sem):
    del xs_in_ref

    def start(r, carry):
        for k in range(TOP_K):
            dst = slot_ref[0, r * TOP_K + k]
            pltpu.make_async_copy(h_ref.at[pl.ds(r, 1)], xs_ref.at[pl.ds(dst, 1)], sem).start()
        return carry

    lax.fori_loop(0, ROW_TILE, start, 0, unroll=DMA_UNROLL)
    for _ in range(TOP_K):
        pltpu.make_async_copy(h_ref, xs_ref.at[pl.ds(0, ROW_TILE)], sem).wait()


def _dispatch(slot3, h2):
    xs_init = jnp.zeros((MOE_SEGS * SEG_ROWS, D_MODEL), h2.dtype)
    return pl.pallas_call(
        _dispatch_kernel,
        grid=(N_TILES_LATENT,),
        in_specs=[pl.BlockSpec((None, 1, ROW_TILE * TOP_K), lambda i: (i, 0, 0), memory_space=pltpu.SMEM),
                  pl.BlockSpec((ROW_TILE, D_MODEL), lambda i: (i, 0)),
                  pl.BlockSpec(memory_space=pl.ANY)],
        out_specs=pl.BlockSpec(memory_space=pl.ANY),
        out_shape=jax.ShapeDtypeStruct(xs_init.shape, xs_init.dtype),
        input_output_aliases={2: 0},
        scratch_shapes=[pltpu.SemaphoreType.DMA(())],
        compiler_params=_cparams(("arbitrary",)),
        name="moe_dispatch",
    )(slot3, h2, xs_init)


def _ffn_kernel(se_ref, st_ref, x_ref, wg_ref, wu_ref, wd_ref, o_ref):
    del se_ref
    n_tiles = st_ref[pl.program_id(0)]

    @pl.when(pl.program_id(1) == 0)
    def _():
        o_ref[...] = jnp.zeros_like(o_ref)

    def rows_step(row0, n_rows):
        rows = pl.ds(row0 if isinstance(row0, int) else pl.multiple_of(row0, 16), n_rows)
        x = x_ref[rows, :].astype(BF16)
        g = jnp.dot(x, wg_ref[...].astype(BF16), preferred_element_type=F32)
        u = jnp.dot(x, wu_ref[...].astype(BF16), preferred_element_type=F32)
        a = (g * jax.nn.sigmoid(g)) * u
        o_ref[rows, :] += jnp.dot(a.astype(BF16), wd_ref[...].astype(BF16), preferred_element_type=F32)

    @pl.when(n_tiles == FFN_TILES_PER_SEG)
    def _():
        rows_step(0, SEG_ROWS)

    @pl.when(n_tiles < FFN_TILES_PER_SEG)
    def _():
        def pair(i, carry):
            rows_step(i * (2 * FFN_TILE), 2 * FFN_TILE)
            return carry

        lax.fori_loop(0, lax.shift_right_logical(n_tiles, 1), pair, 0)

        @pl.when((n_tiles & 1) == 1)
        def _():
            rows_step((n_tiles - 1) * FFN_TILE, FFN_TILE)


def _ffn(seg_expert, seg_tiles, xs, wg, wu, wd):
    n_seg = xs.shape[0] // SEG_ROWS
    tf = FFN_CHUNK * (jnp.dtype(F32).itemsize // jnp.dtype(xs.dtype).itemsize)
    n_chunks = D_FF // tf

    def chunk(s, f, st):
        return jnp.where(st[s] > 0, f, n_chunks - 1)

    grid_spec = pltpu.PrefetchScalarGridSpec(
        num_scalar_prefetch=2,
        grid=(n_seg, n_chunks),
        in_specs=[pl.BlockSpec((SEG_ROWS, D_MODEL), lambda s, f, se, st: (s, 0)),
                  pl.BlockSpec((None, D_MODEL, tf), lambda s, f, se, st: (se[s], 0, chunk(s, f, st))),
                  pl.BlockSpec((None, D_MODEL, tf), lambda s, f, se, st: (se[s], 0, chunk(s, f, st))),
                  pl.BlockSpec((None, tf, D_MODEL), lambda s, f, se, st: (se[s], chunk(s, f, st), 0))],
        out_specs=pl.BlockSpec((SEG_ROWS, D_MODEL), lambda s, f, se, st: (s, 0)),
    )
    return pl.pallas_call(
        _ffn_kernel,
        grid_spec=grid_spec,
        out_shape=jax.ShapeDtypeStruct((n_seg * SEG_ROWS, D_MODEL), F32),
        compiler_params=_cparams(("arbitrary", "arbitrary"), VMEM_LIMIT),
        name="swiglu_ffn",
    )(seg_expert, seg_tiles, xs, wg, wu, wd)


def _post_kernel(y_ref, x_ref, gpost_ref, g2_ref, gpre_ref, sh_ref, sc_ref, xo_ref, h_ref):
    x = x_ref[...] + g2_ref[...] * _rmsnorm(y_ref[...], gpost_ref[...])
    xo_ref[...] = x
    h = _rmsnorm(x, gpre_ref[...]) * (1.0 + sc_ref[...]) + sh_ref[...]
    h_ref[...] = h.astype(h_ref.dtype)


def _post(y, x, g_post, g_pre_next, mods3, layer):
    row = pl.BlockSpec((ROW_TILE, D_MODEL), lambda i: (i, 0))
    return pl.pallas_call(
        _post_kernel,
        grid=(N_TILES_ALL,),
        in_specs=[row, row, _vec_spec(layer), _mod_spec(layer, 5, _tile_all),
                  _vec_spec(layer + 1), _mod_spec(layer + 1, 0, _tile_all), _mod_spec(layer + 1, 1, _tile_all)],
        out_specs=[row, row],
        out_shape=[jax.ShapeDtypeStruct((ROWS, D_MODEL), F32), jax.ShapeDtypeStruct((ROWS, D_MODEL), BF16)],
        compiler_params=_cparams(("arbitrary",)),
        name="post_ffn",
    )(y, x, g_post, mods3, g_pre_next, mods3, mods3)


def _combine_kernel(slot_ref, next_slot_ref, w_ref, x_ref, gpost_ref, g2_ref, ys_ref, o_ref, buf_ref, sem):
    i = pl.program_id(0)
    cur = i % 2

    def gather(slots, b):
        def start(r, carry):
            for k in range(TOP_K):
                src = slots[0, r * TOP_K + k]
                pltpu.make_async_copy(ys_ref.at[pl.ds(src, 1)], buf_ref.at[b, pl.ds(k * ROW_TILE + r, 1)],
                                      sem.at[b]).start()
            return carry
        lax.fori_loop(0, ROW_TILE, start, 0, unroll=DMA_UNROLL)

    @pl.when(i == 0)
    def _():
        gather(slot_ref, 0)

    @pl.when(i + 1 < pl.num_programs(0))
    def _():
        gather(next_slot_ref, 1 - cur)

    pltpu.make_async_copy(ys_ref.at[pl.ds(0, TOP_K * ROW_TILE)], buf_ref.at[cur], sem.at[cur]).wait()
    y = w_ref[:, 0:1] * buf_ref[cur, :ROW_TILE, :] + w_ref[:, 1:2] * buf_ref[cur, ROW_TILE:, :]
    o_ref[...] = x_ref[...] + g2_ref[...] * _rmsnorm(y, gpost_ref[...])


def _combine(slot3, weight, x, g_post, mods3, layer, ys):
    row = pl.BlockSpec((ROW_TILE, D_MODEL), lambda i: (i, 0))
    slots = lambda index: pl.BlockSpec((None, 1, ROW_TILE * TOP_K), index, memory_space=pltpu.SMEM)
    return pl.pallas_call(
        _combine_kernel,
        grid=(N_TILES_LATENT,),
        in_specs=[slots(lambda i: (i, 0, 0)),
                  slots(lambda i: (jnp.minimum(i + 1, N_TILES_LATENT - 1), 0, 0)),
                  pl.BlockSpec((ROW_TILE, TOP_K), lambda i: (i, 0)),
                  row, _vec_spec(layer), _mod_spec(layer, 5, _tile_latent),
                  pl.BlockSpec(memory_space=pl.ANY)],
        out_specs=row,
        out_shape=jax.ShapeDtypeStruct((BATCH * SEQ, D_MODEL), F32),
        scratch_shapes=[pltpu.VMEM((2, TOP_K * ROW_TILE, D_MODEL), F32), pltpu.SemaphoreType.DMA((2,))],
        compiler_params=_cparams(("arbitrary",), VMEM_LIMIT),
        name="moe_combine",
    )(slot3, slot3, weight, x, g_post, mods3, ys)


def _rope_table():
    pos = jnp.arange(SEQ, dtype=jnp.int32)
    row_ids = (pos // GRID_W).astype(F32)
    col_ids = (pos % GRID_W).astype(F32)
    n_freq = QK_ROPE // 4
    inv = ROPE_THETA ** (-jnp.arange(n_freq, dtype=F32) / n_freq)
    ang = jnp.concatenate([row_ids[:, None] * inv, col_ids[:, None] * inv], axis=-1)
    cos, sin = jnp.cos(ang), jnp.sin(ang)
    latent = jnp.concatenate([cos, cos, -sin, sin], axis=-1)
    ctx = jnp.concatenate([jnp.ones((CTX_LEN, QK_ROPE), F32), jnp.zeros((CTX_LEN, QK_ROPE), F32)], axis=-1)
    one = jnp.concatenate([ctx, latent], axis=0)
    return jnp.tile(one, (BATCH, 1))


def _routing(logits):
    probs = jax.nn.softmax(logits, axis=-1)
    top_p, top_i = lax.top_k(probs, TOP_K)
    top_p = top_p / jnp.sum(top_p, axis=-1, keepdims=True)
    onehot = jax.nn.one_hot(top_i.reshape(N_ASSIGN), N_EXPERTS, dtype=jnp.int32)
    csum = jnp.cumsum(onehot, axis=0)
    rank = jnp.sum((csum - onehot) * onehot, axis=1)
    counts = csum[-1]
    n_segs = (counts + SEG_ROWS - 1) // SEG_ROWS
    seg_start = jnp.cumsum(n_segs) - n_segs
    slot = jnp.sum(onehot * seg_start[None, :], axis=1) * SEG_ROWS + rank
    seg_ids = jnp.arange(MOE_SEGS, dtype=jnp.int32)
    used = jnp.sum(n_segs)
    seg_expert = jnp.sum((seg_ids[:, None] >= (seg_start + n_segs)[None, :]).astype(jnp.int32), axis=1)
    last_used_expert = jnp.max(jnp.where(counts > 0, jnp.arange(N_EXPERTS), 0))
    seg_expert = jnp.where(seg_ids < used, jnp.minimum(seg_expert, N_EXPERTS - 1), last_used_expert)
    rows_left = counts[seg_expert] - (seg_ids - seg_start[seg_expert]) * SEG_ROWS
    seg_tiles = jnp.clip((rows_left + FFN_TILE - 1) // FFN_TILE, 0, FFN_TILES_PER_SEG)
    seg_tiles = jnp.where(seg_ids < used, seg_tiles, 0)
    return (seg_expert.astype(jnp.int32), seg_tiles.astype(jnp.int32),
            slot.astype(jnp.int32).reshape(BATCH * SEQ, TOP_K), top_p)


def kernel(x, c, ctx, c_ctx, w_mod, b_mod, g_mix_pre, g_mix_post, g_ffn_pre, g_ffn_post, w_in, b_gate, g_q_lat, g_kv_lat, w_q_up, w_kv_up, w_pool, pool_scale, w_out, w_ff_gate, w_ff_up, w_ff_down, w_router, b_router, w_exp_gate, w_exp_up, w_exp_down):
    x2 = x.reshape(BATCH * SEQ, D_MODEL)
    ctx2 = ctx.reshape(BATCH * CTX_LEN, D_MODEL)

    c8 = jnp.concatenate([c, c_ctx[None, :], jnp.zeros((8 - BATCH - 1, D_MODEL), F32)], axis=0)
    mods = _modulation(c8, w_mod, b_mod)
    mods3 = mods[:, :BATCH + 1].reshape(DEPTH * 3 * 6, 1, D_MODEL)

    as_vec = lambda a: a.reshape(DEPTH, 1, a.shape[-1])
    g_mix_pre, g_mix_post, g_ffn_pre, g_ffn_post = map(as_vec, (g_mix_pre, g_mix_post, g_ffn_pre, g_ffn_post))
    g_q_lat, g_kv_lat, pool_scale = map(as_vec, (g_q_lat, g_kv_lat, pool_scale))
    b_gate2 = b_gate.reshape(DEPTH * 2, 1, D_MODEL)
    cs = _rope_table()

    w_in_t = jnp.swapaxes(w_in, 1, 2)
    w_kv, w_pool, w_out = (w.astype(BF16) for w in (w_kv_up, w_pool, w_out))

    h = _norm_mod(x2, ctx2, g_mix_pre, mods3, 0)
    xr = None
    out = None
    for l in range(DEPTH):
        last = l == DEPTH - 1
        p = _in_proj(h, w_in_t, l)
        q = _q_up(p, g_q_lat, w_q_up, cs, l)
        k, v = _kv_up(p, g_kv_lat, w_kv, h, w_in_t, cs, l)

        attn = _attention(q, k, v, with_ctx=not last).reshape(-1, D_MODEL)
        tile_fn, n_tiles = (_tile_latent, N_TILES_LATENT) if last else (_tile_all, N_TILES_ALL)
        pool = _pool(p, w_pool, pool_scale, l, tile_fn, n_tiles)
        x_args = (x2, ctx2) if l == 0 else (xr,)
        merge = functools.partial(_merge, p, attn, pool, b_gate2, w_out, x_args, g_mix_post,
                                  g_ffn_pre, mods3, l, tile_fn, n_tiles)

        if not last:
            xr, h2 = merge()
            n_seg = ROWS // SEG_ROWS
            seg_expert = jnp.zeros((n_seg,), jnp.int32)
            seg_tiles = jnp.full((n_seg,), FFN_TILES_PER_SEG, jnp.int32)
            y = _ffn(seg_expert, seg_tiles, h2, w_ff_gate, w_ff_up, w_ff_down)
            xr, h = _post(y, xr, g_ffn_post, g_mix_pre, mods3, l)
        else:
            w_r = jnp.pad(w_router[0], ((0, 0), (0, 128 - N_EXPERTS)))
            b_r = jnp.pad(b_router[0], (0, 128 - N_EXPERTS)).reshape(1, 128)
            xl, h2, logits = merge(router=(w_r, b_r))
            seg_expert, seg_tiles, slot, weight = _routing(logits[:, :N_EXPERTS])
            slot3 = slot.reshape(N_TILES_LATENT, 1, ROW_TILE * TOP_K)
            xs = _dispatch(slot3, h2)
            ys = _ffn(seg_expert, seg_tiles, xs, w_exp_gate.reshape(N_EXPERTS, D_MODEL, D_FF),
                      w_exp_up.reshape(N_EXPERTS, D_MODEL, D_FF), w_exp_down.reshape(N_EXPERTS, D_FF, D_MODEL))
            out = _combine(slot3, weight, xl, g_ffn_post, mods3, l, ys)
    return out.reshape(BATCH, SEQ, D_MODEL)
```

```python
import functools
import math

import jax
import jax.numpy as jnp
from jax import lax
from jax.experimental import pallas as pl
from jax.experimental.pallas import tpu as pltpu

F32 = jnp.float32
BF16 = jnp.bfloat16

D_MODEL = 2048
BATCH = 2
SEQ = 4096
DEPTH = 2
GRID_W = 64
CTX_LEN = 256
EPS = 1e-6

QK_NOPE = 128
QK_ROPE = 64
V_DIM = 128
N_HEADS = 16
Q_LORA = 512
KV_LORA = 512
ROPE_THETA = 10000.0
ATTN_SCALE = (QK_NOPE + QK_ROPE) ** -0.5
Q_PRESCALE = ATTN_SCALE * math.log2(math.e)

POOL_WINDOWS = (2, 4, 8, 16)
N_POOL_GROUPS = 4
POOL_WIDTH = 1024
POOL_GROUP = 256
POOL_OUT_GROUP = 512

OFF_KR = 1024
OFF_POOL = 1088

D_FF = 7168
N_EXPERTS = 8
TOP_K = 2

T_ROWS = CTX_LEN + SEQ
ROWS = BATCH * T_ROWS
ROW_TILE = 256
TILES_PER_BATCH = T_ROWS // ROW_TILE
LATENT_TILES = SEQ // ROW_TILE
N_TILES_ALL = BATCH * TILES_PER_BATCH
N_TILES_LATENT = BATCH * LATENT_TILES
HALO = 16
K_WIDTH = 256
HEADS_PER_STEP = 8

COL_Q, COL_KV, COL_POOL, COL_GATE = 0, 512, 1024, 2048

FFN_TILE = 272
FFN_TILES_PER_SEG = 4
SEG_ROWS = FFN_TILE * FFN_TILES_PER_SEG
FFN_CHUNK = 256
N_ASSIGN = BATCH * SEQ * TOP_K
MOE_SEGS = N_ASSIGN // SEG_ROWS + N_EXPERTS

VMEM_LIMIT = 56 * 1024 * 1024


def _cparams(sem, vmem=None):
    return pltpu.CompilerParams(dimension_semantics=sem, vmem_limit_bytes=vmem)


def _tile_all(i):
    return i


def _tile_latent(i):
    return (i // LATENT_TILES) * TILES_PER_BATCH + 1 + i % LATENT_TILES


def _group_of_tile(t):
    return jnp.where(t % TILES_PER_BATCH == 0, BATCH, t // TILES_PER_BATCH)


def _mod_spec(layer, chunk, tile_fn):
    def index(i):
        return ((layer * 3 + _group_of_tile(tile_fn(i))) * 6 + chunk, 0, 0)
    return pl.BlockSpec((None, 1, D_MODEL), index)


def _vec_spec(layer, width=D_MODEL):
    return pl.BlockSpec((None, 1, width), lambda i: (layer, 0, 0))


def _input_stream_specs():
    def x_index(t):
        return ((t // TILES_PER_BATCH) * LATENT_TILES + jnp.maximum(t % TILES_PER_BATCH - 1, 0), 0)

    def ctx_index(t):
        return (t // TILES_PER_BATCH, 0)

    return (pl.BlockSpec((ROW_TILE, D_MODEL), x_index), pl.BlockSpec((CTX_LEN, D_MODEL), ctx_index))


def _stream_tile(x_ref, ctx_ref, tile):
    return jnp.where(tile % TILES_PER_BATCH == 0, ctx_ref[...], x_ref[...])


def _rmsnorm(x, g):
    return x * lax.rsqrt(jnp.mean(x * x, axis=-1, keepdims=True) + EPS) * g


def _mod_kernel(c_ref, w_ref, b_ref, o_ref):
    c = c_ref[...]
    a = c * jax.nn.sigmoid(c)
    o_ref[...] = jnp.dot(a, w_ref[...], preferred_element_type=F32) + b_ref[...]


def _modulation(c8, w_mod, b_mod):
    n = w_mod.shape[2]
    tn = 1024
    return pl.pallas_call(
        _mod_kernel,
        grid=(DEPTH, n // tn),
        in_specs=[pl.BlockSpec((8, D_MODEL), lambda l, j: (0, 0)),
                  pl.BlockSpec((None, D_MODEL, tn), lambda l, j: (l, 0, j)),
                  pl.BlockSpec((None, 1, tn), lambda l, j: (l, 0, j))],
        out_specs=pl.BlockSpec((None, 8, tn), lambda l, j: (l, 0, j)),
        out_shape=jax.ShapeDtypeStruct((DEPTH, 8, n), F32),
        compiler_params=_cparams(("arbitrary", "arbitrary"), VMEM_LIMIT),
        name="modulation",
    )(c8, w_mod, b_mod.reshape(DEPTH, 1, n))


def _norm_mod_kernel(x_ref, ctx_ref, g_ref, sh_ref, sc_ref, o_ref):
    y = _rmsnorm(_stream_tile(x_ref, ctx_ref, pl.program_id(0)), g_ref[...])
    o_ref[...] = (y * (1.0 + sc_ref[...]) + sh_ref[...]).astype(o_ref.dtype)


def _norm_mod(x2, ctx2, g, mods3, layer):
    x_spec, ctx_spec = _input_stream_specs()
    return pl.pallas_call(
        _norm_mod_kernel,
        grid=(N_TILES_ALL,),
        in_specs=[x_spec, ctx_spec, _vec_spec(layer),
                  _mod_spec(layer, 0, _tile_all), _mod_spec(layer, 1, _tile_all)],
        out_specs=pl.BlockSpec((ROW_TILE, D_MODEL), lambda i: (i, 0)),
        out_shape=jax.ShapeDtypeStruct((ROWS, D_MODEL), BF16),
        compiler_params=_cparams(("arbitrary",)),
        name="norm_mod",
    )(x2, ctx2, g, mods3, mods3)


_CONTRACT_LAST = (((1,), (1,)), ((), ()))
IN_TILE = 1024
IN_ROWS = 1088
MAIN_WIDTH = Q_LORA + KV_LORA + POOL_WIDTH + 2 * D_MODEL


def _in_proj_kernel(layer, a_ref, wt_ref, o_ref, wbuf_ref, wcast_ref, sem):
    j = pl.program_id(0)

    def fetch(tile):
        row0 = tile * IN_TILE + jnp.where(tile * IN_TILE >= OFF_KR, QK_ROPE, 0)
        rows = pl.ds(pl.multiple_of(row0, QK_ROPE), IN_TILE)
        slot = tile % 2
        return pltpu.make_async_copy(wt_ref.at[layer, rows], wbuf_ref.at[slot], sem.at[slot])

    @pl.when(pl.program_id(1) == 0)
    def _():
        @pl.when(j == 0)
        def _():
            fetch(j).start()

        @pl.when(j + 1 < pl.num_programs(0))
        def _():
            fetch(j + 1).start()

        fetch(j).wait()
        wcast_ref[...] = wbuf_ref[j % 2].astype(wcast_ref.dtype)

    y = lax.dot_general(a_ref[...], wcast_ref[...], _CONTRACT_LAST, preferred_element_type=F32)
    o_ref[...] = y.astype(o_ref.dtype)


def _in_proj(a, w_in_t, layer):
    m, k = a.shape
    assert OFF_KR % IN_TILE == 0 and MAIN_WIDTH % IN_TILE == 0 and m % IN_ROWS == 0
    return pl.pallas_call(
        functools.partial(_in_proj_kernel, layer),
        grid=(MAIN_WIDTH // IN_TILE, m // IN_ROWS),
        in_specs=[pl.BlockSpec((IN_ROWS, k), lambda j, i: (i, 0)),
                  pl.BlockSpec(memory_space=pl.ANY)],
        out_specs=pl.BlockSpec((IN_ROWS, IN_TILE), lambda j, i: (i, j)),
        out_shape=jax.ShapeDtypeStruct((m, MAIN_WIDTH), BF16),
        scratch_shapes=[pltpu.VMEM((2, IN_TILE, k), F32), pltpu.VMEM((IN_TILE, k), BF16),
                        pltpu.SemaphoreType.DMA((2,))],
        compiler_params=_cparams(("arbitrary", "arbitrary"), VMEM_LIMIT),
        name="in_proj",
    )(a, w_in_t)


def _rope_rotate(t, cs):
    lane = lax.broadcasted_iota(jnp.int32, t.shape, 1)
    s_tab = pltpu.roll(cs, QK_ROPE, 1)
    return jnp.where(lane < QK_ROPE, t * cs + _swap_rope_halves(t) * s_tab, 0.0)


def _swap_rope_halves(t):
    half = QK_ROPE // 2
    lane = lax.broadcasted_iota(jnp.int32, t.shape, 1)
    return jnp.where(lane < half, pltpu.roll(t, 128 - half, 1), pltpu.roll(t, half, 1))


def _q_up_kernel(p_ref, g_ref, w_ref, cs_ref, q_ref, w2_ref):
    head = QK_NOPE + QK_ROPE

    @pl.when(pl.program_id(0) == 0)
    def _():
        lane = lax.broadcasted_iota(jnp.int32, (Q_LORA, 128), 1)
        w = w_ref[...]
        for h in range(N_HEADS):
            base = h * head
            w2_ref[:, h * K_WIDTH:h * K_WIDTH + QK_NOPE] = w[:, base:base + QK_NOPE].astype(w2_ref.dtype)
            tail = pltpu.roll(w[:, base + head - 128:base + head], QK_ROPE, 1)
            both = jnp.where(lane < QK_ROPE, tail, pltpu.roll(_swap_rope_halves(tail), QK_ROPE, 1))
            w2_ref[:, h * K_WIDTH + QK_NOPE:(h + 1) * K_WIDTH] = both.astype(w2_ref.dtype)

    n = _rmsnorm(p_ref[...].astype(F32), g_ref[...]).astype(BF16)
    y = jnp.dot(n, w2_ref[...], preferred_element_type=F32) * Q_PRESCALE
    cs = cs_ref[...]
    lane = lax.broadcasted_iota(jnp.int32, (ROW_TILE, 128), 1)
    for h in range(N_HEADS):
        yh = y[:, h * K_WIDTH:(h + 1) * K_WIDTH]
        q_ref[h, :, :QK_NOPE] = yh[:, :QK_NOPE].astype(q_ref.dtype)
        t = yh[:, QK_NOPE:] * cs
        q_ref[h, :, QK_NOPE:] = jnp.where(lane < QK_ROPE, t + pltpu.roll(t, QK_ROPE, 1), 0.0).astype(q_ref.dtype)


def _head_index(i):
    return (i // TILES_PER_BATCH, 0, i % TILES_PER_BATCH, 0)


def _q_up(p, g_q, w_q_up, cs, layer):
    return pl.pallas_call(
        _q_up_kernel,
        grid=(N_TILES_ALL,),
        in_specs=[pl.BlockSpec((ROW_TILE, Q_LORA), lambda i: (i, COL_Q // Q_LORA)),
                  _vec_spec(layer, Q_LORA),
                  pl.BlockSpec((None, Q_LORA, N_HEADS * (QK_NOPE + QK_ROPE)), lambda i: (layer, 0, 0)),
                  pl.BlockSpec((ROW_TILE, 128), lambda i: (i, 0))],
        out_specs=pl.BlockSpec((None, N_HEADS, ROW_TILE, K_WIDTH), _head_index),
        out_shape=jax.ShapeDtypeStruct((BATCH, N_HEADS, T_ROWS, K_WIDTH), BF16),
        scratch_shapes=[pltpu.VMEM((Q_LORA, N_HEADS * K_WIDTH), BF16)],
        compiler_params=_cparams(("arbitrary",), VMEM_LIMIT),
        name="q_up",
    )(p, g_q, w_q_up, cs)


def _kv_up_kernel(p_ref, g_ref, w_ref, h_ref, wkr_ref, cs_ref, k_ref, v_ref):
    n = _rmsnorm(p_ref[...].astype(F32), g_ref[...]).astype(BF16)
    y = jnp.dot(n, w_ref[...], preferred_element_type=F32)
    kr = lax.dot_general(h_ref[...], wkr_ref[...].astype(BF16), _CONTRACT_LAST, preferred_element_type=F32)
    kr = _rope_rotate(kr, cs_ref[...]).astype(k_ref.dtype)
    for h in range(N_HEADS):
        base = h * (QK_NOPE + V_DIM)
        k_ref[h, :, :QK_NOPE] = y[:, base:base + QK_NOPE].astype(k_ref.dtype)
        k_ref[h, :, QK_NOPE:] = kr
        v_ref[h] = y[:, base + QK_NOPE:base + QK_NOPE + V_DIM].astype(v_ref.dtype)


def _kv_up(p, g_kv, w_kv, h, w_in_t, cs, layer):
    return pl.pallas_call(
        _kv_up_kernel,
        grid=(N_TILES_ALL,),
        in_specs=[pl.BlockSpec((ROW_TILE, KV_LORA), lambda i: (i, COL_KV // KV_LORA)),
                  _vec_spec(layer, KV_LORA),
                  pl.BlockSpec((None, KV_LORA, N_HEADS * (QK_NOPE + V_DIM)), lambda i: (layer, 0, 0)),
                  pl.BlockSpec((ROW_TILE, D_MODEL), lambda i: (i, 0)),
                  pl.BlockSpec((None, 128, D_MODEL), lambda i: (layer, OFF_KR // 128, 0)),
                  pl.BlockSpec((ROW_TILE, 128), lambda i: (i, 0))],
        out_specs=[pl.BlockSpec((None, N_HEADS, ROW_TILE, K_WIDTH), _head_index),
                   pl.BlockSpec((None, N_HEADS, ROW_TILE, V_DIM), _head_index)],
        out_shape=[jax.ShapeDtypeStruct((BATCH, N_HEADS, T_ROWS, K_WIDTH), BF16),
                   jax.ShapeDtypeStruct((BATCH, N_HEADS, T_ROWS, V_DIM), BF16)],
        compiler_params=_cparams(("arbitrary",), VMEM_LIMIT),
        name="kv_up",
    )(p, g_kv, w_kv, h, w_in_t, cs)


def _attend(q, k, v):
    s = lax.dot_general(q, k, (((1,), (1,)), ((), ())), preferred_element_type=F32)
    m = jnp.max(s, axis=-1, keepdims=True)
    e = jnp.exp2(s - m)
    denom = jnp.sum(e, axis=-1, keepdims=True)
    o = jnp.dot(e.astype(BF16), v, preferred_element_type=F32)
    return o / denom


def _attention_kernel(with_ctx, q_ref, k_ref, v_ref, o_ref):
    def run(n_keys):
        for h in range(HEADS_PER_STEP):
            o = _attend(q_ref[h], k_ref[h, :n_keys, :], v_ref[h, :n_keys, :])
            o_ref[:, h * V_DIM:(h + 1) * V_DIM] = o.astype(o_ref.dtype)

    if not with_ctx:
        run(T_ROWS)
        return

    pl.when(pl.program_id(2) == 0)(functools.partial(run, CTX_LEN))
    pl.when(pl.program_id(2) > 0)(functools.partial(run, T_ROWS))


def _attention(q, k, v, with_ctx):
    q0 = 0 if with_ctx else 1
    n_q = TILES_PER_BATCH - q0
    hps = HEADS_PER_STEP
    return pl.pallas_call(
        functools.partial(_attention_kernel, with_ctx),
        grid=(BATCH, N_HEADS // hps, n_q),
        in_specs=[pl.BlockSpec((None, hps, ROW_TILE, K_WIDTH), lambda b, h, i: (b, h, i + q0, 0)),
                  pl.BlockSpec((None, hps, T_ROWS, K_WIDTH), lambda b, h, i: (b, h, 0, 0),
                               pipeline_mode=pl.Buffered(1)),
                  pl.BlockSpec((None, hps, T_ROWS, V_DIM), lambda b, h, i: (b, h, 0, 0),
                               pipeline_mode=pl.Buffered(1))],
        out_specs=pl.BlockSpec((None, ROW_TILE, hps * V_DIM), lambda b, h, i: (b, i, h)),
        out_shape=jax.ShapeDtypeStruct((BATCH, n_q * ROW_TILE, D_MODEL), BF16),
        compiler_params=_cparams(("arbitrary", "arbitrary", "arbitrary"), VMEM_LIMIT),
        name="attention",
    )(q, k, v)


def _pool_kernel(tile_fn, prev_ref, cur_ref, next_ref, w_ref, ps_ref, o_ref, buf_ref):
    t = tile_fn(pl.program_id(0)) % TILES_PER_BATCH
    is_ctx = t == 0
    seg_len = jnp.where(is_ctx, CTX_LEN, SEQ)
    pos0 = jnp.where(is_ctx, 0, (t - 1) * ROW_TILE)
    has_prev = pos0 > 0
    has_next = pos0 + ROW_TILE < seg_len
    buf_ref[0:HALO, :] = jnp.where(has_prev, prev_ref[...].astype(F32), 0.0)
    buf_ref[HALO:HALO + ROW_TILE, :] = cur_ref[...].astype(F32)
    buf_ref[HALO + ROW_TILE:, :] = jnp.where(has_next, next_ref[...].astype(F32), 0.0)

    pos = pos0 + lax.broadcasted_iota(jnp.int32, (ROW_TILE, 1), 0)
    for g, win in enumerate(POOL_WINDOWS):
        cols = slice(g * POOL_GROUP, (g + 1) * POOL_GROUP)
        half = win // 2
        acc = buf_ref[HALO - half:HALO - half + ROW_TILE, cols]
        for j in range(1, win):
            acc = acc + buf_ref[HALO - half + j:HALO - half + j + ROW_TILE, cols]
        lo = jnp.maximum(pos - half, 0)
        hi = jnp.minimum(pos - half + win, seg_len)
        mean = acc / (hi - lo).astype(F32)
        pooled = (mean - buf_ref[HALO:HALO + ROW_TILE, cols]).astype(BF16)
        out = jnp.dot(pooled, w_ref[g], preferred_element_type=F32)
        ocols = slice(g * POOL_OUT_GROUP, (g + 1) * POOL_OUT_GROUP)
        o_ref[:, ocols] = (out * ps_ref[:, ocols]).astype(o_ref.dtype)


def _pool(p, w_pool, pool_scale, layer, tile_fn, n_tiles):
    per16 = ROW_TILE // HALO
    last16 = ROWS // HALO - 1
    col = COL_POOL // POOL_WIDTH
    return pl.pallas_call(
        functools.partial(_pool_kernel, tile_fn),
        grid=(n_tiles,),
        in_specs=[pl.BlockSpec((HALO, POOL_WIDTH), lambda i: (jnp.maximum(tile_fn(i) * per16 - 1, 0), col)),
                  pl.BlockSpec((ROW_TILE, POOL_WIDTH), lambda i: (tile_fn(i), col)),
                  pl.BlockSpec((HALO, POOL_WIDTH), lambda i: (jnp.minimum((tile_fn(i) + 1) * per16, last16), col)),
                  pl.BlockSpec((None, N_POOL_GROUPS, POOL_GROUP, POOL_OUT_GROUP), lambda i: (layer, 0, 0, 0)),
                  _vec_spec(layer)],
        out_specs=pl.BlockSpec((ROW_TILE, D_MODEL), lambda i: (i, 0)),
        out_shape=jax.ShapeDtypeStruct((n_tiles * ROW_TILE, D_MODEL), BF16),
        scratch_shapes=[pltpu.VMEM((ROW_TILE + 2 * HALO, POOL_WIDTH), F32)],
        compiler_params=_cparams(("arbitrary",)),
        name="pool",
    )(p, p, p, w_pool, pool_scale)


def _split_bf16(a):
    hi = a.astype(BF16)
    return hi, (a - hi.astype(F32)).astype(BF16)


def _merge_kernel(first, with_router, *refs):
    refs = list(refs)
    ga_ref, gb_ref, attn_ref, pool_ref, ba_ref, bb_ref, w_ref = refs[:7]
    del refs[:7]
    if first:
        x = _stream_tile(refs[0], refs[1], pl.program_id(0))
        del refs[:2]
    else:
        x = refs.pop(0)[...]
    gpost_ref, g1_ref, gpre_ref, sh_ref, sc_ref = refs[:5]
    del refs[:5]
    if with_router:
        wr_ref, br_ref, xo_ref, h_ref, lg_ref = refs
    else:
        xo_ref, h_ref = refs

    ga = jax.nn.sigmoid(ga_ref[...].astype(F32) + ba_ref[...])
    gb = jax.nn.sigmoid(gb_ref[...].astype(F32) + bb_ref[...])
    mixed = ga * attn_ref[...].astype(F32) + gb * pool_ref[...].astype(F32)
    y = jnp.dot(mixed.astype(BF16), w_ref[...], preferred_element_type=F32)
    x = x + g1_ref[...] * _rmsnorm(y, gpost_ref[...])
    xo_ref[...] = x
    h = _rmsnorm(x, gpre_ref[...]) * (1.0 + sc_ref[...]) + sh_ref[...]
    h_ref[...] = h.astype(BF16).astype(h_ref.dtype)
    if with_router:
        h_hi, h_lo = _split_bf16(h)
        w_hi, w_lo = _split_bf16(wr_ref[...])
        dot = functools.partial(jnp.dot, preferred_element_type=F32)
        lg_ref[...] = dot(h_hi, w_hi) + (dot(h_lo, w_hi) + dot(h_hi, w_lo)) + br_ref[...]


def _merge(p, attn, pool, b_gate2, w_out, x_args, g_post, g_pre, mods3, layer, tile_fn, n_tiles,
           router=None):
    first = len(x_args) == 2
    gate_a = COL_GATE // D_MODEL
    act = pl.BlockSpec((ROW_TILE, D_MODEL), lambda i: (i, 0))
    in_specs = [pl.BlockSpec((ROW_TILE, D_MODEL), lambda i: (tile_fn(i), gate_a)),
                pl.BlockSpec((ROW_TILE, D_MODEL), lambda i: (tile_fn(i), gate_a + 1)),
                act, act,
                pl.BlockSpec((None, 1, D_MODEL), lambda i: (layer * 2, 0, 0)),
                pl.BlockSpec((None, 1, D_MODEL), lambda i: (layer * 2 + 1, 0, 0)),
                pl.BlockSpec((None, D_MODEL, D_MODEL), lambda i: (layer, 0, 0))]
    if first:
        in_specs += list(_input_stream_specs())
    else:
        in_specs.append(pl.BlockSpec((ROW_TILE, D_MODEL), lambda i: (tile_fn(i), 0)))
    in_specs += [_vec_spec(layer), _mod_spec(layer, 2, tile_fn), _vec_spec(layer),
                 _mod_spec(layer, 3, tile_fn), _mod_spec(layer, 4, tile_fn)]
    args = [p, p, attn, pool, b_gate2, b_gate2, w_out, *x_args, g_post, mods3, g_pre, mods3, mods3]
    rows = n_tiles * ROW_TILE
    out_specs = [act, act]
    h_dtype = BF16 if router is None else F32
    out_shape = [jax.ShapeDtypeStruct((rows, D_MODEL), F32), jax.ShapeDtypeStruct((rows, D_MODEL), h_dtype)]
    if router is not None:
        in_specs += [pl.BlockSpec((D_MODEL, 128), lambda i: (0, 0)), pl.BlockSpec((1, 128), lambda i: (0, 0))]
        args += list(router)
        out_specs.append(pl.BlockSpec((ROW_TILE, 128), lambda i: (i, 0)))
        out_shape.append(jax.ShapeDtypeStruct((rows, 128), F32))
    return pl.pallas_call(
        functools.partial(_merge_kernel, first, router is not None),
        grid=(n_tiles,),
        in_specs=in_specs,
        out_specs=out_specs,
        out_shape=out_shape,
        compiler_params=_cparams(("arbitrary",), VMEM_LIMIT),
        name="merge_out_proj",
    )(*args)


DMA_UNROLL = 8


def _dispatch_kernel(slot_ref, h_ref, xs_in_ref, xs_ref, sem):
    del xs_in_ref

    def start(r, carry):
        for k in range(TOP_K):
            dst = slot_ref[0, r * TOP_K + k]
            pltpu.make_async_copy(h_ref.at[pl.ds(r, 1)], xs_ref.at[pl.ds(dst, 1)], sem).start()
        return carry

    lax.fori_loop(0, ROW_TILE, start, 0, unroll=DMA_UNROLL)
    for _ in range(TOP_K):
        pltpu.make_async_copy(h_ref, xs_ref.at[pl.ds(0, ROW_TILE)], sem).wait()


def _dispatch(slot3, h2):
    xs_init = jnp.zeros((MOE_SEGS * SEG_ROWS, D_MODEL), h2.dtype)
    return pl.pallas_call(
        _dispatch_kernel,
        grid=(N_TILES_LATENT,),
        in_specs=[pl.BlockSpec((None, 1, ROW_TILE * TOP_K), lambda i: (i, 0, 0), memory_space=pltpu.SMEM),
                  pl.BlockSpec((ROW_TILE, D_MODEL), lambda i: (i, 0)),
                  pl.BlockSpec(memory_space=pl.ANY)],
        out_specs=pl.BlockSpec(memory_space=pl.ANY),
        out_shape=jax.ShapeDtypeStruct(xs_init.shape, xs_init.dtype),
        input_output_aliases={2: 0},
        scratch_shapes=[pltpu.SemaphoreType.DMA(())],
        compiler_params=_cparams(("arbitrary",)),
        name="moe_dispatch",
    )(slot3, h2, xs_init)


def _ffn_kernel(se_ref, st_ref, sb_ref, x_ref, wg_ref, wu_ref, wd_ref, o_ref):
    del se_ref, sb_ref
    n_tiles = st_ref[pl.program_id(0)]

    @pl.when(jnp.logical_and(pl.program_id(1) == 0, n_tiles > 0))
    def _():
        o_ref[...] = jnp.zeros_like(o_ref)

    def rows_step(row0, n_rows):
        rows = pl.ds(row0 if isinstance(row0, int) else pl.multiple_of(row0, 16), n_rows)
        x = x_ref[rows, :].astype(BF16)
        g = jnp.dot(x, wg_ref[...].astype(BF16), preferred_element_type=F32)
        u = jnp.dot(x, wu_ref[...].astype(BF16), preferred_element_type=F32)
        a = (g * jax.nn.sigmoid(g)) * u
        o_ref[rows, :] += jnp.dot(a.astype(BF16), wd_ref[...].astype(BF16), preferred_element_type=F32)

    @pl.when(n_tiles == FFN_TILES_PER_SEG)
    def _():
        rows_step(0, SEG_ROWS)

    @pl.when(n_tiles < FFN_TILES_PER_SEG)
    def _():
        def pair(i, carry):
            rows_step(i * (2 * FFN_TILE), 2 * FFN_TILE)
            return carry

        lax.fori_loop(0, lax.shift_right_logical(n_tiles, 1), pair, 0)

        @pl.when((n_tiles & 1) == 1)
        def _():
            rows_step((n_tiles - 1) * FFN_TILE, FFN_TILE)


def _ffn(seg_expert, seg_tiles, xs, wg, wu, wd):
    n_seg = xs.shape[0] // SEG_ROWS
    in_place = xs.dtype == F32
    tf = FFN_CHUNK * (jnp.dtype(F32).itemsize // jnp.dtype(xs.dtype).itemsize)
    n_chunks = D_FF // tf
    seg_block = jnp.minimum(jnp.arange(n_seg, dtype=jnp.int32), jnp.sum((seg_tiles > 0).astype(jnp.int32)) - 1)

    def chunk(s, f, st):
        return jnp.where(st[s] > 0, f, n_chunks - 1)

    grid_spec = pltpu.PrefetchScalarGridSpec(
        num_scalar_prefetch=3,
        grid=(n_seg, n_chunks),
        in_specs=[pl.BlockSpec((SEG_ROWS, D_MODEL), lambda s, f, se, st, sb: (sb[s], 0)),
                  pl.BlockSpec((None, D_MODEL, tf), lambda s, f, se, st, sb: (se[s], 0, chunk(s, f, st))),
                  pl.BlockSpec((None, D_MODEL, tf), lambda s, f, se, st, sb: (se[s], 0, chunk(s, f, st))),
                  pl.BlockSpec((None, tf, D_MODEL), lambda s, f, se, st, sb: (se[s], chunk(s, f, st), 0))],
        out_specs=pl.BlockSpec((SEG_ROWS, D_MODEL), lambda s, f, se, st, sb: (sb[s], 0)),
    )
    return pl.pallas_call(
        _ffn_kernel,
        grid_spec=grid_spec,
        out_shape=jax.ShapeDtypeStruct((n_seg * SEG_ROWS, D_MODEL), F32),
        input_output_aliases={3: 0} if in_place else {},
        compiler_params=_cparams(("arbitrary", "arbitrary"), VMEM_LIMIT),
        name="swiglu_ffn",
    )(seg_expert, seg_tiles, seg_block, xs, wg, wu, wd)


def _post_kernel(y_ref, x_ref, gpost_ref, g2_ref, gpre_ref, sh_ref, sc_ref, xo_ref, h_ref):
    x = x_ref[...] + g2_ref[...] * _rmsnorm(y_ref[...], gpost_ref[...])
    xo_ref[...] = x
    h = _rmsnorm(x, gpre_ref[...]) * (1.0 + sc_ref[...]) + sh_ref[...]
    h_ref[...] = h.astype(h_ref.dtype)


def _post(y, x, g_post, g_pre_next, mods3, layer):
    row = pl.BlockSpec((ROW_TILE, D_MODEL), lambda i: (i, 0))
    return pl.pallas_call(
        _post_kernel,
        grid=(N_TILES_ALL,),
        in_specs=[row, row, _vec_spec(layer), _mod_spec(layer, 5, _tile_all),
                  _vec_spec(layer + 1), _mod_spec(layer + 1, 0, _tile_all), _mod_spec(layer + 1, 1, _tile_all)],
        out_specs=[row, row],
        out_shape=[jax.ShapeDtypeStruct((ROWS, D_MODEL), F32), jax.ShapeDtypeStruct((ROWS, D_MODEL), BF16)],
        compiler_params=_cparams(("arbitrary",)),
        name="post_ffn",
    )(y, x, g_post, mods3, g_pre_next, mods3, mods3)


def _combine_kernel(slot_ref, next_slot_ref, w_ref, x_ref, gpost_ref, g2_ref, ys_ref, o_ref, buf_ref, sem):
    i = pl.program_id(0)
    cur = i % 2

    def gather(slots, b):
        def start(r, carry):
            for k in range(TOP_K):
                src = slots[0, r * TOP_K + k]
                pltpu.make_async_copy(ys_ref.at[pl.ds(src, 1)], buf_ref.at[b, pl.ds(k * ROW_TILE + r, 1)],
                                      sem.at[b]).start()
            return carry
        lax.fori_loop(0, ROW_TILE, start, 0, unroll=DMA_UNROLL)

    @pl.when(i == 0)
    def _():
        gather(slot_ref, 0)

    @pl.when(i + 1 < pl.num_programs(0))
    def _():
        gather(next_slot_ref, 1 - cur)

    pltpu.make_async_copy(ys_ref.at[pl.ds(0, TOP_K * ROW_TILE)], buf_ref.at[cur], sem.at[cur]).wait()
    y = w_ref[:, 0:1] * buf_ref[cur, :ROW_TILE, :] + w_ref[:, 1:2] * buf_ref[cur, ROW_TILE:, :]
    o_ref[...] = x_ref[...] + g2_ref[...] * _rmsnorm(y, gpost_ref[...])


def _combine(slot3, weight, x, g_post, mods3, layer, ys):
    row = pl.BlockSpec((ROW_TILE, D_MODEL), lambda i: (i, 0))
    slots = lambda index: pl.BlockSpec((None, 1, ROW_TILE * TOP_K), index, memory_space=pltpu.SMEM)
    return pl.pallas_call(
        _combine_kernel,
        grid=(N_TILES_LATENT,),
        in_specs=[slots(lambda i: (i, 0, 0)),
                  slots(lambda i: (jnp.minimum(i + 1, N_TILES_LATENT - 1), 0, 0)),
                  pl.BlockSpec((ROW_TILE, TOP_K), lambda i: (i, 0)),
                  row, _vec_spec(layer), _mod_spec(layer, 5, _tile_latent),
                  pl.BlockSpec(memory_space=pl.ANY)],
        out_specs=row,
        out_shape=jax.ShapeDtypeStruct((BATCH * SEQ, D_MODEL), F32),
        scratch_shapes=[pltpu.VMEM((2, TOP_K * ROW_TILE, D_MODEL), F32), pltpu.SemaphoreType.DMA((2,))],
        compiler_params=_cparams(("arbitrary",), VMEM_LIMIT),
        name="moe_combine",
    )(slot3, slot3, weight, x, g_post, mods3, ys)


def _rope_table():
    pos = jnp.arange(SEQ, dtype=jnp.int32)
    row_ids = (pos // GRID_W).astype(F32)
    col_ids = (pos % GRID_W).astype(F32)
    n_freq = QK_ROPE // 4
    inv = ROPE_THETA ** (-jnp.arange(n_freq, dtype=F32) / n_freq)
    ang = jnp.concatenate([row_ids[:, None] * inv, col_ids[:, None] * inv], axis=-1)
    cos, sin = jnp.cos(ang), jnp.sin(ang)
    latent = jnp.concatenate([cos, cos, -sin, sin], axis=-1)
    ctx = jnp.concatenate([jnp.ones((CTX_LEN, QK_ROPE), F32), jnp.zeros((CTX_LEN, QK_ROPE), F32)], axis=-1)
    one = jnp.concatenate([ctx, latent], axis=0)
    return jnp.tile(one, (BATCH, 1))


def _routing(logits):
    probs = jax.nn.softmax(logits, axis=-1)
    top_p, top_i = lax.top_k(probs, TOP_K)
    top_p = top_p / jnp.sum(top_p, axis=-1, keepdims=True)
    onehot = jax.nn.one_hot(top_i.reshape(N_ASSIGN), N_EXPERTS, dtype=jnp.int32)
    csum = jnp.cumsum(onehot, axis=0)
    rank = jnp.sum((csum - onehot) * onehot, axis=1)
    counts = csum[-1]
    n_segs = (counts + SEG_ROWS - 1) // SEG_ROWS
    seg_start = jnp.cumsum(n_segs) - n_segs
    slot = jnp.sum(onehot * seg_start[None, :], axis=1) * SEG_ROWS + rank
    seg_ids = jnp.arange(MOE_SEGS, dtype=jnp.int32)
    used = jnp.sum(n_segs)
    seg_expert = jnp.sum((seg_ids[:, None] >= (seg_start + n_segs)[None, :]).astype(jnp.int32), axis=1)
    last_used_expert = jnp.max(jnp.where(counts > 0, jnp.arange(N_EXPERTS), 0))
    seg_expert = jnp.where(seg_ids < used, jnp.minimum(seg_expert, N_EXPERTS - 1), last_used_expert)
    rows_left = counts[seg_expert] - (seg_ids - seg_start[seg_expert]) * SEG_ROWS
    seg_tiles = jnp.clip((rows_left + FFN_TILE - 1) // FFN_TILE, 0, FFN_TILES_PER_SEG)
    seg_tiles = jnp.where(seg_ids < used, seg_tiles, 0)
    return (seg_expert.astype(jnp.int32), seg_tiles.astype(jnp.int32),
            slot.astype(jnp.int32).reshape(BATCH * SEQ, TOP_K), top_p)


def kernel(x, c, ctx, c_ctx, w_mod, b_mod, g_mix_pre, g_mix_post, g_ffn_pre, g_ffn_post, w_in, b_gate, g_q_lat, g_kv_lat, w_q_up, w_kv_up, w_pool, pool_scale, w_out, w_ff_gate, w_ff_up, w_ff_down, w_router, b_router, w_exp_gate, w_exp_up, w_exp_down):
    x2 = x.reshape(BATCH * SEQ, D_MODEL)
    ctx2 = ctx.reshape(BATCH * CTX_LEN, D_MODEL)

    c8 = jnp.concatenate([c, c_ctx[None, :], jnp.zeros((8 - BATCH - 1, D_MODEL), F32)], axis=0)
    mods = _modulation(c8, w_mod, b_mod)
    mods3 = mods[:, :BATCH + 1].reshape(DEPTH * 3 * 6, 1, D_MODEL)

    as_vec = lambda a: a.reshape(DEPTH, 1, a.shape[-1])
    g_mix_pre, g_mix_post, g_ffn_pre, g_ffn_post = map(as_vec, (g_mix_pre, g_mix_post, g_ffn_pre, g_ffn_post))
    g_q_lat, g_kv_lat, pool_scale = map(as_vec, (g_q_lat, g_kv_lat, pool_scale))
    b_gate2 = b_gate.reshape(DEPTH * 2, 1, D_MODEL)
    cs = _rope_table()

    w_in_t = jnp.swapaxes(w_in, 1, 2)
    w_kv, w_pool, w_out = (w.astype(BF16) for w in (w_kv_up, w_pool, w_out))

    h = _norm_mod(x2, ctx2, g_mix_pre, mods3, 0)
    xr = None
    out = None
    for l in range(DEPTH):
        last = l == DEPTH - 1
        p = _in_proj(h, w_in_t, l)
        q = _q_up(p, g_q_lat, w_q_up, cs, l)
        k, v = _kv_up(p, g_kv_lat, w_kv, h, w_in_t, cs, l)

        attn = _attention(q, k, v, with_ctx=not last).reshape(-1, D_MODEL)
        tile_fn, n_tiles = (_tile_latent, N_TILES_LATENT) if last else (_tile_all, N_TILES_ALL)
        pool = _pool(p, w_pool, pool_scale, l, tile_fn, n_tiles)
        x_args = (x2, ctx2) if l == 0 else (xr,)
        merge = functools.partial(_merge, p, attn, pool, b_gate2, w_out, x_args, g_mix_post,
                                  g_ffn_pre, mods3, l, tile_fn, n_tiles)

        if not last:
            xr, h2 = merge()
            n_seg = ROWS // SEG_ROWS
            seg_expert = jnp.zeros((n_seg,), jnp.int32)
            seg_tiles = jnp.full((n_seg,), FFN_TILES_PER_SEG, jnp.int32)
            y = _ffn(seg_expert, seg_tiles, h2, w_ff_gate, w_ff_up, w_ff_down)
            xr, h = _post(y, xr, g_ffn_post, g_mix_pre, mods3, l)
        else:
            w_r = jnp.pad(w_router[0], ((0, 0), (0, 128 - N_EXPERTS)))
            b_r = jnp.pad(b_router[0], (0, 128 - N_EXPERTS)).reshape(1, 128)
            xl, h2, logits = merge(router=(w_r, b_r))
            seg_expert, seg_tiles, slot, weight = _routing(logits[:, :N_EXPERTS])
            slot3 = slot.reshape(N_TILES_LATENT, 1, ROW_TILE * TOP_K)
            xs = _dispatch(slot3, h2)
            ys = _ffn(seg_expert, seg_tiles, xs, w_exp_gate.reshape(N_EXPERTS, D_MODEL, D_FF),
                      w_exp_up.reshape(N_EXPERTS, D_MODEL, D_FF), w_exp_down.reshape(N_EXPERTS, D_FF, D_MODEL))
            out = _combine(slot3, weight, xl, g_ffn_post, mods3, l, ys)
    return out.reshape(BATCH, SEQ, D_MODEL)
```

```python
import functools
import math

import jax
import jax.numpy as jnp
from jax import lax
from jax.experimental import pallas as pl
from jax.experimental.pallas import tpu as pltpu

F32 = jnp.float32
BF16 = jnp.bfloat16

D_MODEL = 2048
BATCH = 2
SEQ = 4096
DEPTH = 2
GRID_W = 64
CTX_LEN = 256
EPS = 1e-6

QK_NOPE = 128
QK_ROPE = 64
V_DIM = 128
N_HEADS = 16
Q_LORA = 512
KV_LORA = 512
ROPE_THETA = 10000.0
ATTN_SCALE = (QK_NOPE + QK_ROPE) ** -0.5
Q_PRESCALE = ATTN_SCALE * math.log2(math.e)

POOL_WINDOWS = (2, 4, 8, 16)
N_POOL_GROUPS = 4
POOL_WIDTH = 1024
POOL_GROUP = 256
POOL_OUT_GROUP = 512

OFF_KR = 1024
OFF_POOL = 1088

D_FF = 7168
N_EXPERTS = 8
TOP_K = 2

T_ROWS = CTX_LEN + SEQ
ROWS = BATCH * T_ROWS
ROW_TILE = 256
TILES_PER_BATCH = T_ROWS // ROW_TILE
LATENT_TILES = SEQ // ROW_TILE
N_TILES_ALL = BATCH * TILES_PER_BATCH
N_TILES_LATENT = BATCH * LATENT_TILES
HALO = 16
K_WIDTH = 256
HEADS_PER_STEP = 8

COL_Q, COL_KV, COL_POOL, COL_GATE = 0, 512, 1024, 2048

FFN_TILE = 272
FFN_TILES_PER_SEG = 4
SEG_ROWS = FFN_TILE * FFN_TILES_PER_SEG
FFN_CHUNK = 512
N_ASSIGN = BATCH * SEQ * TOP_K
MOE_SEGS = N_ASSIGN // SEG_ROWS + N_EXPERTS

VMEM_LIMIT = 56 * 1024 * 1024


def _cparams(sem, vmem=None):
    return pltpu.CompilerParams(dimension_semantics=sem, vmem_limit_bytes=vmem)


def _tile_all(i):
    return i


def _tile_latent(i):
    return (i // LATENT_TILES) * TILES_PER_BATCH + 1 + i % LATENT_TILES


def _group_of_tile(t):
    return jnp.where(t % TILES_PER_BATCH == 0, BATCH, t // TILES_PER_BATCH)


def _mod_spec(layer, chunk, tile_fn):
    def index(i):
        return ((layer * 3 + _group_of_tile(tile_fn(i))) * 6 + chunk, 0, 0)
    return pl.BlockSpec((None, 1, D_MODEL), index)


def _vec_spec(layer, width=D_MODEL):
    return pl.BlockSpec((None, 1, width), lambda i: (layer, 0, 0))


def _input_stream_specs():
    def x_index(t):
        return ((t // TILES_PER_BATCH) * LATENT_TILES + jnp.maximum(t % TILES_PER_BATCH - 1, 0), 0)

    def ctx_index(t):
        return (t // TILES_PER_BATCH, 0)

    return (pl.BlockSpec((ROW_TILE, D_MODEL), x_index), pl.BlockSpec((CTX_LEN, D_MODEL), ctx_index))


def _stream_tile(x_ref, ctx_ref, tile):
    return jnp.where(tile % TILES_PER_BATCH == 0, ctx_ref[...], x_ref[...])


def _rmsnorm(x, g):
    return x * lax.rsqrt(jnp.mean(x * x, axis=-1, keepdims=True) + EPS) * g


def _mod_kernel(c_ref, w_ref, b_ref, o_ref):
    c = c_ref[...]
    a = c * jax.nn.sigmoid(c)
    o_ref[...] = jnp.dot(a, w_ref[...], preferred_element_type=F32) + b_ref[...]


def _modulation(c8, w_mod, b_mod):
    n = w_mod.shape[2]
    tn = 1024
    return pl.pallas_call(
        _mod_kernel,
        grid=(DEPTH, n // tn),
        in_specs=[pl.BlockSpec((8, D_MODEL), lambda l, j: (0, 0)),
                  pl.BlockSpec((None, D_MODEL, tn), lambda l, j: (l, 0, j)),
                  pl.BlockSpec((None, 1, tn), lambda l, j: (l, 0, j))],
        out_specs=pl.BlockSpec((None, 8, tn), lambda l, j: (l, 0, j)),
        out_shape=jax.ShapeDtypeStruct((DEPTH, 8, n), F32),
        compiler_params=_cparams(("arbitrary", "arbitrary"), VMEM_LIMIT),
        name="modulation",
    )(c8, w_mod, b_mod.reshape(DEPTH, 1, n))


def _norm_mod_kernel(x_ref, ctx_ref, g_ref, sh_ref, sc_ref, o_ref):
    y = _rmsnorm(_stream_tile(x_ref, ctx_ref, pl.program_id(0)), g_ref[...])
    o_ref[...] = (y * (1.0 + sc_ref[...]) + sh_ref[...]).astype(o_ref.dtype)


def _norm_mod(x2, ctx2, g, mods3, layer):
    x_spec, ctx_spec = _input_stream_specs()
    return pl.pallas_call(
        _norm_mod_kernel,
        grid=(N_TILES_ALL,),
        in_specs=[x_spec, ctx_spec, _vec_spec(layer),
                  _mod_spec(layer, 0, _tile_all), _mod_spec(layer, 1, _tile_all)],
        out_specs=pl.BlockSpec((ROW_TILE, D_MODEL), lambda i: (i, 0)),
        out_shape=jax.ShapeDtypeStruct((ROWS, D_MODEL), BF16),
        compiler_params=_cparams(("arbitrary",)),
        name="norm_mod",
    )(x2, ctx2, g, mods3, mods3)


_CONTRACT_LAST = (((1,), (1,)), ((), ()))
IN_TILE = 1024
IN_ROWS = 1088
MAIN_WIDTH = Q_LORA + KV_LORA + POOL_WIDTH + 2 * D_MODEL


def _in_proj_kernel(layer, a_ref, wt_ref, o_ref, wbuf_ref, wcast_ref, sem):
    j = pl.program_id(0)

    def fetch(tile):
        row0 = tile * IN_TILE + jnp.where(tile * IN_TILE >= OFF_KR, QK_ROPE, 0)
        rows = pl.ds(pl.multiple_of(row0, QK_ROPE), IN_TILE)
        slot = tile % 2
        return pltpu.make_async_copy(wt_ref.at[layer, rows], wbuf_ref.at[slot], sem.at[slot])

    @pl.when(pl.program_id(1) == 0)
    def _():
        @pl.when(j == 0)
        def _():
            fetch(j).start()

        @pl.when(j + 1 < pl.num_programs(0))
        def _():
            fetch(j + 1).start()

        fetch(j).wait()
        wcast_ref[...] = wbuf_ref[j % 2].astype(wcast_ref.dtype)

    y = lax.dot_general(a_ref[...], wcast_ref[...], _CONTRACT_LAST, preferred_element_type=F32)
    o_ref[...] = y.astype(o_ref.dtype)


def _in_proj(a, w_in_t, layer):
    m, k = a.shape
    assert OFF_KR % IN_TILE == 0 and MAIN_WIDTH % IN_TILE == 0 and m % IN_ROWS == 0
    return pl.pallas_call(
        functools.partial(_in_proj_kernel, layer),
        grid=(MAIN_WIDTH // IN_TILE, m // IN_ROWS),
        in_specs=[pl.BlockSpec((IN_ROWS, k), lambda j, i: (i, 0)),
                  pl.BlockSpec(memory_space=pl.ANY)],
        out_specs=pl.BlockSpec((IN_ROWS, IN_TILE), lambda j, i: (i, j)),
        out_shape=jax.ShapeDtypeStruct((m, MAIN_WIDTH), BF16),
        scratch_shapes=[pltpu.VMEM((2, IN_TILE, k), F32), pltpu.VMEM((IN_TILE, k), BF16),
                        pltpu.SemaphoreType.DMA((2,))],
        compiler_params=_cparams(("arbitrary", "arbitrary"), VMEM_LIMIT),
        name="in_proj",
    )(a, w_in_t)


def _rope_rotate(t, cs):
    lane = lax.broadcasted_iota(jnp.int32, t.shape, 1)
    s_tab = pltpu.roll(cs, QK_ROPE, 1)
    return jnp.where(lane < QK_ROPE, t * cs + _swap_rope_halves(t) * s_tab, 0.0)


def _swap_rope_halves(t):
    half = QK_ROPE // 2
    lane = lax.broadcasted_iota(jnp.int32, t.shape, 1)
    return jnp.where(lane < half, pltpu.roll(t, 128 - half, 1), pltpu.roll(t, half, 1))


def _q_up_kernel(p_ref, g_ref, w_ref, cs_ref, q_ref, w2_ref):
    head = QK_NOPE + QK_ROPE

    @pl.when(pl.program_id(0) == 0)
    def _():
        lane = lax.broadcasted_iota(jnp.int32, (Q_LORA, 128), 1)
        w = w_ref[...]
        for h in range(N_HEADS):
            base = h * head
            w2_ref[:, h * K_WIDTH:h * K_WIDTH + QK_NOPE] = w[:, base:base + QK_NOPE].astype(w2_ref.dtype)
            tail = pltpu.roll(w[:, base + head - 128:base + head], QK_ROPE, 1)
            both = jnp.where(lane < QK_ROPE, tail, pltpu.roll(_swap_rope_halves(tail), QK_ROPE, 1))
            w2_ref[:, h * K_WIDTH + QK_NOPE:(h + 1) * K_WIDTH] = both.astype(w2_ref.dtype)

    n = _rmsnorm(p_ref[...].astype(F32), g_ref[...]).astype(BF16)
    y = jnp.dot(n, w2_ref[...], preferred_element_type=F32) * Q_PRESCALE
    cs = cs_ref[...]
    lane = lax.broadcasted_iota(jnp.int32, (ROW_TILE, 128), 1)
    for h in range(N_HEADS):
        yh = y[:, h * K_WIDTH:(h + 1) * K_WIDTH]
        q_ref[h, :, :QK_NOPE] = yh[:, :QK_NOPE].astype(q_ref.dtype)
        t = yh[:, QK_NOPE:] * cs
        q_ref[h, :, QK_NOPE:] = jnp.where(lane < QK_ROPE, t + pltpu.roll(t, QK_ROPE, 1), 0.0).astype(q_ref.dtype)


def _head_index(i):
    return (i // TILES_PER_BATCH, 0, i % TILES_PER_BATCH, 0)


def _q_up(p, g_q, w_q_up, cs, layer):
    return pl.pallas_call(
        _q_up_kernel,
        grid=(N_TILES_ALL,),
        in_specs=[pl.BlockSpec((ROW_TILE, Q_LORA), lambda i: (i, COL_Q // Q_LORA)),
                  _vec_spec(layer, Q_LORA),
                  pl.BlockSpec((None, Q_LORA, N_HEADS * (QK_NOPE + QK_ROPE)), lambda i: (layer, 0, 0)),
                  pl.BlockSpec((ROW_TILE, 128), lambda i: (i, 0))],
        out_specs=pl.BlockSpec((None, N_HEADS, ROW_TILE, K_WIDTH), _head_index),
        out_shape=jax.ShapeDtypeStruct((BATCH, N_HEADS, T_ROWS, K_WIDTH), BF16),
        scratch_shapes=[pltpu.VMEM((Q_LORA, N_HEADS * K_WIDTH), BF16)],
        compiler_params=_cparams(("arbitrary",), VMEM_LIMIT),
        name="q_up",
    )(p, g_q, w_q_up, cs)


def _kv_up_kernel(p_ref, g_ref, w_ref, h_ref, wkr_ref, cs_ref, k_ref, v_ref):
    n = _rmsnorm(p_ref[...].astype(F32), g_ref[...]).astype(BF16)
    y = jnp.dot(n, w_ref[...], preferred_element_type=F32)
    kr = lax.dot_general(h_ref[...], wkr_ref[...].astype(BF16), _CONTRACT_LAST, preferred_element_type=F32)
    kr = _rope_rotate(kr, cs_ref[...]).astype(k_ref.dtype)
    for h in range(N_HEADS):
        base = h * (QK_NOPE + V_DIM)
        k_ref[h, :, :QK_NOPE] = y[:, base:base + QK_NOPE].astype(k_ref.dtype)
        k_ref[h, :, QK_NOPE:] = kr
        v_ref[h] = y[:, base + QK_NOPE:base + QK_NOPE + V_DIM].astype(v_ref.dtype)


def _kv_up(p, g_kv, w_kv, h, w_in_t, cs, layer):
    return pl.pallas_call(
        _kv_up_kernel,
        grid=(N_TILES_ALL,),
        in_specs=[pl.BlockSpec((ROW_TILE, KV_LORA), lambda i: (i, COL_KV // KV_LORA)),
                  _vec_spec(layer, KV_LORA),
                  pl.BlockSpec((None, KV_LORA, N_HEADS * (QK_NOPE + V_DIM)), lambda i: (layer, 0, 0)),
                  pl.BlockSpec((ROW_TILE, D_MODEL), lambda i: (i, 0)),
                  pl.BlockSpec((None, 128, D_MODEL), lambda i: (layer, OFF_KR // 128, 0)),
                  pl.BlockSpec((ROW_TILE, 128), lambda i: (i, 0))],
        out_specs=[pl.BlockSpec((None, N_HEADS, ROW_TILE, K_WIDTH), _head_index),
                   pl.BlockSpec((None, N_HEADS, ROW_TILE, V_DIM), _head_index)],
        out_shape=[jax.ShapeDtypeStruct((BATCH, N_HEADS, T_ROWS, K_WIDTH), BF16),
                   jax.ShapeDtypeStruct((BATCH, N_HEADS, T_ROWS, V_DIM), BF16)],
        compiler_params=_cparams(("arbitrary",), VMEM_LIMIT),
        name="kv_up",
    )(p, g_kv, w_kv, h, w_in_t, cs)


def _attend(q, k, v):
    s = lax.dot_general(q, k, (((1,), (1,)), ((), ())), preferred_element_type=F32)
    m = jnp.max(s, axis=-1, keepdims=True)
    e = jnp.exp2(s - m)
    denom = jnp.sum(e, axis=-1, keepdims=True)
    o = jnp.dot(e.astype(BF16), v, preferred_element_type=F32)
    return o / denom


def _attention_kernel(with_ctx, q_ref, k_ref, v_ref, o_ref):
    def run(n_keys):
        for h in range(HEADS_PER_STEP):
            o = _attend(q_ref[h], k_ref[h, :n_keys, :], v_ref[h, :n_keys, :])
            o_ref[:, h * V_DIM:(h + 1) * V_DIM] = o.astype(o_ref.dtype)

    if not with_ctx:
        run(T_ROWS)
        return

    pl.when(pl.program_id(2) == 0)(functools.partial(run, CTX_LEN))
    pl.when(pl.program_id(2) > 0)(functools.partial(run, T_ROWS))


def _attention(q, k, v, with_ctx):
    q0 = 0 if with_ctx else 1
    n_q = TILES_PER_BATCH - q0
    hps = HEADS_PER_STEP
    return pl.pallas_call(
        functools.partial(_attention_kernel, with_ctx),
        grid=(BATCH, N_HEADS // hps, n_q),
        in_specs=[pl.BlockSpec((None, hps, ROW_TILE, K_WIDTH), lambda b, h, i: (b, h, i + q0, 0)),
                  pl.BlockSpec((None, hps, T_ROWS, K_WIDTH), lambda b, h, i: (b, h, 0, 0),
                               pipeline_mode=pl.Buffered(1)),
                  pl.BlockSpec((None, hps, T_ROWS, V_DIM), lambda b, h, i: (b, h, 0, 0),
                               pipeline_mode=pl.Buffered(1))],
        out_specs=pl.BlockSpec((None, ROW_TILE, hps * V_DIM), lambda b, h, i: (b, i, h)),
        out_shape=jax.ShapeDtypeStruct((BATCH, n_q * ROW_TILE, D_MODEL), BF16),
        compiler_params=_cparams(("arbitrary", "arbitrary", "arbitrary"), VMEM_LIMIT),
        name="attention",
    )(q, k, v)


def _pool_kernel(tile_fn, prev_ref, cur_ref, next_ref, w_ref, ps_ref, o_ref, buf_ref):
    t = tile_fn(pl.program_id(0)) % TILES_PER_BATCH
    is_ctx = t == 0
    seg_len = jnp.where(is_ctx, CTX_LEN, SEQ)
    pos0 = jnp.where(is_ctx, 0, (t - 1) * ROW_TILE)
    has_prev = pos0 > 0
    has_next = pos0 + ROW_TILE < seg_len
    buf_ref[0:HALO, :] = jnp.where(has_prev, prev_ref[...].astype(F32), 0.0)
    buf_ref[HALO:HALO + ROW_TILE, :] = cur_ref[...].astype(F32)
    buf_ref[HALO + ROW_TILE:, :] = jnp.where(has_next, next_ref[...].astype(F32), 0.0)

    pos = pos0 + lax.broadcasted_iota(jnp.int32, (ROW_TILE, 1), 0)
    for g, win in enumerate(POOL_WINDOWS):
        cols = slice(g * POOL_GROUP, (g + 1) * POOL_GROUP)
        half = win // 2
        acc = buf_ref[HALO - half:HALO - half + ROW_TILE, cols]
        for j in range(1, win):
            acc = acc + buf_ref[HALO - half + j:HALO - half + j + ROW_TILE, cols]
        lo = jnp.maximum(pos - half, 0)
        hi = jnp.minimum(pos - half + win, seg_len)
        mean = acc / (hi - lo).astype(F32)
        pooled = (mean - buf_ref[HALO:HALO + ROW_TILE, cols]).astype(BF16)
        out = jnp.dot(pooled, w_ref[g], preferred_element_type=F32)
        ocols = slice(g * POOL_OUT_GROUP, (g + 1) * POOL_OUT_GROUP)
        o_ref[:, ocols] = (out * ps_ref[:, ocols]).astype(o_ref.dtype)


def _pool(p, w_pool, pool_scale, layer, tile_fn, n_tiles):
    per16 = ROW_TILE // HALO
    last16 = ROWS // HALO - 1
    col = COL_POOL // POOL_WIDTH
    return pl.pallas_call(
        functools.partial(_pool_kernel, tile_fn),
        grid=(n_tiles,),
        in_specs=[pl.BlockSpec((HALO, POOL_WIDTH), lambda i: (jnp.maximum(tile_fn(i) * per16 - 1, 0), col)),
                  pl.BlockSpec((ROW_TILE, POOL_WIDTH), lambda i: (tile_fn(i), col)),
                  pl.BlockSpec((HALO, POOL_WIDTH), lambda i: (jnp.minimum((tile_fn(i) + 1) * per16, last16), col)),
                  pl.BlockSpec((None, N_POOL_GROUPS, POOL_GROUP, POOL_OUT_GROUP), lambda i: (layer, 0, 0, 0)),
                  _vec_spec(layer)],
        out_specs=pl.BlockSpec((ROW_TILE, D_MODEL), lambda i: (i, 0)),
        out_shape=jax.ShapeDtypeStruct((n_tiles * ROW_TILE, D_MODEL), BF16),
        scratch_shapes=[pltpu.VMEM((ROW_TILE + 2 * HALO, POOL_WIDTH), F32)],
        compiler_params=_cparams(("arbitrary",)),
        name="pool",
    )(p, p, p, w_pool, pool_scale)


def _split_bf16(a):
    hi = a.astype(BF16)
    return hi, (a - hi.astype(F32)).astype(BF16)


def _merge_kernel(first, with_router, *refs):
    refs = list(refs)
    ga_ref, gb_ref, attn_ref, pool_ref, ba_ref, bb_ref, w_ref = refs[:7]
    del refs[:7]
    if first:
        x = _stream_tile(refs[0], refs[1], pl.program_id(0))
        del refs[:2]
    else:
        x = refs.pop(0)[...]
    gpost_ref, g1_ref, gpre_ref, sh_ref, sc_ref = refs[:5]
    del refs[:5]
    if with_router:
        wr_ref, br_ref, xo_ref, h_ref, lg_ref = refs
    else:
        xo_ref, h_ref = refs

    ga = jax.nn.sigmoid(ga_ref[...].astype(F32) + ba_ref[...])
    gb = jax.nn.sigmoid(gb_ref[...].astype(F32) + bb_ref[...])
    mixed = ga * attn_ref[...].astype(F32) + gb * pool_ref[...].astype(F32)
    y = jnp.dot(mixed.astype(BF16), w_ref[...], preferred_element_type=F32)
    x = x + g1_ref[...] * _rmsnorm(y, gpost_ref[...])
    xo_ref[...] = x
    h = _rmsnorm(x, gpre_ref[...]) * (1.0 + sc_ref[...]) + sh_ref[...]
    h_ref[...] = h.astype(BF16).astype(h_ref.dtype)
    if with_router:
        h_hi, h_lo = _split_bf16(h)
        w_hi, w_lo = _split_bf16(wr_ref[...])
        dot = functools.partial(jnp.dot, preferred_element_type=F32)
        lg_ref[...] = dot(h_hi, w_hi) + (dot(h_lo, w_hi) + dot(h_hi, w_lo)) + br_ref[...]


def _merge(p, attn, pool, b_gate2, w_out, x_args, g_post, g_pre, mods3, layer, tile_fn, n_tiles,
           router=None):
    first = len(x_args) == 2
    gate_a = COL_GATE // D_MODEL
    act = pl.BlockSpec((ROW_TILE, D_MODEL), lambda i: (i, 0))
    in_specs = [pl.BlockSpec((ROW_TILE, D_MODEL), lambda i: (tile_fn(i), gate_a)),
                pl.BlockSpec((ROW_TILE, D_MODEL), lambda i: (tile_fn(i), gate_a + 1)),
                act, act,
                pl.BlockSpec((None, 1, D_MODEL), lambda i: (layer * 2, 0, 0)),
                pl.BlockSpec((None, 1, D_MODEL), lambda i: (layer * 2 + 1, 0, 0)),
                pl.BlockSpec((None, D_MODEL, D_MODEL), lambda i: (layer, 0, 0))]
    if first:
        in_specs += list(_input_stream_specs())
    else:
        in_specs.append(pl.BlockSpec((ROW_TILE, D_MODEL), lambda i: (tile_fn(i), 0)))
    in_specs += [_vec_spec(layer), _mod_spec(layer, 2, tile_fn), _vec_spec(layer),
                 _mod_spec(layer, 3, tile_fn), _mod_spec(layer, 4, tile_fn)]
    args = [p, p, attn, pool, b_gate2, b_gate2, w_out, *x_args, g_post, mods3, g_pre, mods3, mods3]
    rows = n_tiles * ROW_TILE
    out_specs = [act, act]
    h_dtype = BF16 if router is None else F32
    out_shape = [jax.ShapeDtypeStruct((rows, D_MODEL), F32), jax.ShapeDtypeStruct((rows, D_MODEL), h_dtype)]
    if router is not None:
        in_specs += [pl.BlockSpec((D_MODEL, 128), lambda i: (0, 0)), pl.BlockSpec((1, 128), lambda i: (0, 0))]
        args += list(router)
        out_specs.append(pl.BlockSpec((ROW_TILE, 128), lambda i: (i, 0)))
        out_shape.append(jax.ShapeDtypeStruct((rows, 128), F32))
    return pl.pallas_call(
        functools.partial(_merge_kernel, first, router is not None),
        grid=(n_tiles,),
        in_specs=in_specs,
        out_specs=out_specs,
        out_shape=out_shape,
        compiler_params=_cparams(("arbitrary",), VMEM_LIMIT),
        name="merge_out_proj",
    )(*args)


DMA_UNROLL = 8


def _dispatch_kernel(slot_ref, h_ref, xs_in_ref, xs_ref, sem):
    del xs_in_ref

    def start(r, carry):
        for k in range(TOP_K):
            dst = slot_ref[0, r * TOP_K + k]
            pltpu.make_async_copy(h_ref.at[pl.ds(r, 1)], xs_ref.at[pl.ds(dst, 1)], sem).start()
        return carry

    lax.fori_loop(0, ROW_TILE, start, 0, unroll=DMA_UNROLL)
    for _ in range(TOP_K):
        pltpu.make_async_copy(h_ref, xs_ref.at[pl.ds(0, ROW_TILE)], sem).wait()


def _dispatch(slot3, h2):
    xs_init = jnp.zeros((MOE_SEGS * SEG_ROWS, D_MODEL), h2.dtype)
    return pl.pallas_call(
        _dispatch_kernel,
        grid=(N_TILES_LATENT,),
        in_specs=[pl.BlockSpec((None, 1, ROW_TILE * TOP_K), lambda i: (i, 0, 0), memory_space=pltpu.SMEM),
                  pl.BlockSpec((ROW_TILE, D_MODEL), lambda i: (i, 0)),
                  pl.BlockSpec(memory_space=pl.ANY)],
        out_specs=pl.BlockSpec(memory_space=pl.ANY),
        out_shape=jax.ShapeDtypeStruct(xs_init.shape, xs_init.dtype),
        input_output_aliases={2: 0},
        scratch_shapes=[pltpu.SemaphoreType.DMA(())],
        compiler_params=_cparams(("arbitrary",)),
        name="moe_dispatch",
    )(slot3, h2, xs_init)


def _ffn_kernel(se_ref, st_ref, sb_ref, x_ref, wg_ref, wu_ref, wd_ref, o_ref):
    del se_ref, sb_ref
    n_tiles = st_ref[pl.program_id(0)]

    @pl.when(jnp.logical_and(pl.program_id(1) == 0, n_tiles > 0))
    def _():
        o_ref[...] = jnp.zeros_like(o_ref)

    def rows_step(row0, n_rows):
        rows = pl.ds(row0 if isinstance(row0, int) else pl.multiple_of(row0, 16), n_rows)
        x = x_ref[rows, :].astype(BF16)
        g = jnp.dot(x, wg_ref[...].astype(BF16), preferred_element_type=F32)
        u = jnp.dot(x, wu_ref[...].astype(BF16), preferred_element_type=F32)
        a = (g * jax.nn.sigmoid(g)) * u
        o_ref[rows, :] += jnp.dot(a.astype(BF16), wd_ref[...].astype(BF16), preferred_element_type=F32)

    @pl.when(n_tiles == FFN_TILES_PER_SEG)
    def _():
        rows_step(0, SEG_ROWS)

    @pl.when(n_tiles < FFN_TILES_PER_SEG)
    def _():
        def pair(i, carry):
            rows_step(i * (2 * FFN_TILE), 2 * FFN_TILE)
            return carry

        lax.fori_loop(0, lax.shift_right_logical(n_tiles, 1), pair, 0)

        @pl.when((n_tiles & 1) == 1)
        def _():
            rows_step((n_tiles - 1) * FFN_TILE, FFN_TILE)


def _ffn(seg_expert, seg_tiles, xs, wg, wu, wd):
    n_seg = xs.shape[0] // SEG_ROWS
    in_place = xs.dtype == F32
    tf = FFN_CHUNK
    n_chunks = D_FF // tf
    x_mode = dict(pipeline_mode=pl.Buffered(1)) if in_place else {}
    seg_block = jnp.minimum(jnp.arange(n_seg, dtype=jnp.int32), jnp.sum((seg_tiles > 0).astype(jnp.int32)) - 1)

    def chunk(s, f, st):
        return jnp.where(st[s] > 0, f, n_chunks - 1)

    grid_spec = pltpu.PrefetchScalarGridSpec(
        num_scalar_prefetch=3,
        grid=(n_seg, n_chunks),
        in_specs=[pl.BlockSpec((SEG_ROWS, D_MODEL), lambda s, f, se, st, sb: (sb[s], 0), **x_mode),
                  pl.BlockSpec((None, D_MODEL, tf), lambda s, f, se, st, sb: (se[s], 0, chunk(s, f, st))),
                  pl.BlockSpec((None, D_MODEL, tf), lambda s, f, se, st, sb: (se[s], 0, chunk(s, f, st))),
                  pl.BlockSpec((None, tf, D_MODEL), lambda s, f, se, st, sb: (se[s], chunk(s, f, st), 0))],
        out_specs=pl.BlockSpec((SEG_ROWS, D_MODEL), lambda s, f, se, st, sb: (sb[s], 0)),
    )
    return pl.pallas_call(
        _ffn_kernel,
        grid_spec=grid_spec,
        out_shape=jax.ShapeDtypeStruct((n_seg * SEG_ROWS, D_MODEL), F32),
        input_output_aliases={3: 0} if in_place else {},
        compiler_params=_cparams(("arbitrary", "arbitrary"), VMEM_LIMIT),
        name="swiglu_ffn",
    )(seg_expert, seg_tiles, seg_block, xs, wg, wu, wd)


def _post_kernel(y_ref, x_ref, gpost_ref, g2_ref, gpre_ref, sh_ref, sc_ref, xo_ref, h_ref):
    x = x_ref[...] + g2_ref[...] * _rmsnorm(y_ref[...], gpost_ref[...])
    xo_ref[...] = x
    h = _rmsnorm(x, gpre_ref[...]) * (1.0 + sc_ref[...]) + sh_ref[...]
    h_ref[...] = h.astype(h_ref.dtype)


def _post(y, x, g_post, g_pre_next, mods3, layer):
    row = pl.BlockSpec((ROW_TILE, D_MODEL), lambda i: (i, 0))
    return pl.pallas_call(
        _post_kernel,
        grid=(N_TILES_ALL,),
        in_specs=[row, row, _vec_spec(layer), _mod_spec(layer, 5, _tile_all),
                  _vec_spec(layer + 1), _mod_spec(layer + 1, 0, _tile_all), _mod_spec(layer + 1, 1, _tile_all)],
        out_specs=[row, row],
        out_shape=[jax.ShapeDtypeStruct((ROWS, D_MODEL), F32), jax.ShapeDtypeStruct((ROWS, D_MODEL), BF16)],
        compiler_params=_cparams(("arbitrary",)),
        name="post_ffn",
    )(y, x, g_post, mods3, g_pre_next, mods3, mods3)


def _combine_kernel(slot_ref, next_slot_ref, w_ref, x_ref, gpost_ref, g2_ref, ys_ref, o_ref, buf_ref, sem):
    i = pl.program_id(0)
    cur = i % 2

    def gather(slots, b):
        def start(r, carry):
            for k in range(TOP_K):
                src = slots[0, r * TOP_K + k]
                pltpu.make_async_copy(ys_ref.at[pl.ds(src, 1)], buf_ref.at[b, pl.ds(k * ROW_TILE + r, 1)],
                                      sem.at[b]).start()
            return carry
        lax.fori_loop(0, ROW_TILE, start, 0, unroll=DMA_UNROLL)

    @pl.when(i == 0)
    def _():
        gather(slot_ref, 0)

    @pl.when(i + 1 < pl.num_programs(0))
    def _():
        gather(next_slot_ref, 1 - cur)

    pltpu.make_async_copy(ys_ref.at[pl.ds(0, TOP_K * ROW_TILE)], buf_ref.at[cur], sem.at[cur]).wait()
    y = w_ref[:, 0:1] * buf_ref[cur, :ROW_TILE, :] + w_ref[:, 1:2] * buf_ref[cur, ROW_TILE:, :]
    o_ref[...] = x_ref[...] + g2_ref[...] * _rmsnorm(y, gpost_ref[...])


def _combine(slot3, weight, x, g_post, mods3, layer, ys):
    row = pl.BlockSpec((ROW_TILE, D_MODEL), lambda i: (i, 0))
    slots = lambda index: pl.BlockSpec((None, 1, ROW_TILE * TOP_K), index, memory_space=pltpu.SMEM)
    return pl.pallas_call(
        _combine_kernel,
        grid=(N_TILES_LATENT,),
        in_specs=[slots(lambda i: (i, 0, 0)),
                  slots(lambda i: (jnp.minimum(i + 1, N_TILES_LATENT - 1), 0, 0)),
                  pl.BlockSpec((ROW_TILE, TOP_K), lambda i: (i, 0)),
                  row, _vec_spec(layer), _mod_spec(layer, 5, _tile_latent),
                  pl.BlockSpec(memory_space=pl.ANY)],
        out_specs=row,
        out_shape=jax.ShapeDtypeStruct((BATCH * SEQ, D_MODEL), F32),
        scratch_shapes=[pltpu.VMEM((2, TOP_K * ROW_TILE, D_MODEL), F32), pltpu.SemaphoreType.DMA((2,))],
        compiler_params=_cparams(("arbitrary",), VMEM_LIMIT),
        name="moe_combine",
    )(slot3, slot3, weight, x, g_post, mods3, ys)


def _rope_table():
    pos = jnp.arange(SEQ, dtype=jnp.int32)
    row_ids = (pos // GRID_W).astype(F32)
    col_ids = (pos % GRID_W).astype(F32)
    n_freq = QK_ROPE // 4
    inv = ROPE_THETA ** (-jnp.arange(n_freq, dtype=F32) / n_freq)
    ang = jnp.concatenate([row_ids[:, None] * inv, col_ids[:, None] * inv], axis=-1)
    cos, sin = jnp.cos(ang), jnp.sin(ang)
    latent = jnp.concatenate([cos, cos, -sin, sin], axis=-1)
    ctx = jnp.concatenate([jnp.ones((CTX_LEN, QK_ROPE), F32), jnp.zeros((CTX_LEN, QK_ROPE), F32)], axis=-1)
    one = jnp.concatenate([ctx, latent], axis=0)
    return jnp.tile(one, (BATCH, 1))


def _routing(logits):
    probs = jax.nn.softmax(logits, axis=-1)
    top_p, top_i = lax.top_k(probs, TOP_K)
    top_p = top_p / jnp.sum(top_p, axis=-1, keepdims=True)
    onehot = jax.nn.one_hot(top_i.reshape(N_ASSIGN), N_EXPERTS, dtype=jnp.int32)
    csum = jnp.cumsum(onehot, axis=0)
    rank = jnp.sum((csum - onehot) * onehot, axis=1)
    counts = csum[-1]
    n_segs = (counts + SEG_ROWS - 1) // SEG_ROWS
    seg_start = jnp.cumsum(n_segs) - n_segs
    slot = jnp.sum(onehot * seg_start[None, :], axis=1) * SEG_ROWS + rank
    seg_ids = jnp.arange(MOE_SEGS, dtype=jnp.int32)
    used = jnp.sum(n_segs)
    seg_expert = jnp.sum((seg_ids[:, None] >= (seg_start + n_segs)[None, :]).astype(jnp.int32), axis=1)
    last_used_expert = jnp.max(jnp.where(counts > 0, jnp.arange(N_EXPERTS), 0))
    seg_expert = jnp.where(seg_ids < used, jnp.minimum(seg_expert, N_EXPERTS - 1), last_used_expert)
    rows_left = counts[seg_expert] - (seg_ids - seg_start[seg_expert]) * SEG_ROWS
    seg_tiles = jnp.clip((rows_left + FFN_TILE - 1) // FFN_TILE, 0, FFN_TILES_PER_SEG)
    seg_tiles = jnp.where(seg_ids < used, seg_tiles, 0)
    return (seg_expert.astype(jnp.int32), seg_tiles.astype(jnp.int32),
            slot.astype(jnp.int32).reshape(BATCH * SEQ, TOP_K), top_p)


def kernel(x, c, ctx, c_ctx, w_mod, b_mod, g_mix_pre, g_mix_post, g_ffn_pre, g_ffn_post, w_in, b_gate, g_q_lat, g_kv_lat, w_q_up, w_kv_up, w_pool, pool_scale, w_out, w_ff_gate, w_ff_up, w_ff_down, w_router, b_router, w_exp_gate, w_exp_up, w_exp_down):
    x2 = x.reshape(BATCH * SEQ, D_MODEL)
    ctx2 = ctx.reshape(BATCH * CTX_LEN, D_MODEL)

    c8 = jnp.concatenate([c, c_ctx[None, :], jnp.zeros((8 - BATCH - 1, D_MODEL), F32)], axis=0)
    mods = _modulation(c8, w_mod, b_mod)
    mods3 = mods[:, :BATCH + 1].reshape(DEPTH * 3 * 6, 1, D_MODEL)

    as_vec = lambda a: a.reshape(DEPTH, 1, a.shape[-1])
    g_mix_pre, g_mix_post, g_ffn_pre, g_ffn_post = map(as_vec, (g_mix_pre, g_mix_post, g_ffn_pre, g_ffn_post))
    g_q_lat, g_kv_lat, pool_scale = map(as_vec, (g_q_lat, g_kv_lat, pool_scale))
    b_gate2 = b_gate.reshape(DEPTH * 2, 1, D_MODEL)
    cs = _rope_table()

    w_in_t = jnp.swapaxes(w_in, 1, 2)
    w_kv, w_pool, w_out = (w.astype(BF16) for w in (w_kv_up, w_pool, w_out))

    h = _norm_mod(x2, ctx2, g_mix_pre, mods3, 0)
    xr = None
    out = None
    for l in range(DEPTH):
        last = l == DEPTH - 1
        p = _in_proj(h, w_in_t, l)
        q = _q_up(p, g_q_lat, w_q_up, cs, l)
        k, v = _kv_up(p, g_kv_lat, w_kv, h, w_in_t, cs, l)

        attn = _attention(q, k, v, with_ctx=not last).reshape(-1, D_MODEL)
        tile_fn, n_tiles = (_tile_latent, N_TILES_LATENT) if last else (_tile_all, N_TILES_ALL)
        pool = _pool(p, w_pool, pool_scale, l, tile_fn, n_tiles)
        x_args = (x2, ctx2) if l == 0 else (xr,)
        merge = functools.partial(_merge, p, attn, pool, b_gate2, w_out, x_args, g_mix_post,
                                  g_ffn_pre, mods3, l, tile_fn, n_tiles)

        if not last:
            xr, h2 = merge()
            n_seg = ROWS // SEG_ROWS
            seg_expert = jnp.zeros((n_seg,), jnp.int32)
            seg_tiles = jnp.full((n_seg,), FFN_TILES_PER_SEG, jnp.int32)
            y = _ffn(seg_expert, seg_tiles, h2, w_ff_gate, w_ff_up, w_ff_down)
            xr, h = _post(y, xr, g_ffn_post, g_mix_pre, mods3, l)
        else:
            w_r = jnp.pad(w_router[0], ((0, 0), (0, 128 - N_EXPERTS)))
            b_r = jnp.pad(b_router[0], (0, 128 - N_EXPERTS)).reshape(1, 128)
            xl, h2, logits = merge(router=(w_r, b_r))
            seg_expert, seg_tiles, slot, weight = _routing(logits[:, :N_EXPERTS])
            slot3 = slot.reshape(N_TILES_LATENT, 1, ROW_TILE * TOP_K)
            xs = _dispatch(slot3, h2)
            ys = _ffn(seg_expert, seg_tiles, xs, w_exp_gate.reshape(N_EXPERTS, D_MODEL, D_FF),
                      w_exp_up.reshape(N_EXPERTS, D_MODEL, D_FF), w_exp_down.reshape(N_EXPERTS, D_FF, D_MODEL))
            out = _combine(slot3, weight, xl, g_ffn_post, mods3, l, ys)
    return out.reshape(BATCH, SEQ, D_MODEL)
```

```python
import functools
import math

import jax
import jax.numpy as jnp
from jax import lax
from jax.experimental import pallas as pl
from jax.experimental.pallas import tpu as pltpu

F32 = jnp.float32
BF16 = jnp.bfloat16

D_MODEL = 2048
BATCH = 2
SEQ = 4096
DEPTH = 2
GRID_W = 64
CTX_LEN = 256
EPS = 1e-6

QK_NOPE = 128
QK_ROPE = 64
V_DIM = 128
N_HEADS = 16
Q_LORA = 512
KV_LORA = 512
ROPE_THETA = 10000.0
ATTN_SCALE = (QK_NOPE + QK_ROPE) ** -0.5
Q_PRESCALE = ATTN_SCALE * math.log2(math.e)

POOL_WINDOWS = (2, 4, 8, 16)
N_POOL_GROUPS = 4
POOL_WIDTH = 1024
POOL_GROUP = 256
POOL_OUT_GROUP = 512

OFF_KR = 1024
LANES = 128

D_FF = 7168
N_EXPERTS = 8
TOP_K = 2

T_ROWS = CTX_LEN + SEQ
ROWS = BATCH * T_ROWS
ROW_TILE = 256
TILES_PER_BATCH = T_ROWS // ROW_TILE
LATENT_TILES = SEQ // ROW_TILE
N_TILES_ALL = BATCH * TILES_PER_BATCH
N_TILES_LATENT = BATCH * LATENT_TILES
HALO = 16
K_WIDTH = 256
HEADS_PER_STEP = 8

COL_Q, COL_KV, COL_POOL, COL_GATE = 0, 512, 1024, 2048

FFN_TILE = 272
FFN_TILES_PER_SEG = 4
SEG_ROWS = FFN_TILE * FFN_TILES_PER_SEG
FFN_CHUNK = 512
N_ASSIGN = BATCH * SEQ * TOP_K
MOE_SEGS = N_ASSIGN // SEG_ROWS + N_EXPERTS

VMEM_LIMIT = 56 * 1024 * 1024


def _cparams(sem, vmem=None):
    return pltpu.CompilerParams(dimension_semantics=sem, vmem_limit_bytes=vmem)


def _tile_all(i):
    return i


def _tile_latent(i):
    return (i // LATENT_TILES) * TILES_PER_BATCH + 1 + i % LATENT_TILES


def _group_of_tile(t):
    return jnp.where(t % TILES_PER_BATCH == 0, BATCH, t // TILES_PER_BATCH)


def _mod_spec(layer, chunk, tile_fn):
    def index(i):
        return ((layer * 3 + _group_of_tile(tile_fn(i))) * 6 + chunk, 0, 0)
    return pl.BlockSpec((None, 1, D_MODEL), index)


def _vec_spec(layer, width=D_MODEL):
    return pl.BlockSpec((None, 1, width), lambda i: (layer, 0, 0))


def _input_stream_specs():
    def x_index(t):
        return ((t // TILES_PER_BATCH) * LATENT_TILES + jnp.maximum(t % TILES_PER_BATCH - 1, 0), 0)

    def ctx_index(t):
        return (t // TILES_PER_BATCH, 0)

    return (pl.BlockSpec((ROW_TILE, D_MODEL), x_index), pl.BlockSpec((CTX_LEN, D_MODEL), ctx_index))


def _stream_tile(x_ref, ctx_ref, tile):
    return jnp.where(tile % TILES_PER_BATCH == 0, ctx_ref[...], x_ref[...])


def _rmsnorm(x, g):
    return x * lax.rsqrt(jnp.mean(x * x, axis=-1, keepdims=True) + EPS) * g


def _mod_kernel(c_ref, w_ref, b_ref, o_ref):
    c = c_ref[...]
    a = c * jax.nn.sigmoid(c)
    o_ref[...] = jnp.dot(a, w_ref[...], preferred_element_type=F32) + b_ref[...]


def _modulation(c8, w_mod, b_mod):
    n = w_mod.shape[2]
    tn = 1024
    return pl.pallas_call(
        _mod_kernel,
        grid=(DEPTH, n // tn),
        in_specs=[pl.BlockSpec((8, D_MODEL), lambda l, j: (0, 0)),
                  pl.BlockSpec((None, D_MODEL, tn), lambda l, j: (l, 0, j)),
                  pl.BlockSpec((None, 1, tn), lambda l, j: (l, 0, j))],
        out_specs=pl.BlockSpec((None, 8, tn), lambda l, j: (l, 0, j)),
        out_shape=jax.ShapeDtypeStruct((DEPTH, 8, n), F32),
        compiler_params=_cparams(("arbitrary", "arbitrary"), VMEM_LIMIT),
        name="modulation",
    )(c8, w_mod, b_mod.reshape(DEPTH, 1, n))


def _norm_mod_kernel(x_ref, ctx_ref, g_ref, sh_ref, sc_ref, o_ref):
    y = _rmsnorm(_stream_tile(x_ref, ctx_ref, pl.program_id(0)), g_ref[...])
    o_ref[...] = (y * (1.0 + sc_ref[...]) + sh_ref[...]).astype(o_ref.dtype)


def _norm_mod(x2, ctx2, g, mods3, layer):
    x_spec, ctx_spec = _input_stream_specs()
    return pl.pallas_call(
        _norm_mod_kernel,
        grid=(N_TILES_ALL,),
        in_specs=[x_spec, ctx_spec, _vec_spec(layer),
                  _mod_spec(layer, 0, _tile_all), _mod_spec(layer, 1, _tile_all)],
        out_specs=pl.BlockSpec((ROW_TILE, D_MODEL), lambda i: (i, 0)),
        out_shape=jax.ShapeDtypeStruct((ROWS, D_MODEL), BF16),
        compiler_params=_cparams(("arbitrary",)),
        name="norm_mod",
    )(x2, ctx2, g, mods3, mods3)


_CONTRACT_LAST = (((1,), (1,)), ((), ()))
IN_TILE = 1024
IN_ROWS = 1088
MAIN_WIDTH = Q_LORA + KV_LORA + POOL_WIDTH + 2 * D_MODEL


def _in_proj_kernel(layer, a_ref, wt_ref, o_ref, wbuf_ref, wcast_ref, sem):
    j = pl.program_id(0)

    def fetch(tile):
        row0 = tile * IN_TILE + jnp.where(tile * IN_TILE >= OFF_KR, QK_ROPE, 0)
        rows = pl.ds(pl.multiple_of(row0, QK_ROPE), IN_TILE)
        slot = tile % 2
        return pltpu.make_async_copy(wt_ref.at[layer, rows], wbuf_ref.at[slot], sem.at[slot])

    @pl.when(pl.program_id(1) == 0)
    def _():
        @pl.when(j == 0)
        def _():
            fetch(j).start()

        @pl.when(j + 1 < pl.num_programs(0))
        def _():
            fetch(j + 1).start()

        fetch(j).wait()
        wcast_ref[...] = wbuf_ref[j % 2].astype(wcast_ref.dtype)

    y = lax.dot_general(a_ref[...], wcast_ref[...], _CONTRACT_LAST, preferred_element_type=F32)
    o_ref[...] = y.astype(o_ref.dtype)


def _in_proj(a, w_in_t, layer):
    m, k = a.shape
    assert OFF_KR % IN_TILE == 0 and MAIN_WIDTH % IN_TILE == 0 and m % IN_ROWS == 0
    return pl.pallas_call(
        functools.partial(_in_proj_kernel, layer),
        grid=(MAIN_WIDTH // IN_TILE, m // IN_ROWS),
        in_specs=[pl.BlockSpec((IN_ROWS, k), lambda j, i: (i, 0)),
                  pl.BlockSpec(memory_space=pl.ANY)],
        out_specs=pl.BlockSpec((IN_ROWS, IN_TILE), lambda j, i: (i, j)),
        out_shape=jax.ShapeDtypeStruct((m, MAIN_WIDTH), BF16),
        scratch_shapes=[pltpu.VMEM((2, IN_TILE, k), F32), pltpu.VMEM((IN_TILE, k), BF16),
                        pltpu.SemaphoreType.DMA((2,))],
        compiler_params=_cparams(("arbitrary", "arbitrary"), VMEM_LIMIT),
        name="in_proj",
    )(a, w_in_t)


def _rope_rotate(t, cs):
    lane = lax.broadcasted_iota(jnp.int32, t.shape, 1)
    s_tab = pltpu.roll(cs, QK_ROPE, 1)
    return jnp.where(lane < QK_ROPE, t * cs + _swap_rope_halves(t) * s_tab, 0.0)


def _swap_rope_halves(t):
    half = QK_ROPE // 2
    lane = lax.broadcasted_iota(jnp.int32, t.shape, 1)
    return jnp.where(lane < half, pltpu.roll(t, LANES - half, 1), pltpu.roll(t, half, 1))


def _q_up_kernel(p_ref, g_ref, w_ref, cs_ref, q_ref, w2_ref):
    head = QK_NOPE + QK_ROPE

    @pl.when(pl.program_id(0) == 0)
    def _():
        lane = lax.broadcasted_iota(jnp.int32, (Q_LORA, LANES), 1)
        w = w_ref[...]
        for h in range(N_HEADS):
            base = h * head
            w2_ref[:, h * K_WIDTH:h * K_WIDTH + QK_NOPE] = w[:, base:base + QK_NOPE].astype(w2_ref.dtype)
            tail = pltpu.roll(w[:, base + head - LANES:base + head], QK_ROPE, 1)
            both = jnp.where(lane < QK_ROPE, tail, pltpu.roll(_swap_rope_halves(tail), QK_ROPE, 1))
            w2_ref[:, h * K_WIDTH + QK_NOPE:(h + 1) * K_WIDTH] = both.astype(w2_ref.dtype)

    n = _rmsnorm(p_ref[...].astype(F32), g_ref[...]).astype(BF16)
    y = jnp.dot(n, w2_ref[...], preferred_element_type=F32) * Q_PRESCALE
    cs = cs_ref[...]
    lane = lax.broadcasted_iota(jnp.int32, (ROW_TILE, LANES), 1)
    for h in range(N_HEADS):
        yh = y[:, h * K_WIDTH:(h + 1) * K_WIDTH]
        q_ref[h, :, :QK_NOPE] = yh[:, :QK_NOPE].astype(q_ref.dtype)
        t = yh[:, QK_NOPE:] * cs
        q_ref[h, :, QK_NOPE:] = jnp.where(lane < QK_ROPE, t + pltpu.roll(t, QK_ROPE, 1), 0.0).astype(q_ref.dtype)


def _head_index(i):
    return (i // TILES_PER_BATCH, 0, i % TILES_PER_BATCH, 0)


def _q_up(p, g_q, w_q_up, cs, layer):
    return pl.pallas_call(
        _q_up_kernel,
        grid=(N_TILES_ALL,),
        in_specs=[pl.BlockSpec((ROW_TILE, Q_LORA), lambda i: (i, COL_Q // Q_LORA)),
                  _vec_spec(layer, Q_LORA),
                  pl.BlockSpec((None, Q_LORA, N_HEADS * (QK_NOPE + QK_ROPE)), lambda i: (layer, 0, 0)),
                  pl.BlockSpec((ROW_TILE, LANES), lambda i: (i, 0))],
        out_specs=pl.BlockSpec((None, N_HEADS, ROW_TILE, K_WIDTH), _head_index),
        out_shape=jax.ShapeDtypeStruct((BATCH, N_HEADS, T_ROWS, K_WIDTH), BF16),
        scratch_shapes=[pltpu.VMEM((Q_LORA, N_HEADS * K_WIDTH), BF16)],
        compiler_params=_cparams(("arbitrary",), VMEM_LIMIT),
        name="q_up",
    )(p, g_q, w_q_up, cs)


def _kv_up_kernel(p_ref, g_ref, w_ref, h_ref, wkr_ref, cs_ref, k_ref, v_ref, wb_ref):
    @pl.when(pl.program_id(0) == 0)
    def _():
        wb_ref[...] = w_ref[...].astype(wb_ref.dtype)

    n = _rmsnorm(p_ref[...].astype(F32), g_ref[...]).astype(BF16)
    y = jnp.dot(n, wb_ref[...], preferred_element_type=F32)
    kr = lax.dot_general(h_ref[...], wkr_ref[...].astype(BF16), _CONTRACT_LAST, preferred_element_type=F32)
    kr = _rope_rotate(kr, cs_ref[...]).astype(k_ref.dtype)
    for h in range(N_HEADS):
        base = h * (QK_NOPE + V_DIM)
        k_ref[h, :, :QK_NOPE] = y[:, base:base + QK_NOPE].astype(k_ref.dtype)
        k_ref[h, :, QK_NOPE:] = kr
        v_ref[h] = y[:, base + QK_NOPE:base + QK_NOPE + V_DIM].astype(v_ref.dtype)


def _kv_up(p, g_kv, w_kv, h, w_in_t, cs, layer):
    return pl.pallas_call(
        _kv_up_kernel,
        grid=(N_TILES_ALL,),
        in_specs=[pl.BlockSpec((ROW_TILE, KV_LORA), lambda i: (i, COL_KV // KV_LORA)),
                  _vec_spec(layer, KV_LORA),
                  pl.BlockSpec((None, KV_LORA, N_HEADS * (QK_NOPE + V_DIM)), lambda i: (layer, 0, 0)),
                  pl.BlockSpec((ROW_TILE, D_MODEL), lambda i: (i, 0)),
                  pl.BlockSpec((None, LANES, D_MODEL), lambda i: (layer, OFF_KR // LANES, 0)),
                  pl.BlockSpec((ROW_TILE, LANES), lambda i: (i, 0))],
        out_specs=[pl.BlockSpec((None, N_HEADS, ROW_TILE, K_WIDTH), _head_index),
                   pl.BlockSpec((None, N_HEADS, ROW_TILE, V_DIM), _head_index)],
        out_shape=[jax.ShapeDtypeStruct((BATCH, N_HEADS, T_ROWS, K_WIDTH), BF16),
                   jax.ShapeDtypeStruct((BATCH, N_HEADS, T_ROWS, V_DIM), BF16)],
        scratch_shapes=[pltpu.VMEM((KV_LORA, N_HEADS * (QK_NOPE + V_DIM)), BF16)],
        compiler_params=_cparams(("arbitrary",), VMEM_LIMIT),
        name="kv_up",
    )(p, g_kv, w_kv, h, w_in_t, cs)


def _attend(q, k, v):
    s = lax.dot_general(q, k, (((1,), (1,)), ((), ())), preferred_element_type=F32)
    m = jnp.max(s, axis=-1, keepdims=True)
    e = jnp.exp2(s - m)
    denom = jnp.sum(e, axis=-1, keepdims=True)
    o = jnp.dot(e.astype(BF16), v, preferred_element_type=F32)
    return o / denom


def _attention_kernel(with_ctx, q_ref, k_ref, v_ref, o_ref):
    def run(n_keys):
        for h in range(HEADS_PER_STEP):
            o = _attend(q_ref[h], k_ref[h, :n_keys, :], v_ref[h, :n_keys, :])
            o_ref[:, h * V_DIM:(h + 1) * V_DIM] = o.astype(o_ref.dtype)

    if not with_ctx:
        run(T_ROWS)
        return

    pl.when(pl.program_id(2) == 0)(functools.partial(run, CTX_LEN))
    pl.when(pl.program_id(2) > 0)(functools.partial(run, T_ROWS))


def _attention(q, k, v, with_ctx):
    q0 = 0 if with_ctx else 1
    n_q = TILES_PER_BATCH - q0
    hps = HEADS_PER_STEP
    return pl.pallas_call(
        functools.partial(_attention_kernel, with_ctx),
        grid=(BATCH, N_HEADS // hps, n_q),
        in_specs=[pl.BlockSpec((None, hps, ROW_TILE, K_WIDTH), lambda b, h, i: (b, h, i + q0, 0)),
                  pl.BlockSpec((None, hps, T_ROWS, K_WIDTH), lambda b, h, i: (b, h, 0, 0),
                               pipeline_mode=pl.Buffered(1)),
                  pl.BlockSpec((None, hps, T_ROWS, V_DIM), lambda b, h, i: (b, h, 0, 0),
                               pipeline_mode=pl.Buffered(1))],
        out_specs=pl.BlockSpec((None, ROW_TILE, hps * V_DIM), lambda b, h, i: (b, i, h)),
        out_shape=jax.ShapeDtypeStruct((BATCH, n_q * ROW_TILE, D_MODEL), BF16),
        compiler_params=_cparams(("arbitrary", "arbitrary", "arbitrary"), VMEM_LIMIT),
        name="attention",
    )(q, k, v)


def _pool_kernel(tile_fn, prev_ref, cur_ref, next_ref, w_ref, ps_ref, o_ref, buf_ref):
    t = tile_fn(pl.program_id(0)) % TILES_PER_BATCH
    is_ctx = t == 0
    seg_len = jnp.where(is_ctx, CTX_LEN, SEQ)
    pos0 = jnp.where(is_ctx, 0, (t - 1) * ROW_TILE)
    has_prev = pos0 > 0
    has_next = pos0 + ROW_TILE < seg_len
    buf_ref[0:HALO, :] = jnp.where(has_prev, prev_ref[...].astype(F32), 0.0)
    buf_ref[HALO:HALO + ROW_TILE, :] = cur_ref[...].astype(F32)
    buf_ref[HALO + ROW_TILE:, :] = jnp.where(has_next, next_ref[...].astype(F32), 0.0)

    pos = pos0 + lax.broadcasted_iota(jnp.int32, (ROW_TILE, 1), 0)
    for g, win in enumerate(POOL_WINDOWS):
        cols = slice(g * POOL_GROUP, (g + 1) * POOL_GROUP)
        half = win // 2
        acc = buf_ref[HALO - half:HALO - half + ROW_TILE, cols]
        for j in range(1, win):
            acc = acc + buf_ref[HALO - half + j:HALO - half + j + ROW_TILE, cols]
        lo = jnp.maximum(pos - half, 0)
        hi = jnp.minimum(pos - half + win, seg_len)
        mean = acc / (hi - lo).astype(F32)
        pooled = (mean - buf_ref[HALO:HALO + ROW_TILE, cols]).astype(BF16)
        out = jnp.dot(pooled, w_ref[g], preferred_element_type=F32)
        ocols = slice(g * POOL_OUT_GROUP, (g + 1) * POOL_OUT_GROUP)
        o_ref[:, ocols] = (out * ps_ref[:, ocols]).astype(o_ref.dtype)


def _pool(p, w_pool, pool_scale, layer, tile_fn, n_tiles):
    per16 = ROW_TILE // HALO
    last16 = ROWS // HALO - 1
    col = COL_POOL // POOL_WIDTH
    return pl.pallas_call(
        functools.partial(_pool_kernel, tile_fn),
        grid=(n_tiles,),
        in_specs=[pl.BlockSpec((HALO, POOL_WIDTH), lambda i: (jnp.maximum(tile_fn(i) * per16 - 1, 0), col)),
                  pl.BlockSpec((ROW_TILE, POOL_WIDTH), lambda i: (tile_fn(i), col)),
                  pl.BlockSpec((HALO, POOL_WIDTH), lambda i: (jnp.minimum((tile_fn(i) + 1) * per16, last16), col)),
                  pl.BlockSpec((None, N_POOL_GROUPS, POOL_GROUP, POOL_OUT_GROUP), lambda i: (layer, 0, 0, 0)),
                  _vec_spec(layer)],
        out_specs=pl.BlockSpec((ROW_TILE, D_MODEL), lambda i: (i, 0)),
        out_shape=jax.ShapeDtypeStruct((n_tiles * ROW_TILE, D_MODEL), BF16),
        scratch_shapes=[pltpu.VMEM((ROW_TILE + 2 * HALO, POOL_WIDTH), F32)],
        compiler_params=_cparams(("arbitrary",)),
        name="pool",
    )(p, p, p, w_pool, pool_scale)


def _split_bf16(a):
    hi = a.astype(BF16)
    return hi, (a - hi.astype(F32)).astype(BF16)


def _merge_kernel(first, with_router, *refs):
    refs = list(refs)
    wb_ref = refs.pop()
    ga_ref, gb_ref, attn_ref, pool_ref, ba_ref, bb_ref, w_ref = refs[:7]

    @pl.when(pl.program_id(0) == 0)
    def _():
        wb_ref[...] = w_ref[...].astype(wb_ref.dtype)

    del refs[:7]
    if first:
        x = _stream_tile(refs[0], refs[1], pl.program_id(0))
        del refs[:2]
    else:
        x = refs.pop(0)[...]
    gpost_ref, g1_ref, gpre_ref, sh_ref, sc_ref = refs[:5]
    del refs[:5]
    if with_router:
        wr_ref, br_ref, xo_ref, h_ref, lg_ref = refs
    else:
        xo_ref, h_ref = refs

    ga = jax.nn.sigmoid(ga_ref[...].astype(F32) + ba_ref[...])
    gb = jax.nn.sigmoid(gb_ref[...].astype(F32) + bb_ref[...])
    mixed = ga * attn_ref[...].astype(F32) + gb * pool_ref[...].astype(F32)
    y = jnp.dot(mixed.astype(BF16), wb_ref[...], preferred_element_type=F32)
    x = x + g1_ref[...] * _rmsnorm(y, gpost_ref[...])
    xo_ref[...] = x
    h = _rmsnorm(x, gpre_ref[...]) * (1.0 + sc_ref[...]) + sh_ref[...]
    h_ref[...] = h.astype(BF16).astype(h_ref.dtype)
    if with_router:
        h_hi, h_lo = _split_bf16(h)
        w_hi, w_lo = _split_bf16(wr_ref[...])
        dot = functools.partial(jnp.dot, preferred_element_type=F32)
        lg_ref[...] = dot(h_hi, w_hi) + (dot(h_lo, w_hi) + dot(h_hi, w_lo)) + br_ref[...]


def _merge(p, attn, pool, b_gate2, w_out, x_args, g_post, g_pre, mods3, layer, tile_fn, n_tiles,
           router=None):
    first = len(x_args) == 2
    gate_a = COL_GATE // D_MODEL
    act = pl.BlockSpec((ROW_TILE, D_MODEL), lambda i: (i, 0))
    in_specs = [pl.BlockSpec((ROW_TILE, D_MODEL), lambda i: (tile_fn(i), gate_a)),
                pl.BlockSpec((ROW_TILE, D_MODEL), lambda i: (tile_fn(i), gate_a + 1)),
                act, act,
                pl.BlockSpec((None, 1, D_MODEL), lambda i: (layer * 2, 0, 0)),
                pl.BlockSpec((None, 1, D_MODEL), lambda i: (layer * 2 + 1, 0, 0)),
                pl.BlockSpec((None, D_MODEL, D_MODEL), lambda i: (layer, 0, 0), pipeline_mode=pl.Buffered(1))]
    if first:
        in_specs += list(_input_stream_specs())
    else:
        in_specs.append(pl.BlockSpec((ROW_TILE, D_MODEL), lambda i: (tile_fn(i), 0)))
    in_specs += [_vec_spec(layer), _mod_spec(layer, 2, tile_fn), _vec_spec(layer),
                 _mod_spec(layer, 3, tile_fn), _mod_spec(layer, 4, tile_fn)]
    args = [p, p, attn, pool, b_gate2, b_gate2, w_out, *x_args, g_post, mods3, g_pre, mods3, mods3]
    rows = n_tiles * ROW_TILE
    out_specs = [act, act]
    h_dtype = BF16 if router is None else F32
    out_shape = [jax.ShapeDtypeStruct((rows, D_MODEL), F32), jax.ShapeDtypeStruct((rows, D_MODEL), h_dtype)]
    if router is not None:
        in_specs += [pl.BlockSpec((D_MODEL, LANES), lambda i: (0, 0)), pl.BlockSpec((1, LANES), lambda i: (0, 0))]
        args += list(router)
        out_specs.append(pl.BlockSpec((ROW_TILE, LANES), lambda i: (i, 0)))
        out_shape.append(jax.ShapeDtypeStruct((rows, LANES), F32))
    return pl.pallas_call(
        functools.partial(_merge_kernel, first, router is not None),
        grid=(n_tiles,),
        in_specs=in_specs,
        out_specs=out_specs,
        out_shape=out_shape,
        scratch_shapes=[pltpu.VMEM((D_MODEL, D_MODEL), BF16)],
        compiler_params=_cparams(("arbitrary",), VMEM_LIMIT),
        name="merge_out_proj",
    )(*args)


DMA_UNROLL = 8


def _dispatch_kernel(slot_ref, h_ref, xs_in_ref, xs_ref, sem):
    del xs_in_ref

    def start(r, carry):
        for k in range(TOP_K):
            dst = slot_ref[0, r * TOP_K + k]
            pltpu.make_async_copy(h_ref.at[pl.ds(r, 1)], xs_ref.at[pl.ds(dst, 1)], sem).start()
        return carry

    lax.fori_loop(0, ROW_TILE, start, 0, unroll=DMA_UNROLL)
    for _ in range(TOP_K):
        pltpu.make_async_copy(h_ref, xs_ref.at[pl.ds(0, ROW_TILE)], sem).wait()


def _dispatch(slot3, h2):
    xs_init = jnp.zeros((MOE_SEGS * SEG_ROWS, D_MODEL), h2.dtype)
    return pl.pallas_call(
        _dispatch_kernel,
        grid=(N_TILES_LATENT,),
        in_specs=[pl.BlockSpec((None, 1, ROW_TILE * TOP_K), lambda i: (i, 0, 0), memory_space=pltpu.SMEM),
                  pl.BlockSpec((ROW_TILE, D_MODEL), lambda i: (i, 0)),
                  pl.BlockSpec(memory_space=pl.ANY)],
        out_specs=pl.BlockSpec(memory_space=pl.ANY),
        out_shape=jax.ShapeDtypeStruct(xs_init.shape, xs_init.dtype),
        input_output_aliases={2: 0},
        scratch_shapes=[pltpu.SemaphoreType.DMA(())],
        compiler_params=_cparams(("arbitrary",)),
        name="moe_dispatch",
    )(slot3, h2, xs_init)


def _ffn_kernel(se_ref, st_ref, sb_ref, x_ref, wg_ref, wu_ref, wd_ref, o_ref):
    del se_ref, sb_ref
    n_tiles = st_ref[pl.program_id(0)]

    @pl.when(jnp.logical_and(pl.program_id(1) == 0, n_tiles > 0))
    def _():
        o_ref[...] = jnp.zeros_like(o_ref)

    def rows_step(row0, n_rows):
        rows = pl.ds(row0 if isinstance(row0, int) else pl.multiple_of(row0, 16), n_rows)
        x = x_ref[rows, :].astype(BF16)
        g = jnp.dot(x, wg_ref[...].astype(BF16), preferred_element_type=F32)
        u = jnp.dot(x, wu_ref[...].astype(BF16), preferred_element_type=F32)
        a = (g * jax.nn.sigmoid(g)) * u
        o_ref[rows, :] += jnp.dot(a.astype(BF16), wd_ref[...].astype(BF16), preferred_element_type=F32)

    @pl.when(n_tiles == FFN_TILES_PER_SEG)
    def _():
        rows_step(0, SEG_ROWS)

    @pl.when(n_tiles < FFN_TILES_PER_SEG)
    def _():
        def pair(i, carry):
            rows_step(i * (2 * FFN_TILE), 2 * FFN_TILE)
            return carry

        lax.fori_loop(0, lax.shift_right_logical(n_tiles, 1), pair, 0)

        @pl.when((n_tiles & 1) == 1)
        def _():
            rows_step((n_tiles - 1) * FFN_TILE, FFN_TILE)


def _ffn(seg_expert, seg_tiles, xs, wg, wu, wd):
    n_seg = xs.shape[0] // SEG_ROWS
    in_place = xs.dtype == F32
    tf = FFN_CHUNK
    n_chunks = D_FF // tf
    x_mode = dict(pipeline_mode=pl.Buffered(1)) if in_place else {}
    seg_block = jnp.minimum(jnp.arange(n_seg, dtype=jnp.int32), jnp.sum((seg_tiles > 0).astype(jnp.int32)) - 1)

    def chunk(s, f, st):
        return jnp.where(st[s] > 0, f, n_chunks - 1)

    grid_spec = pltpu.PrefetchScalarGridSpec(
        num_scalar_prefetch=3,
        grid=(n_seg, n_chunks),
        in_specs=[pl.BlockSpec((SEG_ROWS, D_MODEL), lambda s, f, se, st, sb: (sb[s], 0), **x_mode),
                  pl.BlockSpec((None, D_MODEL, tf), lambda s, f, se, st, sb: (se[s], 0, chunk(s, f, st))),
                  pl.BlockSpec((None, D_MODEL, tf), lambda s, f, se, st, sb: (se[s], 0, chunk(s, f, st))),
                  pl.BlockSpec((None, tf, D_MODEL), lambda s, f, se, st, sb: (se[s], chunk(s, f, st), 0))],
        out_specs=pl.BlockSpec((SEG_ROWS, D_MODEL), lambda s, f, se, st, sb: (sb[s], 0)),
    )
    return pl.pallas_call(
        _ffn_kernel,
        grid_spec=grid_spec,
        out_shape=jax.ShapeDtypeStruct((n_seg * SEG_ROWS, D_MODEL), F32),
        input_output_aliases={3: 0} if in_place else {},
        compiler_params=_cparams(("arbitrary", "arbitrary"), VMEM_LIMIT),
        name="swiglu_ffn",
    )(seg_expert, seg_tiles, seg_block, xs, wg, wu, wd)


def _post_kernel(y_ref, x_ref, gpost_ref, g2_ref, gpre_ref, sh_ref, sc_ref, xo_ref, h_ref):
    x = x_ref[...] + g2_ref[...] * _rmsnorm(y_ref[...], gpost_ref[...])
    xo_ref[...] = x
    h = _rmsnorm(x, gpre_ref[...]) * (1.0 + sc_ref[...]) + sh_ref[...]
    h_ref[...] = h.astype(h_ref.dtype)


def _post(y, x, g_post, g_pre_next, mods3, layer):
    row = pl.BlockSpec((ROW_TILE, D_MODEL), lambda i: (i, 0))
    return pl.pallas_call(
        _post_kernel,
        grid=(N_TILES_ALL,),
        in_specs=[row, row, _vec_spec(layer), _mod_spec(layer, 5, _tile_all),
                  _vec_spec(layer + 1), _mod_spec(layer + 1, 0, _tile_all), _mod_spec(layer + 1, 1, _tile_all)],
        out_specs=[row, row],
        out_shape=[jax.ShapeDtypeStruct((ROWS, D_MODEL), F32), jax.ShapeDtypeStruct((ROWS, D_MODEL), BF16)],
        compiler_params=_cparams(("arbitrary",)),
        name="post_ffn",
    )(y, x, g_post, mods3, g_pre_next, mods3, mods3)


def _combine_kernel(slot_ref, next_slot_ref, w_ref, x_ref, gpost_ref, g2_ref, ys_ref, o_ref, buf_ref, sem):
    i = pl.program_id(0)
    cur = i % 2

    def gather(slots, b):
        def start(r, carry):
            for k in range(TOP_K):
                src = slots[0, r * TOP_K + k]
                pltpu.make_async_copy(ys_ref.at[pl.ds(src, 1)], buf_ref.at[b, pl.ds(k * ROW_TILE + r, 1)],
                                      sem.at[b]).start()
            return carry
        lax.fori_loop(0, ROW_TILE, start, 0, unroll=DMA_UNROLL)

    @pl.when(i == 0)
    def _():
        gather(slot_ref, 0)

    @pl.when(i + 1 < pl.num_programs(0))
    def _():
        gather(next_slot_ref, 1 - cur)

    pltpu.make_async_copy(ys_ref.at[pl.ds(0, TOP_K * ROW_TILE)], buf_ref.at[cur], sem.at[cur]).wait()
    y = w_ref[:, 0:1] * buf_ref[cur, :ROW_TILE, :] + w_ref[:, 1:2] * buf_ref[cur, ROW_TILE:, :]
    o_ref[...] = x_ref[...] + g2_ref[...] * _rmsnorm(y, gpost_ref[...])


def _combine(slot3, weight, x, g_post, mods3, layer, ys):
    row = pl.BlockSpec((ROW_TILE, D_MODEL), lambda i: (i, 0))
    slots = lambda index: pl.BlockSpec((None, 1, ROW_TILE * TOP_K), index, memory_space=pltpu.SMEM)
    return pl.pallas_call(
        _combine_kernel,
        grid=(N_TILES_LATENT,),
        in_specs=[slots(lambda i: (i, 0, 0)),
                  slots(lambda i: (jnp.minimum(i + 1, N_TILES_LATENT - 1), 0, 0)),
                  pl.BlockSpec((ROW_TILE, TOP_K), lambda i: (i, 0)),
                  row, _vec_spec(layer), _mod_spec(layer, 5, _tile_latent),
                  pl.BlockSpec(memory_space=pl.ANY)],
        out_specs=row,
        out_shape=jax.ShapeDtypeStruct((BATCH * SEQ, D_MODEL), F32),
        scratch_shapes=[pltpu.VMEM((2, TOP_K * ROW_TILE, D_MODEL), F32), pltpu.SemaphoreType.DMA((2,))],
        compiler_params=_cparams(("arbitrary",), VMEM_LIMIT),
        name="moe_combine",
    )(slot3, slot3, weight, x, g_post, mods3, ys)


def _rope_table():
    pos = jnp.arange(SEQ, dtype=jnp.int32)
    row_ids = (pos // GRID_W).astype(F32)
    col_ids = (pos % GRID_W).astype(F32)
    n_freq = QK_ROPE // 4
    inv = ROPE_THETA ** (-jnp.arange(n_freq, dtype=F32) / n_freq)
    ang = jnp.concatenate([row_ids[:, None] * inv, col_ids[:, None] * inv], axis=-1)
    cos, sin = jnp.cos(ang), jnp.sin(ang)
    latent = jnp.concatenate([cos, cos, -sin, sin], axis=-1)
    ctx = jnp.concatenate([jnp.ones((CTX_LEN, QK_ROPE), F32), jnp.zeros((CTX_LEN, QK_ROPE), F32)], axis=-1)
    one = jnp.concatenate([ctx, latent], axis=0)
    return jnp.tile(one, (BATCH, 1))


def _routing(logits):
    probs = jax.nn.softmax(logits, axis=-1)
    top_p, top_i = lax.top_k(probs, TOP_K)
    top_p = top_p / jnp.sum(top_p, axis=-1, keepdims=True)
    onehot = jax.nn.one_hot(top_i.reshape(N_ASSIGN), N_EXPERTS, dtype=jnp.int32)
    csum = jnp.cumsum(onehot, axis=0)
    rank = jnp.sum((csum - onehot) * onehot, axis=1)
    counts = csum[-1]
    n_segs = (counts + SEG_ROWS - 1) // SEG_ROWS
    seg_start = jnp.cumsum(n_segs) - n_segs
    slot = jnp.sum(onehot * seg_start[None, :], axis=1) * SEG_ROWS + rank
    seg_ids = jnp.arange(MOE_SEGS, dtype=jnp.int32)
    used = jnp.sum(n_segs)
    seg_expert = jnp.sum((seg_ids[:, None] >= (seg_start + n_segs)[None, :]).astype(jnp.int32), axis=1)
    last_used_expert = jnp.max(jnp.where(counts > 0, jnp.arange(N_EXPERTS), 0))
    seg_expert = jnp.where(seg_ids < used, jnp.minimum(seg_expert, N_EXPERTS - 1), last_used_expert)
    rows_left = counts[seg_expert] - (seg_ids - seg_start[seg_expert]) * SEG_ROWS
    seg_tiles = jnp.clip((rows_left + FFN_TILE - 1) // FFN_TILE, 0, FFN_TILES_PER_SEG)
    seg_tiles = jnp.where(seg_ids < used, seg_tiles, 0)
    return (seg_expert.astype(jnp.int32), seg_tiles.astype(jnp.int32),
            slot.astype(jnp.int32).reshape(BATCH * SEQ, TOP_K), top_p)


def kernel(x, c, ctx, c_ctx, w_mod, b_mod, g_mix_pre, g_mix_post, g_ffn_pre, g_ffn_post, w_in, b_gate, g_q_lat, g_kv_lat, w_q_up, w_kv_up, w_pool, pool_scale, w_out, w_ff_gate, w_ff_up, w_ff_down, w_router, b_router, w_exp_gate, w_exp_up, w_exp_down):
    x2 = x.reshape(BATCH * SEQ, D_MODEL)
    ctx2 = ctx.reshape(BATCH * CTX_LEN, D_MODEL)

    c8 = jnp.concatenate([c, c_ctx[None, :], jnp.zeros((8 - BATCH - 1, D_MODEL), F32)], axis=0)
    mods = _modulation(c8, w_mod, b_mod)
    mods3 = mods[:, :BATCH + 1].reshape(DEPTH * 3 * 6, 1, D_MODEL)

    as_vec = lambda a: a.reshape(DEPTH, 1, a.shape[-1])
    g_mix_pre, g_mix_post, g_ffn_pre, g_ffn_post = map(as_vec, (g_mix_pre, g_mix_post, g_ffn_pre, g_ffn_post))
    g_q_lat, g_kv_lat, pool_scale = map(as_vec, (g_q_lat, g_kv_lat, pool_scale))
    b_gate2 = b_gate.reshape(DEPTH * 2, 1, D_MODEL)
    cs = _rope_table()

    w_in_t = jnp.swapaxes(w_in, 1, 2)
    w_pool = w_pool.astype(BF16)

    h = _norm_mod(x2, ctx2, g_mix_pre, mods3, 0)
    xr = None
    out = None
    for l in range(DEPTH):
        last = l == DEPTH - 1
        p = _in_proj(h, w_in_t, l)
        q = _q_up(p, g_q_lat, w_q_up, cs, l)
        k, v = _kv_up(p, g_kv_lat, w_kv_up, h, w_in_t, cs, l)

        attn = _attention(q, k, v, with_ctx=not last).reshape(-1, D_MODEL)
        tile_fn, n_tiles = (_tile_latent, N_TILES_LATENT) if last else (_tile_all, N_TILES_ALL)
        pool = _pool(p, w_pool, pool_scale, l, tile_fn, n_tiles)
        x_args = (x2, ctx2) if l == 0 else (xr,)
        merge = functools.partial(_merge, p, attn, pool, b_gate2, w_out, x_args, g_mix_post,
                                  g_ffn_pre, mods3, l, tile_fn, n_tiles)

        if not last:
            xr, h2 = merge()
            n_seg = ROWS // SEG_ROWS
            seg_expert = jnp.zeros((n_seg,), jnp.int32)
            seg_tiles = jnp.full((n_seg,), FFN_TILES_PER_SEG, jnp.int32)
            y = _ffn(seg_expert, seg_tiles, h2, w_ff_gate, w_ff_up, w_ff_down)
            xr, h = _post(y, xr, g_ffn_post, g_mix_pre, mods3, l)
        else:
            w_r = jnp.pad(w_router[0], ((0, 0), (0, LANES - N_EXPERTS)))
            b_r = jnp.pad(b_router[0], (0, LANES - N_EXPERTS)).reshape(1, LANES)
            xl, h2, logits = merge(router=(w_r, b_r))
            seg_expert, seg_tiles, slot, weight = _routing(logits[:, :N_EXPERTS])
            slot3 = slot.reshape(N_TILES_LATENT, 1, ROW_TILE * TOP_K)
            xs = _dispatch(slot3, h2)
            ys = _ffn(seg_expert, seg_tiles, xs, w_exp_gate.reshape(N_EXPERTS, D_MODEL, D_FF),
                      w_exp_up.reshape(N_EXPERTS, D_MODEL, D_FF), w_exp_down.reshape(N_EXPERTS, D_FF, D_MODEL))
            out = _combine(slot3, weight, xl, g_ffn_post, mods3, l, ys)
    return out.reshape(BATCH, SEQ, D_MODEL)
```

```python
import functools
import math

import jax
import jax.numpy as jnp
from jax import lax
from jax.experimental import pallas as pl
from jax.experimental.pallas import tpu as pltpu

F32 = jnp.float32
BF16 = jnp.bfloat16

D_MODEL = 2048
BATCH = 2
SEQ = 4096
DEPTH = 2
GRID_W = 64
CTX_LEN = 256
EPS = 1e-6

QK_NOPE = 128
QK_ROPE = 64
V_DIM = 128
N_HEADS = 16
Q_LORA = 512
KV_LORA = 512
ROPE_THETA = 10000.0
ATTN_SCALE = (QK_NOPE + QK_ROPE) ** -0.5
Q_PRESCALE = ATTN_SCALE * math.log2(math.e)

POOL_WINDOWS = (2, 4, 8, 16)
N_POOL_GROUPS = 4
POOL_WIDTH = 1024
POOL_GROUP = 256
POOL_OUT_GROUP = 512

OFF_KR = 1024
LANES = 128

D_FF = 7168
N_EXPERTS = 8
TOP_K = 2

T_ROWS = CTX_LEN + SEQ
ROWS = BATCH * T_ROWS
ROW_TILE = 256
TILES_PER_BATCH = T_ROWS // ROW_TILE
LATENT_TILES = SEQ // ROW_TILE
N_TILES_ALL = BATCH * TILES_PER_BATCH
N_TILES_LATENT = BATCH * LATENT_TILES
HALO = 16
K_WIDTH = 256
HEADS_PER_STEP = 8

COL_Q, COL_KV, COL_POOL, COL_GATE = 0, 512, 1024, 2048

FFN_TILE = 272
FFN_TILES_PER_SEG = 4
SEG_ROWS = FFN_TILE * FFN_TILES_PER_SEG
FFN_CHUNK = 512
N_ASSIGN = BATCH * SEQ * TOP_K
MOE_SEGS = N_ASSIGN // SEG_ROWS + N_EXPERTS

VMEM_LIMIT = 56 * 1024 * 1024


def _cparams(sem, vmem=None):
    return pltpu.CompilerParams(dimension_semantics=sem, vmem_limit_bytes=vmem)


def _tile_all(i):
    return i


def _tile_latent(i):
    return (i // LATENT_TILES) * TILES_PER_BATCH + 1 + i % LATENT_TILES


def _group_of_tile(t):
    return jnp.where(t % TILES_PER_BATCH == 0, BATCH, t // TILES_PER_BATCH)


def _mod_spec(layer, chunk, tile_fn):
    def index(i):
        return ((layer * 3 + _group_of_tile(tile_fn(i))) * 6 + chunk, 0, 0)
    return pl.BlockSpec((None, 1, D_MODEL), index)


def _vec_spec(layer, width=D_MODEL):
    return pl.BlockSpec((None, 1, width), lambda i: (layer, 0, 0))


def _input_stream_specs():
    def x_index(t):
        return ((t // TILES_PER_BATCH) * LATENT_TILES + jnp.maximum(t % TILES_PER_BATCH - 1, 0), 0)

    def ctx_index(t):
        return (t // TILES_PER_BATCH, 0)

    return (pl.BlockSpec((ROW_TILE, D_MODEL), x_index), pl.BlockSpec((CTX_LEN, D_MODEL), ctx_index))


def _stream_tile(x_ref, ctx_ref, tile):
    return jnp.where(tile % TILES_PER_BATCH == 0, ctx_ref[...], x_ref[...])


def _rmsnorm(x, g):
    return x * lax.rsqrt(jnp.mean(x * x, axis=-1, keepdims=True) + EPS) * g


def _mod_kernel(c_ref, w_ref, b_ref, o_ref):
    c = c_ref[...]
    a = c * jax.nn.sigmoid(c)
    o_ref[...] = jnp.dot(a, w_ref[...], preferred_element_type=F32) + b_ref[...]


def _modulation(c8, w_mod, b_mod):
    n = w_mod.shape[2]
    tn = 1024
    return pl.pallas_call(
        _mod_kernel,
        grid=(DEPTH, n // tn),
        in_specs=[pl.BlockSpec((8, D_MODEL), lambda l, j: (0, 0)),
                  pl.BlockSpec((None, D_MODEL, tn), lambda l, j: (l, 0, j)),
                  pl.BlockSpec((None, 1, tn), lambda l, j: (l, 0, j))],
        out_specs=pl.BlockSpec((None, 8, tn), lambda l, j: (l, 0, j)),
        out_shape=jax.ShapeDtypeStruct((DEPTH, 8, n), F32),
        compiler_params=_cparams(("arbitrary", "arbitrary"), VMEM_LIMIT),
        name="modulation",
    )(c8, w_mod, b_mod.reshape(DEPTH, 1, n))


def _norm_mod_kernel(x_ref, ctx_ref, g_ref, sh_ref, sc_ref, o_ref):
    y = _rmsnorm(_stream_tile(x_ref, ctx_ref, pl.program_id(0)), g_ref[...])
    o_ref[...] = (y * (1.0 + sc_ref[...]) + sh_ref[...]).astype(o_ref.dtype)


def _norm_mod(x2, ctx2, g, mods3, layer):
    x_spec, ctx_spec = _input_stream_specs()
    return pl.pallas_call(
        _norm_mod_kernel,
        grid=(N_TILES_ALL,),
        in_specs=[x_spec, ctx_spec, _vec_spec(layer),
                  _mod_spec(layer, 0, _tile_all), _mod_spec(layer, 1, _tile_all)],
        out_specs=pl.BlockSpec((ROW_TILE, D_MODEL), lambda i: (i, 0)),
        out_shape=jax.ShapeDtypeStruct((ROWS, D_MODEL), BF16),
        compiler_params=_cparams(("arbitrary",)),
        name="norm_mod",
    )(x2, ctx2, g, mods3, mods3)


_CONTRACT_LAST = (((1,), (1,)), ((), ()))
IN_TILE = 1024
IN_ROWS = 1088
MAIN_WIDTH = Q_LORA + KV_LORA + POOL_WIDTH + 2 * D_MODEL


def _in_proj_kernel(layer, a_ref, wt_ref, o_ref, wbuf_ref, wcast_ref, sem):
    j = pl.program_id(0)

    def fetch(tile):
        row0 = tile * IN_TILE + jnp.where(tile * IN_TILE >= OFF_KR, QK_ROPE, 0)
        rows = pl.ds(pl.multiple_of(row0, QK_ROPE), IN_TILE)
        slot = tile % 2
        return pltpu.make_async_copy(wt_ref.at[layer, rows], wbuf_ref.at[slot], sem.at[slot])

    @pl.when(pl.program_id(1) == 0)
    def _():
        @pl.when(j == 0)
        def _():
            fetch(j).start()

        @pl.when(j + 1 < pl.num_programs(0))
        def _():
            fetch(j + 1).start()

        fetch(j).wait()
        wcast_ref[...] = wbuf_ref[j % 2].astype(wcast_ref.dtype)

    y = lax.dot_general(a_ref[...], wcast_ref[...], _CONTRACT_LAST, preferred_element_type=F32)
    o_ref[...] = y.astype(o_ref.dtype)


def _in_proj(a, w_in_t, layer):
    m, k = a.shape
    assert OFF_KR % IN_TILE == 0 and MAIN_WIDTH % IN_TILE == 0 and m % IN_ROWS == 0
    return pl.pallas_call(
        functools.partial(_in_proj_kernel, layer),
        grid=(MAIN_WIDTH // IN_TILE, m // IN_ROWS),
        in_specs=[pl.BlockSpec((IN_ROWS, k), lambda j, i: (i, 0)),
                  pl.BlockSpec(memory_space=pl.ANY)],
        out_specs=pl.BlockSpec((IN_ROWS, IN_TILE), lambda j, i: (i, j)),
        out_shape=jax.ShapeDtypeStruct((m, MAIN_WIDTH), BF16),
        scratch_shapes=[pltpu.VMEM((2, IN_TILE, k), F32), pltpu.VMEM((IN_TILE, k), BF16),
                        pltpu.SemaphoreType.DMA((2,))],
        compiler_params=_cparams(("arbitrary", "arbitrary"), VMEM_LIMIT),
        name="in_proj",
    )(a, w_in_t)


def _rope_rotate(t, cs):
    lane = lax.broadcasted_iota(jnp.int32, t.shape, 1)
    s_tab = pltpu.roll(cs, QK_ROPE, 1)
    return jnp.where(lane < QK_ROPE, t * cs + _swap_rope_halves(t) * s_tab, 0.0)


def _swap_rope_halves(t):
    half = QK_ROPE // 2
    lane = lax.broadcasted_iota(jnp.int32, t.shape, 1)
    return jnp.where(lane < half, pltpu.roll(t, LANES - half, 1), pltpu.roll(t, half, 1))


def _q_up_kernel(p_ref, g_ref, w_ref, cs_ref, q_ref, w2_ref):
    head = QK_NOPE + QK_ROPE

    @pl.when(pl.program_id(0) == 0)
    def _():
        lane = lax.broadcasted_iota(jnp.int32, (Q_LORA, LANES), 1)
        w = w_ref[...]
        for h in range(N_HEADS):
            base = h * head
            w2_ref[:, h * K_WIDTH:h * K_WIDTH + QK_NOPE] = w[:, base:base + QK_NOPE].astype(w2_ref.dtype)
            tail = pltpu.roll(w[:, base + head - LANES:base + head], QK_ROPE, 1)
            both = jnp.where(lane < QK_ROPE, tail, pltpu.roll(_swap_rope_halves(tail), QK_ROPE, 1))
            w2_ref[:, h * K_WIDTH + QK_NOPE:(h + 1) * K_WIDTH] = both.astype(w2_ref.dtype)

    n = _rmsnorm(p_ref[...].astype(F32), g_ref[...]).astype(BF16)
    y = jnp.dot(n, w2_ref[...], preferred_element_type=F32) * Q_PRESCALE
    cs = cs_ref[...]
    lane = lax.broadcasted_iota(jnp.int32, (ROW_TILE, LANES), 1)
    for h in range(N_HEADS):
        yh = y[:, h * K_WIDTH:(h + 1) * K_WIDTH]
        q_ref[h, :, :QK_NOPE] = yh[:, :QK_NOPE].astype(q_ref.dtype)
        t = yh[:, QK_NOPE:] * cs
        q_ref[h, :, QK_NOPE:] = jnp.where(lane < QK_ROPE, t + pltpu.roll(t, QK_ROPE, 1), 0.0).astype(q_ref.dtype)


def _head_index(i):
    return (i // TILES_PER_BATCH, 0, i % TILES_PER_BATCH, 0)


def _q_up(p, g_q, w_q_up, cs, layer):
    return pl.pallas_call(
        _q_up_kernel,
        grid=(N_TILES_ALL,),
        in_specs=[pl.BlockSpec((ROW_TILE, Q_LORA), lambda i: (i, COL_Q // Q_LORA)),
                  _vec_spec(layer, Q_LORA),
                  pl.BlockSpec((None, Q_LORA, N_HEADS * (QK_NOPE + QK_ROPE)), lambda i: (layer, 0, 0)),
                  pl.BlockSpec((ROW_TILE, LANES), lambda i: (i, 0))],
        out_specs=pl.BlockSpec((None, N_HEADS, ROW_TILE, K_WIDTH), _head_index),
        out_shape=jax.ShapeDtypeStruct((BATCH, N_HEADS, T_ROWS, K_WIDTH), BF16),
        scratch_shapes=[pltpu.VMEM((Q_LORA, N_HEADS * K_WIDTH), BF16)],
        compiler_params=_cparams(("arbitrary",), VMEM_LIMIT),
        name="q_up",
    )(p, g_q, w_q_up, cs)


def _kv_up_kernel(p_ref, g_ref, w_ref, h_ref, wkr_ref, cs_ref, k_ref, v_ref, wb_ref):
    @pl.when(pl.program_id(0) == 0)
    def _():
        wb_ref[...] = w_ref[...].astype(wb_ref.dtype)

    n = _rmsnorm(p_ref[...].astype(F32), g_ref[...]).astype(BF16)
    y = jnp.dot(n, wb_ref[...], preferred_element_type=F32)
    kr = lax.dot_general(h_ref[...], wkr_ref[...].astype(BF16), _CONTRACT_LAST, preferred_element_type=F32)
    kr = _rope_rotate(kr, cs_ref[...]).astype(k_ref.dtype)
    for h in range(N_HEADS):
        base = h * (QK_NOPE + V_DIM)
        k_ref[h, :, :QK_NOPE] = y[:, base:base + QK_NOPE].astype(k_ref.dtype)
        k_ref[h, :, QK_NOPE:] = kr
        v_ref[h] = y[:, base + QK_NOPE:base + QK_NOPE + V_DIM].astype(v_ref.dtype)


def _kv_up(p, g_kv, w_kv, h, w_in_t, cs, layer):
    return pl.pallas_call(
        _kv_up_kernel,
        grid=(N_TILES_ALL,),
        in_specs=[pl.BlockSpec((ROW_TILE, KV_LORA), lambda i: (i, COL_KV // KV_LORA)),
                  _vec_spec(layer, KV_LORA),
                  pl.BlockSpec((None, KV_LORA, N_HEADS * (QK_NOPE + V_DIM)), lambda i: (layer, 0, 0)),
                  pl.BlockSpec((ROW_TILE, D_MODEL), lambda i: (i, 0)),
                  pl.BlockSpec((None, LANES, D_MODEL), lambda i: (layer, OFF_KR // LANES, 0)),
                  pl.BlockSpec((ROW_TILE, LANES), lambda i: (i, 0))],
        out_specs=[pl.BlockSpec((None, N_HEADS, ROW_TILE, K_WIDTH), _head_index),
                   pl.BlockSpec((None, N_HEADS, ROW_TILE, V_DIM), _head_index)],
        out_shape=[jax.ShapeDtypeStruct((BATCH, N_HEADS, T_ROWS, K_WIDTH), BF16),
                   jax.ShapeDtypeStruct((BATCH, N_HEADS, T_ROWS, V_DIM), BF16)],
        scratch_shapes=[pltpu.VMEM((KV_LORA, N_HEADS * (QK_NOPE + V_DIM)), BF16)],
        compiler_params=_cparams(("arbitrary",), VMEM_LIMIT),
        name="kv_up",
    )(p, g_kv, w_kv, h, w_in_t, cs)


def _attend(q, k, v):
    s = lax.dot_general(q, k, (((1,), (1,)), ((), ())), preferred_element_type=F32)
    m = jnp.max(s, axis=-1, keepdims=True)
    e = jnp.exp2(s - m)
    denom = jnp.sum(e, axis=-1, keepdims=True)
    o = jnp.dot(e.astype(BF16), v, preferred_element_type=F32)
    return o / denom


def _attention_kernel(with_ctx, q_ref, k_ref, v_ref, o_ref):
    def run(n_keys):
        for h in range(HEADS_PER_STEP):
            o = _attend(q_ref[h], k_ref[h, :n_keys, :], v_ref[h, :n_keys, :])
            o_ref[:, h * V_DIM:(h + 1) * V_DIM] = o.astype(o_ref.dtype)

    if not with_ctx:
        run(T_ROWS)
        return

    pl.when(pl.program_id(2) == 0)(functools.partial(run, CTX_LEN))
    pl.when(pl.program_id(2) > 0)(functools.partial(run, T_ROWS))


def _attention(q, k, v, with_ctx):
    q0 = 0 if with_ctx else 1
    n_q = TILES_PER_BATCH - q0
    hps = HEADS_PER_STEP
    return pl.pallas_call(
        functools.partial(_attention_kernel, with_ctx),
        grid=(BATCH, N_HEADS // hps, n_q),
        in_specs=[pl.BlockSpec((None, hps, ROW_TILE, K_WIDTH), lambda b, h, i: (b, h, i + q0, 0)),
                  pl.BlockSpec((None, hps, T_ROWS, K_WIDTH), lambda b, h, i: (b, h, 0, 0),
                               pipeline_mode=pl.Buffered(1)),
                  pl.BlockSpec((None, hps, T_ROWS, V_DIM), lambda b, h, i: (b, h, 0, 0))],
        out_specs=pl.BlockSpec((None, ROW_TILE, hps * V_DIM), lambda b, h, i: (b, i, h)),
        out_shape=jax.ShapeDtypeStruct((BATCH, n_q * ROW_TILE, D_MODEL), BF16),
        compiler_params=_cparams(("arbitrary", "arbitrary", "arbitrary"), VMEM_LIMIT),
        name="attention",
    )(q, k, v)


def _pool_kernel(tile_fn, prev_ref, cur_ref, next_ref, w_ref, ps_ref, o_ref, buf_ref):
    t = tile_fn(pl.program_id(0)) % TILES_PER_BATCH
    is_ctx = t == 0
    seg_len = jnp.where(is_ctx, CTX_LEN, SEQ)
    pos0 = jnp.where(is_ctx, 0, (t - 1) * ROW_TILE)
    has_prev = pos0 > 0
    has_next = pos0 + ROW_TILE < seg_len
    buf_ref[0:HALO, :] = jnp.where(has_prev, prev_ref[...].astype(F32), 0.0)
    buf_ref[HALO:HALO + ROW_TILE, :] = cur_ref[...].astype(F32)
    buf_ref[HALO + ROW_TILE:, :] = jnp.where(has_next, next_ref[...].astype(F32), 0.0)

    pos = pos0 + lax.broadcasted_iota(jnp.int32, (ROW_TILE, 1), 0)
    for g, win in enumerate(POOL_WINDOWS):
        cols = slice(g * POOL_GROUP, (g + 1) * POOL_GROUP)
        half = win // 2
        acc = buf_ref[HALO - half:HALO - half + ROW_TILE, cols]
        for j in range(1, win):
            acc = acc + buf_ref[HALO - half + j:HALO - half + j + ROW_TILE, cols]
        lo = jnp.maximum(pos - half, 0)
        hi = jnp.minimum(pos - half + win, seg_len)
        mean = acc / (hi - lo).astype(F32)
        pooled = (mean - buf_ref[HALO:HALO + ROW_TILE, cols]).astype(BF16)
        out = jnp.dot(pooled, w_ref[g], preferred_element_type=F32)
        ocols = slice(g * POOL_OUT_GROUP, (g + 1) * POOL_OUT_GROUP)
        o_ref[:, ocols] = (out * ps_ref[:, ocols]).astype(o_ref.dtype)


def _pool(p, w_pool, pool_scale, layer, tile_fn, n_tiles):
    per16 = ROW_TILE // HALO
    last16 = ROWS // HALO - 1
    col = COL_POOL // POOL_WIDTH
    return pl.pallas_call(
        functools.partial(_pool_kernel, tile_fn),
        grid=(n_tiles,),
        in_specs=[pl.BlockSpec((HALO, POOL_WIDTH), lambda i: (jnp.maximum(tile_fn(i) * per16 - 1, 0), col)),
                  pl.BlockSpec((ROW_TILE, POOL_WIDTH), lambda i: (tile_fn(i), col)),
                  pl.BlockSpec((HALO, POOL_WIDTH), lambda i: (jnp.minimum((tile_fn(i) + 1) * per16, last16), col)),
                  pl.BlockSpec((None, N_POOL_GROUPS, POOL_GROUP, POOL_OUT_GROUP), lambda i: (layer, 0, 0, 0)),
                  _vec_spec(layer)],
        out_specs=pl.BlockSpec((ROW_TILE, D_MODEL), lambda i: (i, 0)),
        out_shape=jax.ShapeDtypeStruct((n_tiles * ROW_TILE, D_MODEL), BF16),
        scratch_shapes=[pltpu.VMEM((ROW_TILE + 2 * HALO, POOL_WIDTH), F32)],
        compiler_params=_cparams(("arbitrary",)),
        name="pool",
    )(p, p, p, w_pool, pool_scale)


def _split_bf16(a):
    hi = a.astype(BF16)
    return hi, (a - hi.astype(F32)).astype(BF16)


def _merge_kernel(first, with_router, *refs):
    refs = list(refs)
    wb_ref = refs.pop()
    ga_ref, gb_ref, attn_ref, pool_ref, ba_ref, bb_ref, w_ref = refs[:7]

    @pl.when(pl.program_id(0) == 0)
    def _():
        wb_ref[...] = w_ref[...].astype(wb_ref.dtype)

    del refs[:7]
    if first:
        x = _stream_tile(refs[0], refs[1], pl.program_id(0))
        del refs[:2]
    else:
        x = refs.pop(0)[...]
    gpost_ref, g1_ref, gpre_ref, sh_ref, sc_ref = refs[:5]
    del refs[:5]
    if with_router:
        wr_ref, br_ref, xo_ref, h_ref, lg_ref = refs
    else:
        xo_ref, h_ref = refs

    ga = jax.nn.sigmoid(ga_ref[...].astype(F32) + ba_ref[...])
    gb = jax.nn.sigmoid(gb_ref[...].astype(F32) + bb_ref[...])
    mixed = ga * attn_ref[...].astype(F32) + gb * pool_ref[...].astype(F32)
    y = jnp.dot(mixed.astype(BF16), wb_ref[...], preferred_element_type=F32)
    x = x + g1_ref[...] * _rmsnorm(y, gpost_ref[...])
    xo_ref[...] = x
    h = _rmsnorm(x, gpre_ref[...]) * (1.0 + sc_ref[...]) + sh_ref[...]
    h_ref[...] = h.astype(BF16).astype(h_ref.dtype)
    if with_router:
        h_hi, h_lo = _split_bf16(h)
        w_hi, w_lo = _split_bf16(wr_ref[...])
        dot = functools.partial(jnp.dot, preferred_element_type=F32)
        lg_ref[...] = dot(h_hi, w_hi) + (dot(h_lo, w_hi) + dot(h_hi, w_lo)) + br_ref[...]


def _merge(p, attn, pool, b_gate2, w_out, x_args, g_post, g_pre, mods3, layer, tile_fn, n_tiles,
           router=None):
    first = len(x_args) == 2
    gate_a = COL_GATE // D_MODEL
    act = pl.BlockSpec((ROW_TILE, D_MODEL), lambda i: (i, 0))
    in_specs = [pl.BlockSpec((ROW_TILE, D_MODEL), lambda i: (tile_fn(i), gate_a)),
                pl.BlockSpec((ROW_TILE, D_MODEL), lambda i: (tile_fn(i), gate_a + 1)),
                act, act,
                pl.BlockSpec((None, 1, D_MODEL), lambda i: (layer * 2, 0, 0)),
                pl.BlockSpec((None, 1, D_MODEL), lambda i: (layer * 2 + 1, 0, 0)),
                pl.BlockSpec((None, D_MODEL, D_MODEL), lambda i: (layer, 0, 0), pipeline_mode=pl.Buffered(1))]
    if first:
        in_specs += list(_input_stream_specs())
    else:
        in_specs.append(pl.BlockSpec((ROW_TILE, D_MODEL), lambda i: (tile_fn(i), 0)))
    in_specs += [_vec_spec(layer), _mod_spec(layer, 2, tile_fn), _vec_spec(layer),
                 _mod_spec(layer, 3, tile_fn), _mod_spec(layer, 4, tile_fn)]
    args = [p, p, attn, pool, b_gate2, b_gate2, w_out, *x_args, g_post, mods3, g_pre, mods3, mods3]
    rows = n_tiles * ROW_TILE
    out_specs = [act, act]
    h_dtype = BF16 if router is None else F32
    out_shape = [jax.ShapeDtypeStruct((rows, D_MODEL), F32), jax.ShapeDtypeStruct((rows, D_MODEL), h_dtype)]
    if router is not None:
        in_specs += [pl.BlockSpec((D_MODEL, LANES), lambda i: (0, 0)), pl.BlockSpec((1, LANES), lambda i: (0, 0))]
        args += list(router)
        out_specs.append(pl.BlockSpec((ROW_TILE, LANES), lambda i: (i, 0)))
        out_shape.append(jax.ShapeDtypeStruct((rows, LANES), F32))
    return pl.pallas_call(
        functools.partial(_merge_kernel, first, router is not None),
        grid=(n_tiles,),
        in_specs=in_specs,
        out_specs=out_specs,
        out_shape=out_shape,
        scratch_shapes=[pltpu.VMEM((D_MODEL, D_MODEL), BF16)],
        compiler_params=_cparams(("arbitrary",), VMEM_LIMIT),
        name="merge_out_proj",
    )(*args)


DMA_UNROLL = 8


def _dispatch_kernel(slot_ref, h_ref, xs_in_ref, xs_ref, sem):
    del xs_in_ref

    def start(r, carry):
        for k in range(TOP_K):
            dst = slot_ref[0, r * TOP_K + k]
            pltpu.make_async_copy(h_ref.at[pl.ds(r, 1)], xs_ref.at[pl.ds(dst, 1)], sem).start()
        return carry

    lax.fori_loop(0, ROW_TILE, start, 0, unroll=DMA_UNROLL)
    for _ in range(TOP_K):
        pltpu.make_async_copy(h_ref, xs_ref.at[pl.ds(0, ROW_TILE)], sem).wait()


def _dispatch(slot3, h2):
    xs_init = jnp.zeros((MOE_SEGS * SEG_ROWS, D_MODEL), h2.dtype)
    return pl.pallas_call(
        _dispatch_kernel,
        grid=(N_TILES_LATENT,),
        in_specs=[pl.BlockSpec((None, 1, ROW_TILE * TOP_K), lambda i: (i, 0, 0), memory_space=pltpu.SMEM),
                  pl.BlockSpec((ROW_TILE, D_MODEL), lambda i: (i, 0)),
                  pl.BlockSpec(memory_space=pl.ANY)],
        out_specs=pl.BlockSpec(memory_space=pl.ANY),
        out_shape=jax.ShapeDtypeStruct(xs_init.shape, xs_init.dtype),
        input_output_aliases={2: 0},
        scratch_shapes=[pltpu.SemaphoreType.DMA(())],
        compiler_params=_cparams(("arbitrary",)),
        name="moe_dispatch",
    )(slot3, h2, xs_init)


def _ffn_kernel(se_ref, st_ref, sb_ref, x_ref, wg_ref, wu_ref, wd_ref, o_ref):
    del se_ref, sb_ref
    n_tiles = st_ref[pl.program_id(0)]

    @pl.when(jnp.logical_and(pl.program_id(1) == 0, n_tiles > 0))
    def _():
        o_ref[...] = jnp.zeros_like(o_ref)

    def rows_step(row0, n_rows):
        rows = pl.ds(row0 if isinstance(row0, int) else pl.multiple_of(row0, 16), n_rows)
        x = x_ref[rows, :].astype(BF16)
        g = jnp.dot(x, wg_ref[...].astype(BF16), preferred_element_type=F32)
        u = jnp.dot(x, wu_ref[...].astype(BF16), preferred_element_type=F32)
        a = (g * jax.nn.sigmoid(g)) * u
        o_ref[rows, :] += jnp.dot(a.astype(BF16), wd_ref[...].astype(BF16), preferred_element_type=F32)

    @pl.when(n_tiles == FFN_TILES_PER_SEG)
    def _():
        rows_step(0, SEG_ROWS)

    @pl.when(n_tiles < FFN_TILES_PER_SEG)
    def _():
        def pair(i, carry):
            rows_step(i * (2 * FFN_TILE), 2 * FFN_TILE)
            return carry

        lax.fori_loop(0, lax.shift_right_logical(n_tiles, 1), pair, 0)

        @pl.when((n_tiles & 1) == 1)
        def _():
            rows_step((n_tiles - 1) * FFN_TILE, FFN_TILE)


def _ffn(seg_expert, seg_tiles, xs, wg, wu, wd):
    n_seg = xs.shape[0] // SEG_ROWS
    in_place = xs.dtype == F32
    tf = FFN_CHUNK
    n_chunks = D_FF // tf
    x_mode = dict(pipeline_mode=pl.Buffered(1)) if in_place else {}
    seg_block = jnp.minimum(jnp.arange(n_seg, dtype=jnp.int32), jnp.sum((seg_tiles > 0).astype(jnp.int32)) - 1)

    def chunk(s, f, st):
        return jnp.where(st[s] > 0, f, n_chunks - 1)

    grid_spec = pltpu.PrefetchScalarGridSpec(
        num_scalar_prefetch=3,
        grid=(n_seg, n_chunks),
        in_specs=[pl.BlockSpec((SEG_ROWS, D_MODEL), lambda s, f, se, st, sb: (sb[s], 0), **x_mode),
                  pl.BlockSpec((None, D_MODEL, tf), lambda s, f, se, st, sb: (se[s], 0, chunk(s, f, st))),
                  pl.BlockSpec((None, D_MODEL, tf), lambda s, f, se, st, sb: (se[s], 0, chunk(s, f, st))),
                  pl.BlockSpec((None, tf, D_MODEL), lambda s, f, se, st, sb: (se[s], chunk(s, f, st), 0))],
        out_specs=pl.BlockSpec((SEG_ROWS, D_MODEL), lambda s, f, se, st, sb: (sb[s], 0)),
    )
    return pl.pallas_call(
        _ffn_kernel,
        grid_spec=grid_spec,
        out_shape=jax.ShapeDtypeStruct((n_seg * SEG_ROWS, D_MODEL), F32),
        input_output_aliases={3: 0} if in_place else {},
        compiler_params=_cparams(("arbitrary", "arbitrary"), VMEM_LIMIT),
        name="swiglu_ffn",
    )(seg_expert, seg_tiles, seg_block, xs, wg, wu, wd)


def _post_kernel(y_ref, x_ref, gpost_ref, g2_ref, gpre_ref, sh_ref, sc_ref, xo_ref, h_ref):
    x = x_ref[...] + g2_ref[...] * _rmsnorm(y_ref[...], gpost_ref[...])
    xo_ref[...] = x
    h = _rmsnorm(x, gpre_ref[...]) * (1.0 + sc_ref[...]) + sh_ref[...]
    h_ref[...] = h.astype(h_ref.dtype)


def _post(y, x, g_post, g_pre_next, mods3, layer):
    row = pl.BlockSpec((ROW_TILE, D_MODEL), lambda i: (i, 0))
    return pl.pallas_call(
        _post_kernel,
        grid=(N_TILES_ALL,),
        in_specs=[row, row, _vec_spec(layer), _mod_spec(layer, 5, _tile_all),
                  _vec_spec(layer + 1), _mod_spec(layer + 1, 0, _tile_all), _mod_spec(layer + 1, 1, _tile_all)],
        out_specs=[row, row],
        out_shape=[jax.ShapeDtypeStruct((ROWS, D_MODEL), F32), jax.ShapeDtypeStruct((ROWS, D_MODEL), BF16)],
        compiler_params=_cparams(("arbitrary",)),
        name="post_ffn",
    )(y, x, g_post, mods3, g_pre_next, mods3, mods3)


def _combine_kernel(slot_ref, next_slot_ref, w_ref, x_ref, gpost_ref, g2_ref, ys_ref, o_ref, buf_ref, sem):
    i = pl.program_id(0)
    cur = i % 2

    def gather(slots, b):
        def start(r, carry):
            for k in range(TOP_K):
                src = slots[0, r * TOP_K + k]
                pltpu.make_async_copy(ys_ref.at[pl.ds(src, 1)], buf_ref.at[b, pl.ds(k * ROW_TILE + r, 1)],
                                      sem.at[b]).start()
            return carry
        lax.fori_loop(0, ROW_TILE, start, 0, unroll=DMA_UNROLL)

    @pl.when(i == 0)
    def _():
        gather(slot_ref, 0)

    @pl.when(i + 1 < pl.num_programs(0))
    def _():
        gather(next_slot_ref, 1 - cur)

    pltpu.make_async_copy(ys_ref.at[pl.ds(0, TOP_K * ROW_TILE)], buf_ref.at[cur], sem.at[cur]).wait()
    y = w_ref[:, 0:1] * buf_ref[cur, :ROW_TILE, :] + w_ref[:, 1:2] * buf_ref[cur, ROW_TILE:, :]
    o_ref[...] = x_ref[...] + g2_ref[...] * _rmsnorm(y, gpost_ref[...])


def _combine(slot3, weight, x, g_post, mods3, layer, ys):
    row = pl.BlockSpec((ROW_TILE, D_MODEL), lambda i: (i, 0))
    slots = lambda index: pl.BlockSpec((None, 1, ROW_TILE * TOP_K), index, memory_space=pltpu.SMEM)
    return pl.pallas_call(
        _combine_kernel,
        grid=(N_TILES_LATENT,),
        in_specs=[slots(lambda i: (i, 0, 0)),
                  slots(lambda i: (jnp.minimum(i + 1, N_TILES_LATENT - 1), 0, 0)),
                  pl.BlockSpec((ROW_TILE, TOP_K), lambda i: (i, 0)),
                  row, _vec_spec(layer), _mod_spec(layer, 5, _tile_latent),
                  pl.BlockSpec(memory_space=pl.ANY)],
        out_specs=row,
        out_shape=jax.ShapeDtypeStruct((BATCH * SEQ, D_MODEL), F32),
        scratch_shapes=[pltpu.VMEM((2, TOP_K * ROW_TILE, D_MODEL), F32), pltpu.SemaphoreType.DMA((2,))],
        compiler_params=_cparams(("arbitrary",), VMEM_LIMIT),
        name="moe_combine",
    )(slot3, slot3, weight, x, g_post, mods3, ys)


def _rope_table():
    pos = jnp.arange(SEQ, dtype=jnp.int32)
    row_ids = (pos // GRID_W).astype(F32)
    col_ids = (pos % GRID_W).astype(F32)
    n_freq = QK_ROPE // 4
    inv = ROPE_THETA ** (-jnp.arange(n_freq, dtype=F32) / n_freq)
    ang = jnp.concatenate([row_ids[:, None] * inv, col_ids[:, None] * inv], axis=-1)
    cos, sin = jnp.cos(ang), jnp.sin(ang)
    latent = jnp.concatenate([cos, cos, -sin, sin], axis=-1)
    ctx = jnp.concatenate([jnp.ones((CTX_LEN, QK_ROPE), F32), jnp.zeros((CTX_LEN, QK_ROPE), F32)], axis=-1)
    one = jnp.concatenate([ctx, latent], axis=0)
    return jnp.tile(one, (BATCH, 1))


def _routing(logits):
    probs = jax.nn.softmax(logits, axis=-1)
    top_p, top_i = lax.top_k(probs, TOP_K)
    top_p = top_p / jnp.sum(top_p, axis=-1, keepdims=True)
    onehot = jax.nn.one_hot(top_i.reshape(N_ASSIGN), N_EXPERTS, dtype=jnp.int32)
    csum = jnp.cumsum(onehot, axis=0)
    rank = jnp.sum((csum - onehot) * onehot, axis=1)
    counts = csum[-1]
    n_segs = (counts + SEG_ROWS - 1) // SEG_ROWS
    seg_start = jnp.cumsum(n_segs) - n_segs
    slot = jnp.sum(onehot * seg_start[None, :], axis=1) * SEG_ROWS + rank
    seg_ids = jnp.arange(MOE_SEGS, dtype=jnp.int32)
    used = jnp.sum(n_segs)
    seg_expert = jnp.sum((seg_ids[:, None] >= (seg_start + n_segs)[None, :]).astype(jnp.int32), axis=1)
    last_used_expert = jnp.max(jnp.where(counts > 0, jnp.arange(N_EXPERTS), 0))
    seg_expert = jnp.where(seg_ids < used, jnp.minimum(seg_expert, N_EXPERTS - 1), last_used_expert)
    rows_left = counts[seg_expert] - (seg_ids - seg_start[seg_expert]) * SEG_ROWS
    seg_tiles = jnp.clip((rows_left + FFN_TILE - 1) // FFN_TILE, 0, FFN_TILES_PER_SEG)
    seg_tiles = jnp.where(seg_ids < used, seg_tiles, 0)
    return (seg_expert.astype(jnp.int32), seg_tiles.astype(jnp.int32),
            slot.astype(jnp.int32).reshape(BATCH * SEQ, TOP_K), top_p)


def kernel(x, c, ctx, c_ctx, w_mod, b_mod, g_mix_pre, g_mix_post, g_ffn_pre, g_ffn_post, w_in, b_gate, g_q_lat, g_kv_lat, w_q_up, w_kv_up, w_pool, pool_scale, w_out, w_ff_gate, w_ff_up, w_ff_down, w_router, b_router, w_exp_gate, w_exp_up, w_exp_down):
    x2 = x.reshape(BATCH * SEQ, D_MODEL)
    ctx2 = ctx.reshape(BATCH * CTX_LEN, D_MODEL)

    c8 = jnp.concatenate([c, c_ctx[None, :], jnp.zeros((8 - BATCH - 1, D_MODEL), F32)], axis=0)
    mods = _modulation(c8, w_mod, b_mod)
    mods3 = mods[:, :BATCH + 1].reshape(DEPTH * 3 * 6, 1, D_MODEL)

    as_vec = lambda a: a.reshape(DEPTH, 1, a.shape[-1])
    g_mix_pre, g_mix_post, g_ffn_pre, g_ffn_post = map(as_vec, (g_mix_pre, g_mix_post, g_ffn_pre, g_ffn_post))
    g_q_lat, g_kv_lat, pool_scale = map(as_vec, (g_q_lat, g_kv_lat, pool_scale))
    b_gate2 = b_gate.reshape(DEPTH * 2, 1, D_MODEL)
    cs = _rope_table()

    w_in_t = jnp.swapaxes(w_in, 1, 2)
    w_pool = w_pool.astype(BF16)

    h = _norm_mod(x2, ctx2, g_mix_pre, mods3, 0)
    xr = None
    out = None
    for l in range(DEPTH):
        last = l == DEPTH - 1
        p = _in_proj(h, w_in_t, l)
        q = _q_up(p, g_q_lat, w_q_up, cs, l)
        k, v = _kv_up(p, g_kv_lat, w_kv_up, h, w_in_t, cs, l)

        attn = _attention(q, k, v, with_ctx=not last).reshape(-1, D_MODEL)
        tile_fn, n_tiles = (_tile_latent, N_TILES_LATENT) if last else (_tile_all, N_TILES_ALL)
        pool = _pool(p, w_pool, pool_scale, l, tile_fn, n_tiles)
        x_args = (x2, ctx2) if l == 0 else (xr,)
        merge = functools.partial(_merge, p, attn, pool, b_gate2, w_out, x_args, g_mix_post,
                                  g_ffn_pre, mods3, l, tile_fn, n_tiles)

        if not last:
            xr, h2 = merge()
            n_seg = ROWS // SEG_ROWS
            seg_expert = jnp.zeros((n_seg,), jnp.int32)
            seg_tiles = jnp.full((n_seg,), FFN_TILES_PER_SEG, jnp.int32)
            y = _ffn(seg_expert, seg_tiles, h2, w_ff_gate, w_ff_up, w_ff_down)
            xr, h = _post(y, xr, g_ffn_post, g_mix_pre, mods3, l)
        else:
            w_r = jnp.pad(w_router[0], ((0, 0), (0, LANES - N_EXPERTS)))
            b_r = jnp.pad(b_router[0], (0, LANES - N_EXPERTS)).reshape(1, LANES)
            xl, h2, logits = merge(router=(w_r, b_r))
            seg_expert, seg_tiles, slot, weight = _routing(logits[:, :N_EXPERTS])
            slot3 = slot.reshape(N_TILES_LATENT, 1, ROW_TILE * TOP_K)
            xs = _dispatch(slot3, h2)
            ys = _ffn(seg_expert, seg_tiles, xs, w_exp_gate.reshape(N_EXPERTS, D_MODEL, D_FF),
                      w_exp_up.reshape(N_EXPERTS, D_MODEL, D_FF), w_exp_down.reshape(N_EXPERTS, D_FF, D_MODEL))
            out = _combine(slot3, weight, xl, g_ffn_post, mods3, l, ys)
    return out.reshape(BATCH, SEQ, D_MODEL)
```

```python
import functools
import math

import jax
import jax.numpy as jnp
from jax import lax
from jax.experimental import pallas as pl
from jax.experimental.pallas import tpu as pltpu

F32 = jnp.float32
BF16 = jnp.bfloat16

D_MODEL = 2048
BATCH = 2
SEQ = 4096
DEPTH = 2
GRID_W = 64
CTX_LEN = 256
EPS = 1e-6

QK_NOPE = 128
QK_ROPE = 64
V_DIM = 128
N_HEADS = 16
Q_LORA = 512
KV_LORA = 512
ROPE_THETA = 10000.0
ATTN_SCALE = (QK_NOPE + QK_ROPE) ** -0.5
Q_PRESCALE = ATTN_SCALE * math.log2(math.e)

POOL_WINDOWS = (2, 4, 8, 16)
N_POOL_GROUPS = 4
POOL_WIDTH = 1024
POOL_GROUP = 256
POOL_OUT_GROUP = 512

OFF_KR = 1024
LANES = 128

D_FF = 7168
N_EXPERTS = 8
TOP_K = 2

T_ROWS = CTX_LEN + SEQ
ROWS = BATCH * T_ROWS
ROW_TILE = 256
TILES_PER_BATCH = T_ROWS // ROW_TILE
LATENT_TILES = SEQ // ROW_TILE
N_TILES_ALL = BATCH * TILES_PER_BATCH
N_TILES_LATENT = BATCH * LATENT_TILES
HALO = 16
K_WIDTH = 256
HEADS_PER_STEP = 8

COL_Q, COL_KV, COL_POOL, COL_GATE = 0, 512, 1024, 2048

FFN_TILE = 272
FFN_TILES_PER_SEG = 4
SEG_ROWS = FFN_TILE * FFN_TILES_PER_SEG
FFN_CHUNK = 512
N_ASSIGN = BATCH * SEQ * TOP_K
MOE_SEGS = N_ASSIGN // SEG_ROWS + N_EXPERTS

VMEM_LIMIT = 56 * 1024 * 1024


def _cparams(sem, vmem=None):
    return pltpu.CompilerParams(dimension_semantics=sem, vmem_limit_bytes=vmem)


def _tile_all(i):
    return i


def _tile_latent(i):
    return (i // LATENT_TILES) * TILES_PER_BATCH + 1 + i % LATENT_TILES


def _group_of_tile(t):
    return jnp.where(t % TILES_PER_BATCH == 0, BATCH, t // TILES_PER_BATCH)


def _mod_spec(layer, chunk, tile_fn):
    def index(i):
        return ((layer * 3 + _group_of_tile(tile_fn(i))) * 6 + chunk, 0, 0)
    return pl.BlockSpec((None, 1, D_MODEL), index)


def _vec_spec(layer, width=D_MODEL):
    return pl.BlockSpec((None, 1, width), lambda i: (layer, 0, 0))


def _input_stream_specs():
    def x_index(t):
        return ((t // TILES_PER_BATCH) * LATENT_TILES + jnp.maximum(t % TILES_PER_BATCH - 1, 0), 0)

    def ctx_index(t):
        return (t // TILES_PER_BATCH, 0)

    return (pl.BlockSpec((ROW_TILE, D_MODEL), x_index), pl.BlockSpec((CTX_LEN, D_MODEL), ctx_index))


def _stream_tile(x_ref, ctx_ref, tile):
    return jnp.where(tile % TILES_PER_BATCH == 0, ctx_ref[...], x_ref[...])


def _rmsnorm(x, g):
    return x * lax.rsqrt(jnp.mean(x * x, axis=-1, keepdims=True) + EPS) * g


def _mod_kernel(c_ref, w_ref, b_ref, o_ref):
    c = c_ref[...]
    a = c * jax.nn.sigmoid(c)
    o_ref[...] = jnp.dot(a, w_ref[...], preferred_element_type=F32) + b_ref[...]


def _modulation(c8, w_mod, b_mod):
    n = w_mod.shape[2]
    tn = 1024
    return pl.pallas_call(
        _mod_kernel,
        grid=(DEPTH, n // tn),
        in_specs=[pl.BlockSpec((8, D_MODEL), lambda l, j: (0, 0)),
                  pl.BlockSpec((None, D_MODEL, tn), lambda l, j: (l, 0, j)),
                  pl.BlockSpec((None, 1, tn), lambda l, j: (l, 0, j))],
        out_specs=pl.BlockSpec((None, 8, tn), lambda l, j: (l, 0, j)),
        out_shape=jax.ShapeDtypeStruct((DEPTH, 8, n), F32),
        compiler_params=_cparams(("arbitrary", "arbitrary"), VMEM_LIMIT),
        name="modulation",
    )(c8, w_mod, b_mod.reshape(DEPTH, 1, n))


def _norm_mod_kernel(x_ref, ctx_ref, g_ref, sh_ref, sc_ref, o_ref):
    y = _rmsnorm(_stream_tile(x_ref, ctx_ref, pl.program_id(0)), g_ref[...])
    o_ref[...] = (y * (1.0 + sc_ref[...]) + sh_ref[...]).astype(o_ref.dtype)


def _norm_mod(x2, ctx2, g, mods3, layer):
    x_spec, ctx_spec = _input_stream_specs()
    return pl.pallas_call(
        _norm_mod_kernel,
        grid=(N_TILES_ALL,),
        in_specs=[x_spec, ctx_spec, _vec_spec(layer),
                  _mod_spec(layer, 0, _tile_all), _mod_spec(layer, 1, _tile_all)],
        out_specs=pl.BlockSpec((ROW_TILE, D_MODEL), lambda i: (i, 0)),
        out_shape=jax.ShapeDtypeStruct((ROWS, D_MODEL), BF16),
        compiler_params=_cparams(("arbitrary",)),
        name="norm_mod",
    )(x2, ctx2, g, mods3, mods3)


_CONTRACT_LAST = (((1,), (1,)), ((), ()))
IN_TILE = 1024
IN_ROWS = 1088
MAIN_WIDTH = Q_LORA + KV_LORA + POOL_WIDTH + 2 * D_MODEL


def _in_proj_kernel(layer, a_ref, wt_ref, o_ref, wbuf_ref, wcast_ref, sem):
    j = pl.program_id(0)

    def fetch(tile):
        row0 = tile * IN_TILE + jnp.where(tile * IN_TILE >= OFF_KR, QK_ROPE, 0)
        rows = pl.ds(pl.multiple_of(row0, QK_ROPE), IN_TILE)
        slot = tile % 2
        return pltpu.make_async_copy(wt_ref.at[layer, rows], wbuf_ref.at[slot], sem.at[slot])

    @pl.when(pl.program_id(1) == 0)
    def _():
        @pl.when(j == 0)
        def _():
            fetch(j).start()

        @pl.when(j + 1 < pl.num_programs(0))
        def _():
            fetch(j + 1).start()

        fetch(j).wait()
        wcast_ref[...] = wbuf_ref[j % 2].astype(wcast_ref.dtype)

    y = lax.dot_general(a_ref[...], wcast_ref[...], _CONTRACT_LAST, preferred_element_type=F32)
    o_ref[...] = y.astype(o_ref.dtype)


def _in_proj(a, w_in_t, layer):
    m, k = a.shape
    assert OFF_KR % IN_TILE == 0 and MAIN_WIDTH % IN_TILE == 0 and m % IN_ROWS == 0
    return pl.pallas_call(
        functools.partial(_in_proj_kernel, layer),
        grid=(MAIN_WIDTH // IN_TILE, m // IN_ROWS),
        in_specs=[pl.BlockSpec((IN_ROWS, k), lambda j, i: (i, 0)),
                  pl.BlockSpec(memory_space=pl.ANY)],
        out_specs=pl.BlockSpec((IN_ROWS, IN_TILE), lambda j, i: (i, j)),
        out_shape=jax.ShapeDtypeStruct((m, MAIN_WIDTH), BF16),
        scratch_shapes=[pltpu.VMEM((2, IN_TILE, k), F32), pltpu.VMEM((IN_TILE, k), BF16),
                        pltpu.SemaphoreType.DMA((2,))],
        compiler_params=_cparams(("arbitrary", "arbitrary"), VMEM_LIMIT),
        name="in_proj",
    )(a, w_in_t)


def _rope_rotate(t, cs):
    lane = lax.broadcasted_iota(jnp.int32, t.shape, 1)
    s_tab = pltpu.roll(cs, QK_ROPE, 1)
    return jnp.where(lane < QK_ROPE, t * cs + _swap_rope_halves(t) * s_tab, 0.0)


def _swap_rope_halves(t):
    half = QK_ROPE // 2
    lane = lax.broadcasted_iota(jnp.int32, t.shape, 1)
    return jnp.where(lane < half, pltpu.roll(t, LANES - half, 1), pltpu.roll(t, half, 1))


def _q_up_kernel(p_ref, g_ref, w_ref, cs_ref, q_ref, w2_ref):
    head = QK_NOPE + QK_ROPE

    @pl.when(pl.program_id(0) == 0)
    def _():
        lane = lax.broadcasted_iota(jnp.int32, (Q_LORA, LANES), 1)
        w = w_ref[...]
        for h in range(N_HEADS):
            base = h * head
            w2_ref[:, h * K_WIDTH:h * K_WIDTH + QK_NOPE] = w[:, base:base + QK_NOPE].astype(w2_ref.dtype)
            tail = pltpu.roll(w[:, base + head - LANES:base + head], QK_ROPE, 1)
            both = jnp.where(lane < QK_ROPE, tail, pltpu.roll(_swap_rope_halves(tail), QK_ROPE, 1))
            w2_ref[:, h * K_WIDTH + QK_NOPE:(h + 1) * K_WIDTH] = both.astype(w2_ref.dtype)

    n = _rmsnorm(p_ref[...].astype(F32), g_ref[...]).astype(BF16)
    y = jnp.dot(n, w2_ref[...], preferred_element_type=F32) * Q_PRESCALE
    cs = cs_ref[...]
    lane = lax.broadcasted_iota(jnp.int32, (ROW_TILE, LANES), 1)
    for h in range(N_HEADS):
        yh = y[:, h * K_WIDTH:(h + 1) * K_WIDTH]
        q_ref[h, :, :QK_NOPE] = yh[:, :QK_NOPE].astype(q_ref.dtype)
        t = yh[:, QK_NOPE:] * cs
        q_ref[h, :, QK_NOPE:] = jnp.where(lane < QK_ROPE, t + pltpu.roll(t, QK_ROPE, 1), 0.0).astype(q_ref.dtype)


def _head_index(i):
    return (i // TILES_PER_BATCH, 0, i % TILES_PER_BATCH, 0)


def _q_up(p, g_q, w_q_up, cs, layer):
    return pl.pallas_call(
        _q_up_kernel,
        grid=(N_TILES_ALL,),
        in_specs=[pl.BlockSpec((ROW_TILE, Q_LORA), lambda i: (i, COL_Q // Q_LORA)),
                  _vec_spec(layer, Q_LORA),
                  pl.BlockSpec((None, Q_LORA, N_HEADS * (QK_NOPE + QK_ROPE)), lambda i: (layer, 0, 0)),
                  pl.BlockSpec((ROW_TILE, LANES), lambda i: (i, 0))],
        out_specs=pl.BlockSpec((None, N_HEADS, ROW_TILE, K_WIDTH), _head_index),
        out_shape=jax.ShapeDtypeStruct((BATCH, N_HEADS, T_ROWS, K_WIDTH), BF16),
        scratch_shapes=[pltpu.VMEM((Q_LORA, N_HEADS * K_WIDTH), BF16)],
        compiler_params=_cparams(("arbitrary",), VMEM_LIMIT),
        name="q_up",
    )(p, g_q, w_q_up, cs)


def _kv_up_kernel(p_ref, g_ref, w_ref, h_ref, wkr_ref, cs_ref, k_ref, v_ref, wb_ref):
    @pl.when(pl.program_id(0) == 0)
    def _():
        wb_ref[...] = w_ref[...].astype(wb_ref.dtype)

    n = _rmsnorm(p_ref[...].astype(F32), g_ref[...]).astype(BF16)
    y = jnp.dot(n, wb_ref[...], preferred_element_type=F32)
    kr = lax.dot_general(h_ref[...], wkr_ref[...].astype(BF16), _CONTRACT_LAST, preferred_element_type=F32)
    kr = _rope_rotate(kr, cs_ref[...]).astype(k_ref.dtype)
    for h in range(N_HEADS):
        base = h * (QK_NOPE + V_DIM)
        k_ref[h, :, :QK_NOPE] = y[:, base:base + QK_NOPE].astype(k_ref.dtype)
        k_ref[h, :, QK_NOPE:] = kr
        v_ref[h] = y[:, base + QK_NOPE:base + QK_NOPE + V_DIM].astype(v_ref.dtype)


def _kv_up(p, g_kv, w_kv, h, w_in_t, cs, layer):
    return pl.pallas_call(
        _kv_up_kernel,
        grid=(N_TILES_ALL,),
        in_specs=[pl.BlockSpec((ROW_TILE, KV_LORA), lambda i: (i, COL_KV // KV_LORA)),
                  _vec_spec(layer, KV_LORA),
                  pl.BlockSpec((None, KV_LORA, N_HEADS * (QK_NOPE + V_DIM)), lambda i: (layer, 0, 0)),
                  pl.BlockSpec((ROW_TILE, D_MODEL), lambda i: (i, 0)),
                  pl.BlockSpec((None, LANES, D_MODEL), lambda i: (layer, OFF_KR // LANES, 0)),
                  pl.BlockSpec((ROW_TILE, LANES), lambda i: (i, 0))],
        out_specs=[pl.BlockSpec((None, N_HEADS, ROW_TILE, K_WIDTH), _head_index),
                   pl.BlockSpec((None, N_HEADS, ROW_TILE, V_DIM), _head_index)],
        out_shape=[jax.ShapeDtypeStruct((BATCH, N_HEADS, T_ROWS, K_WIDTH), BF16),
                   jax.ShapeDtypeStruct((BATCH, N_HEADS, T_ROWS, V_DIM), BF16)],
        scratch_shapes=[pltpu.VMEM((KV_LORA, N_HEADS * (QK_NOPE + V_DIM)), BF16)],
        compiler_params=_cparams(("arbitrary",), VMEM_LIMIT),
        name="kv_up",
    )(p, g_kv, w_kv, h, w_in_t, cs)


def _attend(q, k, v):
    s = lax.dot_general(q, k, (((1,), (1,)), ((), ())), preferred_element_type=F32)
    m = jnp.max(s, axis=-1, keepdims=True)
    e = jnp.exp2(s - m)
    denom = jnp.sum(e, axis=-1, keepdims=True)
    o = jnp.dot(e.astype(BF16), v, preferred_element_type=F32)
    return o / denom


def _attention_kernel(with_ctx, q_ref, k_ref, v_ref, o_ref):
    def run(n_keys):
        for h in range(HEADS_PER_STEP):
            o = _attend(q_ref[h], k_ref[h, :n_keys, :], v_ref[h, :n_keys, :])
            o_ref[:, h * V_DIM:(h + 1) * V_DIM] = o.astype(o_ref.dtype)

    if not with_ctx:
        run(T_ROWS)
        return

    pl.when(pl.program_id(2) == 0)(functools.partial(run, CTX_LEN))
    pl.when(pl.program_id(2) > 0)(functools.partial(run, T_ROWS))


def _attention(q, k, v, with_ctx):
    q0 = 0 if with_ctx else 1
    n_q = TILES_PER_BATCH - q0
    hps = HEADS_PER_STEP
    return pl.pallas_call(
        functools.partial(_attention_kernel, with_ctx),
        grid=(BATCH, N_HEADS // hps, n_q),
        in_specs=[pl.BlockSpec((None, hps, ROW_TILE, K_WIDTH), lambda b, h, i: (b, h, i + q0, 0)),
                  pl.BlockSpec((None, hps, T_ROWS, K_WIDTH), lambda b, h, i: (b, h, 0, 0),
                               pipeline_mode=pl.Buffered(1)),
                  pl.BlockSpec((None, hps, T_ROWS, V_DIM), lambda b, h, i: (b, h, 0, 0))],
        out_specs=pl.BlockSpec((None, ROW_TILE, hps * V_DIM), lambda b, h, i: (b, i, h)),
        out_shape=jax.ShapeDtypeStruct((BATCH, n_q * ROW_TILE, D_MODEL), BF16),
        compiler_params=_cparams(("arbitrary", "arbitrary", "arbitrary"), VMEM_LIMIT),
        name="attention",
    )(q, k, v)


def _pool_kernel(tile_fn, prev_ref, cur_ref, next_ref, w_ref, ps_ref, o_ref, buf_ref):
    t = tile_fn(pl.program_id(0)) % TILES_PER_BATCH
    is_ctx = t == 0
    seg_len = jnp.where(is_ctx, CTX_LEN, SEQ)
    pos0 = jnp.where(is_ctx, 0, (t - 1) * ROW_TILE)
    has_prev = pos0 > 0
    has_next = pos0 + ROW_TILE < seg_len
    buf_ref[0:HALO, :] = jnp.where(has_prev, prev_ref[...].astype(F32), 0.0)
    buf_ref[HALO:HALO + ROW_TILE, :] = cur_ref[...].astype(F32)
    buf_ref[HALO + ROW_TILE:, :] = jnp.where(has_next, next_ref[...].astype(F32), 0.0)

    pos = pos0 + lax.broadcasted_iota(jnp.int32, (ROW_TILE, 1), 0)
    for g, win in enumerate(POOL_WINDOWS):
        cols = slice(g * POOL_GROUP, (g + 1) * POOL_GROUP)
        half = win // 2
        acc = buf_ref[HALO - half:HALO - half + ROW_TILE, cols]
        for j in range(1, win):
            acc = acc + buf_ref[HALO - half + j:HALO - half + j + ROW_TILE, cols]
        lo = jnp.maximum(pos - half, 0)
        hi = jnp.minimum(pos - half + win, seg_len)
        mean = acc / (hi - lo).astype(F32)
        pooled = (mean - buf_ref[HALO:HALO + ROW_TILE, cols]).astype(BF16)
        out = jnp.dot(pooled, w_ref[g], preferred_element_type=F32)
        ocols = slice(g * POOL_OUT_GROUP, (g + 1) * POOL_OUT_GROUP)
        o_ref[:, ocols] = (out * ps_ref[:, ocols]).astype(o_ref.dtype)


def _pool(p, w_pool, pool_scale, layer, tile_fn, n_tiles):
    per16 = ROW_TILE // HALO
    last16 = ROWS // HALO - 1
    col = COL_POOL // POOL_WIDTH
    return pl.pallas_call(
        functools.partial(_pool_kernel, tile_fn),
        grid=(n_tiles,),
        in_specs=[pl.BlockSpec((HALO, POOL_WIDTH), lambda i: (jnp.maximum(tile_fn(i) * per16 - 1, 0), col)),
                  pl.BlockSpec((ROW_TILE, POOL_WIDTH), lambda i: (tile_fn(i), col)),
                  pl.BlockSpec((HALO, POOL_WIDTH), lambda i: (jnp.minimum((tile_fn(i) + 1) * per16, last16), col)),
                  pl.BlockSpec((None, N_POOL_GROUPS, POOL_GROUP, POOL_OUT_GROUP), lambda i: (layer, 0, 0, 0)),
                  _vec_spec(layer)],
        out_specs=pl.BlockSpec((ROW_TILE, D_MODEL), lambda i: (i, 0)),
        out_shape=jax.ShapeDtypeStruct((n_tiles * ROW_TILE, D_MODEL), BF16),
        scratch_shapes=[pltpu.VMEM((ROW_TILE + 2 * HALO, POOL_WIDTH), F32)],
        compiler_params=_cparams(("arbitrary",)),
        name="pool",
    )(p, p, p, w_pool, pool_scale)


def _split_bf16(a):
    hi = a.astype(BF16)
    return hi, (a - hi.astype(F32)).astype(BF16)


def _merge_kernel(first, with_router, *refs):
    refs = list(refs)
    wb_ref = refs.pop()
    ga_ref, gb_ref, attn_ref, pool_ref, ba_ref, bb_ref, w_ref = refs[:7]

    @pl.when(pl.program_id(0) == 0)
    def _():
        wb_ref[...] = w_ref[...].astype(wb_ref.dtype)

    del refs[:7]
    if first:
        x = _stream_tile(refs[0], refs[1], pl.program_id(0))
        del refs[:2]
    else:
        x = refs.pop(0)[...]
    gpost_ref, g1_ref, gpre_ref, sh_ref, sc_ref = refs[:5]
    del refs[:5]
    if with_router:
        wr_ref, br_ref, xo_ref, h_ref, lg_ref = refs
    else:
        xo_ref, h_ref = refs

    ga = jax.nn.sigmoid(ga_ref[...].astype(F32) + ba_ref[...])
    gb = jax.nn.sigmoid(gb_ref[...].astype(F32) + bb_ref[...])
    mixed = ga * attn_ref[...].astype(F32) + gb * pool_ref[...].astype(F32)
    y = jnp.dot(mixed.astype(BF16), wb_ref[...], preferred_element_type=F32)
    x = x + g1_ref[...] * _rmsnorm(y, gpost_ref[...])
    xo_ref[...] = x
    h = _rmsnorm(x, gpre_ref[...]) * (1.0 + sc_ref[...]) + sh_ref[...]
    h_ref[...] = h.astype(BF16).astype(h_ref.dtype)
    if with_router:
        h_hi, h_lo = _split_bf16(h)
        w_hi, w_lo = _split_bf16(wr_ref[...])
        dot = functools.partial(jnp.dot, preferred_element_type=F32)
        lg_ref[...] = dot(h_hi, w_hi) + (dot(h_lo, w_hi) + dot(h_hi, w_lo)) + br_ref[...]


def _merge(p, attn, pool, b_gate2, w_out, x_args, g_post, g_pre, mods3, layer, tile_fn, n_tiles,
           router=None):
    first = len(x_args) == 2
    gate_a = COL_GATE // D_MODEL
    act = pl.BlockSpec((ROW_TILE, D_MODEL), lambda i: (i, 0))
    in_specs = [pl.BlockSpec((ROW_TILE, D_MODEL), lambda i: (tile_fn(i), gate_a)),
                pl.BlockSpec((ROW_TILE, D_MODEL), lambda i: (tile_fn(i), gate_a + 1)),
                act, act,
                pl.BlockSpec((None, 1, D_MODEL), lambda i: (layer * 2, 0, 0)),
                pl.BlockSpec((None, 1, D_MODEL), lambda i: (layer * 2 + 1, 0, 0)),
                pl.BlockSpec((None, D_MODEL, D_MODEL), lambda i: (layer, 0, 0), pipeline_mode=pl.Buffered(1))]
    if first:
        in_specs += list(_input_stream_specs())
    else:
        in_specs.append(pl.BlockSpec((ROW_TILE, D_MODEL), lambda i: (tile_fn(i), 0)))
    in_specs += [_vec_spec(layer), _mod_spec(layer, 2, tile_fn), _vec_spec(layer),
                 _mod_spec(layer, 3, tile_fn), _mod_spec(layer, 4, tile_fn)]
    args = [p, p, attn, pool, b_gate2, b_gate2, w_out, *x_args, g_post, mods3, g_pre, mods3, mods3]
    rows = n_tiles * ROW_TILE
    out_specs = [act, act]
    h_dtype = BF16 if router is None else F32
    out_shape = [jax.ShapeDtypeStruct((rows, D_MODEL), F32), jax.ShapeDtypeStruct((rows, D_MODEL), h_dtype)]
    if router is not None:
        in_specs += [pl.BlockSpec((D_MODEL, LANES), lambda i: (0, 0)), pl.BlockSpec((1, LANES), lambda i: (0, 0))]
        args += list(router)
        out_specs.append(pl.BlockSpec((ROW_TILE, LANES), lambda i: (i, 0)))
        out_shape.append(jax.ShapeDtypeStruct((rows, LANES), F32))
    return pl.pallas_call(
        functools.partial(_merge_kernel, first, router is not None),
        grid=(n_tiles,),
        in_specs=in_specs,
        out_specs=out_specs,
        out_shape=out_shape,
        scratch_shapes=[pltpu.VMEM((D_MODEL, D_MODEL), BF16)],
        compiler_params=_cparams(("arbitrary",), VMEM_LIMIT),
        name="merge_out_proj",
    )(*args)


DMA_UNROLL = 8


def _dispatch_kernel(slot_ref, h_ref, xs_in_ref, xs_ref, sem):
    del xs_in_ref

    def start(r, carry):
        for k in range(TOP_K):
            dst = slot_ref[0, r * TOP_K + k]
            pltpu.make_async_copy(h_ref.at[pl.ds(r, 1)], xs_ref.at[pl.ds(dst, 1)], sem).start(priority=k % 2)
        return carry

    lax.fori_loop(0, ROW_TILE, start, 0, unroll=DMA_UNROLL)
    for _ in range(TOP_K):
        pltpu.make_async_copy(h_ref, xs_ref.at[pl.ds(0, ROW_TILE)], sem).wait()


def _dispatch(slot3, h2):
    xs_init = jnp.zeros((MOE_SEGS * SEG_ROWS, D_MODEL), h2.dtype)
    return pl.pallas_call(
        _dispatch_kernel,
        grid=(N_TILES_LATENT,),
        in_specs=[pl.BlockSpec((None, 1, ROW_TILE * TOP_K), lambda i: (i, 0, 0), memory_space=pltpu.SMEM),
                  pl.BlockSpec((ROW_TILE, D_MODEL), lambda i: (i, 0)),
                  pl.BlockSpec(memory_space=pl.ANY)],
        out_specs=pl.BlockSpec(memory_space=pl.ANY),
        out_shape=jax.ShapeDtypeStruct(xs_init.shape, xs_init.dtype),
        input_output_aliases={2: 0},
        scratch_shapes=[pltpu.SemaphoreType.DMA(())],
        compiler_params=_cparams(("arbitrary",)),
        name="moe_dispatch",
    )(slot3, h2, xs_init)


def _ffn_kernel(se_ref, st_ref, sb_ref, x_ref, wg_ref, wu_ref, wd_ref, o_ref):
    del se_ref, sb_ref
    n_tiles = st_ref[pl.program_id(0)]

    @pl.when(jnp.logical_and(pl.program_id(1) == 0, n_tiles > 0))
    def _():
        o_ref[...] = jnp.zeros_like(o_ref)

    def rows_step(row0, n_rows):
        rows = pl.ds(row0 if isinstance(row0, int) else pl.multiple_of(row0, 16), n_rows)
        x = x_ref[rows, :].astype(BF16)
        g = jnp.dot(x, wg_ref[...].astype(BF16), preferred_element_type=F32)
        u = jnp.dot(x, wu_ref[...].astype(BF16), preferred_element_type=F32)
        a = (g * jax.nn.sigmoid(g)) * u
        o_ref[rows, :] += jnp.dot(a.astype(BF16), wd_ref[...].astype(BF16), preferred_element_type=F32)

    @pl.when(n_tiles == FFN_TILES_PER_SEG)
    def _():
        rows_step(0, SEG_ROWS)

    @pl.when(n_tiles < FFN_TILES_PER_SEG)
    def _():
        def pair(i, carry):
            rows_step(i * (2 * FFN_TILE), 2 * FFN_TILE)
            return carry

        lax.fori_loop(0, lax.shift_right_logical(n_tiles, 1), pair, 0)

        @pl.when((n_tiles & 1) == 1)
        def _():
            rows_step((n_tiles - 1) * FFN_TILE, FFN_TILE)


def _ffn(seg_expert, seg_tiles, xs, wg, wu, wd):
    n_seg = xs.shape[0] // SEG_ROWS
    in_place = xs.dtype == F32
    tf = FFN_CHUNK
    n_chunks = D_FF // tf
    x_mode = dict(pipeline_mode=pl.Buffered(1)) if in_place else {}
    seg_block = jnp.minimum(jnp.arange(n_seg, dtype=jnp.int32), jnp.sum((seg_tiles > 0).astype(jnp.int32)) - 1)

    def chunk(s, f, st):
        return jnp.where(st[s] > 0, f, n_chunks - 1)

    grid_spec = pltpu.PrefetchScalarGridSpec(
        num_scalar_prefetch=3,
        grid=(n_seg, n_chunks),
        in_specs=[pl.BlockSpec((SEG_ROWS, D_MODEL), lambda s, f, se, st, sb: (sb[s], 0), **x_mode),
                  pl.BlockSpec((None, D_MODEL, tf), lambda s, f, se, st, sb: (se[s], 0, chunk(s, f, st))),
                  pl.BlockSpec((None, D_MODEL, tf), lambda s, f, se, st, sb: (se[s], 0, chunk(s, f, st))),
                  pl.BlockSpec((None, tf, D_MODEL), lambda s, f, se, st, sb: (se[s], chunk(s, f, st), 0))],
        out_specs=pl.BlockSpec((SEG_ROWS, D_MODEL), lambda s, f, se, st, sb: (sb[s], 0)),
    )
    return pl.pallas_call(
        _ffn_kernel,
        grid_spec=grid_spec,
        out_shape=jax.ShapeDtypeStruct((n_seg * SEG_ROWS, D_MODEL), F32),
        input_output_aliases={3: 0} if in_place else {},
        compiler_params=_cparams(("arbitrary", "arbitrary"), VMEM_LIMIT),
        name="swiglu_ffn",
    )(seg_expert, seg_tiles, seg_block, xs, wg, wu, wd)


def _post_kernel(y_ref, x_ref, gpost_ref, g2_ref, gpre_ref, sh_ref, sc_ref, xo_ref, h_ref):
    x = x_ref[...] + g2_ref[...] * _rmsnorm(y_ref[...], gpost_ref[...])
    xo_ref[...] = x
    h = _rmsnorm(x, gpre_ref[...]) * (1.0 + sc_ref[...]) + sh_ref[...]
    h_ref[...] = h.astype(h_ref.dtype)


def _post(y, x, g_post, g_pre_next, mods3, layer):
    row = pl.BlockSpec((ROW_TILE, D_MODEL), lambda i: (i, 0))
    return pl.pallas_call(
        _post_kernel,
        grid=(N_TILES_ALL,),
        in_specs=[row, row, _vec_spec(layer), _mod_spec(layer, 5, _tile_all),
                  _vec_spec(layer + 1), _mod_spec(layer + 1, 0, _tile_all), _mod_spec(layer + 1, 1, _tile_all)],
        out_specs=[row, row],
        out_shape=[jax.ShapeDtypeStruct((ROWS, D_MODEL), F32), jax.ShapeDtypeStruct((ROWS, D_MODEL), BF16)],
        compiler_params=_cparams(("arbitrary",)),
        name="post_ffn",
    )(y, x, g_post, mods3, g_pre_next, mods3, mods3)


def _combine_kernel(slot_ref, next_slot_ref, w_ref, x_ref, gpost_ref, g2_ref, ys_ref, o_ref, buf_ref, sem):
    i = pl.program_id(0)
    cur = i % 2

    def gather(slots, b):
        def start(r, carry):
            for k in range(TOP_K):
                src = slots[0, r * TOP_K + k]
                pltpu.make_async_copy(ys_ref.at[pl.ds(src, 1)], buf_ref.at[b, pl.ds(k * ROW_TILE + r, 1)],
                                      sem.at[b]).start(priority=k % 2)
            return carry
        lax.fori_loop(0, ROW_TILE, start, 0, unroll=DMA_UNROLL)

    @pl.when(i == 0)
    def _():
        gather(slot_ref, 0)

    @pl.when(i + 1 < pl.num_programs(0))
    def _():
        gather(next_slot_ref, 1 - cur)

    pltpu.make_async_copy(ys_ref.at[pl.ds(0, TOP_K * ROW_TILE)], buf_ref.at[cur], sem.at[cur]).wait()
    y = w_ref[:, 0:1] * buf_ref[cur, :ROW_TILE, :] + w_ref[:, 1:2] * buf_ref[cur, ROW_TILE:, :]
    o_ref[...] = x_ref[...] + g2_ref[...] * _rmsnorm(y, gpost_ref[...])


def _combine(slot3, weight, x, g_post, mods3, layer, ys):
    row = pl.BlockSpec((ROW_TILE, D_MODEL), lambda i: (i, 0))
    slots = lambda index: pl.BlockSpec((None, 1, ROW_TILE * TOP_K), index, memory_space=pltpu.SMEM)
    return pl.pallas_call(
        _combine_kernel,
        grid=(N_TILES_LATENT,),
        in_specs=[slots(lambda i: (i, 0, 0)),
                  slots(lambda i: (jnp.minimum(i + 1, N_TILES_LATENT - 1), 0, 0)),
                  pl.BlockSpec((ROW_TILE, TOP_K), lambda i: (i, 0)),
                  row, _vec_spec(layer), _mod_spec(layer, 5, _tile_latent),
                  pl.BlockSpec(memory_space=pl.ANY)],
        out_specs=row,
        out_shape=jax.ShapeDtypeStruct((BATCH * SEQ, D_MODEL), F32),
        scratch_shapes=[pltpu.VMEM((2, TOP_K * ROW_TILE, D_MODEL), F32), pltpu.SemaphoreType.DMA((2,))],
        compiler_params=_cparams(("arbitrary",), VMEM_LIMIT),
        name="moe_combine",
    )(slot3, slot3, weight, x, g_post, mods3, ys)


def _rope_table():
    pos = jnp.arange(SEQ, dtype=jnp.int32)
    row_ids = (pos // GRID_W).astype(F32)
    col_ids = (pos % GRID_W).astype(F32)
    n_freq = QK_ROPE // 4
    inv = ROPE_THETA ** (-jnp.arange(n_freq, dtype=F32) / n_freq)
    ang = jnp.concatenate([row_ids[:, None] * inv, col_ids[:, None] * inv], axis=-1)
    cos, sin = jnp.cos(ang), jnp.sin(ang)
    latent = jnp.concatenate([cos, cos, -sin, sin], axis=-1)
    ctx = jnp.concatenate([jnp.ones((CTX_LEN, QK_ROPE), F32), jnp.zeros((CTX_LEN, QK_ROPE), F32)], axis=-1)
    one = jnp.concatenate([ctx, latent], axis=0)
    return jnp.tile(one, (BATCH, 1))


def _routing(logits):
    probs = jax.nn.softmax(logits, axis=-1)
    top_p, top_i = lax.top_k(probs, TOP_K)
    top_p = top_p / jnp.sum(top_p, axis=-1, keepdims=True)
    onehot = jax.nn.one_hot(top_i.reshape(N_ASSIGN), N_EXPERTS, dtype=jnp.int32)
    csum = jnp.cumsum(onehot, axis=0)
    rank = jnp.sum((csum - onehot) * onehot, axis=1)
    counts = csum[-1]
    n_segs = (counts + SEG_ROWS - 1) // SEG_ROWS
    seg_start = jnp.cumsum(n_segs) - n_segs
    slot = jnp.sum(onehot * seg_start[None, :], axis=1) * SEG_ROWS + rank
    seg_ids = jnp.arange(MOE_SEGS, dtype=jnp.int32)
    used = jnp.sum(n_segs)
    seg_expert = jnp.sum((seg_ids[:, None] >= (seg_start + n_segs)[None, :]).astype(jnp.int32), axis=1)
    last_used_expert = jnp.max(jnp.where(counts > 0, jnp.arange(N_EXPERTS), 0))
    seg_expert = jnp.where(seg_ids < used, jnp.minimum(seg_expert, N_EXPERTS - 1), last_used_expert)
    rows_left = counts[seg_expert] - (seg_ids - seg_start[seg_expert]) * SEG_ROWS
    seg_tiles = jnp.clip((rows_left + FFN_TILE - 1) // FFN_TILE, 0, FFN_TILES_PER_SEG)
    seg_tiles = jnp.where(seg_ids < used, seg_tiles, 0)
    return (seg_expert.astype(jnp.int32), seg_tiles.astype(jnp.int32),
            slot.astype(jnp.int32).reshape(BATCH * SEQ, TOP_K), top_p)


def kernel(x, c, ctx, c_ctx, w_mod, b_mod, g_mix_pre, g_mix_post, g_ffn_pre, g_ffn_post, w_in, b_gate, g_q_lat, g_kv_lat, w_q_up, w_kv_up, w_pool, pool_scale, w_out, w_ff_gate, w_ff_up, w_ff_down, w_router, b_router, w_exp_gate, w_exp_up, w_exp_down):
    x2 = x.reshape(BATCH * SEQ, D_MODEL)
    ctx2 = ctx.reshape(BATCH * CTX_LEN, D_MODEL)

    c8 = jnp.concatenate([c, c_ctx[None, :], jnp.zeros((8 - BATCH - 1, D_MODEL), F32)], axis=0)
    mods = _modulation(c8, w_mod, b_mod)
    mods3 = mods[:, :BATCH + 1].reshape(DEPTH * 3 * 6, 1, D_MODEL)

    as_vec = lambda a: a.reshape(DEPTH, 1, a.shape[-1])
    g_mix_pre, g_mix_post, g_ffn_pre, g_ffn_post = map(as_vec, (g_mix_pre, g_mix_post, g_ffn_pre, g_ffn_post))
    g_q_lat, g_kv_lat, pool_scale = map(as_vec, (g_q_lat, g_kv_lat, pool_scale))
    b_gate2 = b_gate.reshape(DEPTH * 2, 1, D_MODEL)
    cs = _rope_table()

    w_in_t = jnp.swapaxes(w_in, 1, 2)
    w_pool = w_pool.astype(BF16)

    h = _norm_mod(x2, ctx2, g_mix_pre, mods3, 0)
    xr = None
    out = None
    for l in range(DEPTH):
        last = l == DEPTH - 1
        p = _in_proj(h, w_in_t, l)
        q = _q_up(p, g_q_lat, w_q_up, cs, l)
        k, v = _kv_up(p, g_kv_lat, w_kv_up, h, w_in_t, cs, l)

        attn = _attention(q, k, v, with_ctx=not last).reshape(-1, D_MODEL)
        tile_fn, n_tiles = (_tile_latent, N_TILES_LATENT) if last else (_tile_all, N_TILES_ALL)
        pool = _pool(p, w_pool, pool_scale, l, tile_fn, n_tiles)
        x_args = (x2, ctx2) if l == 0 else (xr,)
        merge = functools.partial(_merge, p, attn, pool, b_gate2, w_out, x_args, g_mix_post,
                                  g_ffn_pre, mods3, l, tile_fn, n_tiles)

        if not last:
            xr, h2 = merge()
            n_seg = ROWS // SEG_ROWS
            seg_expert = jnp.zeros((n_seg,), jnp.int32)
            seg_tiles = jnp.full((n_seg,), FFN_TILES_PER_SEG, jnp.int32)
            y = _ffn(seg_expert, seg_tiles, h2, w_ff_gate, w_ff_up, w_ff_down)
            xr, h = _post(y, xr, g_ffn_post, g_mix_pre, mods3, l)
        else:
            w_r = jnp.pad(w_router[0], ((0, 0), (0, LANES - N_EXPERTS)))
            b_r = jnp.pad(b_router[0], (0, LANES - N_EXPERTS)).reshape(1, LANES)
            xl, h2, logits = merge(router=(w_r, b_r))
            seg_expert, seg_tiles, slot, weight = _routing(logits[:, :N_EXPERTS])
            slot3 = slot.reshape(N_TILES_LATENT, 1, ROW_TILE * TOP_K)
            xs = _dispatch(slot3, h2)
            ys = _ffn(seg_expert, seg_tiles, xs, w_exp_gate.reshape(N_EXPERTS, D_MODEL, D_FF),
                      w_exp_up.reshape(N_EXPERTS, D_MODEL, D_FF), w_exp_down.reshape(N_EXPERTS, D_FF, D_MODEL))
            out = _combine(slot3, weight, xl, g_ffn_post, mods3, l, ys)
    return out.reshape(BATCH, SEQ, D_MODEL)
```
